```python
import jax, jax.numpy as jnp
from jax import lax
import numpy as np

D_MODEL = 2048
BATCH = 2
SEQ = 16384
DEPTH = 2

N_MIXERS = 2
N_LAYERS_A = (DEPTH + 1) // 2
N_LAYERS_B = DEPTH // 2
D_FF = 4 * D_MODEL
PLE_DIM = 256
ROPE_THETA = 500000.0
EPS = 1e-6

GM_WIDTH = D_MODEL
GM_CHUNK = 128
GM_GROUPS = 16
GM_GROUP_DIM = GM_WIDTH // GM_GROUPS

NSA_HEADS = 16
NSA_KV_GROUPS = 4
NSA_HPG = NSA_HEADS // NSA_KV_GROUPS
HEAD_DIM = 128
ROT_DIM = HEAD_DIM // 4
CMP_LEN = 32
CMP_STRIDE = 16
CMP_HIDDEN = 4 * HEAD_DIM
SEL_BLOCK = 64
SEL_TOPK = 16
WINDOW = 512
Q_BLOCK = 128
NSA_IN_DIM = NSA_HEADS * HEAD_DIM + 6 * NSA_KV_GROUPS * HEAD_DIM + 3 * NSA_HEADS
NEG_INF = -1e30
FORCE_SCORE = 1e9

kernel_name = "hybrid_gmlp_nsa_trunk"


def rmsnorm(x, g):
    xf = x.astype(jnp.float32)
    y = xf * lax.rsqrt(jnp.mean(xf * xf, axis=-1, keepdims=True) + EPS)
    return (y * g.astype(jnp.float32)).astype(x.dtype)


def layernorm(x, g, b):
    xf = x.astype(jnp.float32)
    mu = jnp.mean(xf, axis=-1, keepdims=True)
    var = jnp.mean(jnp.square(xf - mu), axis=-1, keepdims=True)
    y = (xf - mu) * lax.rsqrt(var + EPS)
    return (y * g.astype(jnp.float32) + b.astype(jnp.float32)).astype(x.dtype)


def rope(x, pos):
    half = ROT_DIM // 2
    inv = jnp.power(jnp.float32(ROPE_THETA), -jnp.arange(half, dtype=jnp.float32) * 2.0 / ROT_DIM)
    ang = pos.astype(jnp.float32)[:, None] * inv[None, :]
    cos = jnp.cos(ang)[:, None, :]
    sin = jnp.sin(ang)[:, None, :]
    xr = x[..., :ROT_DIM].astype(jnp.float32)
    x1, x2 = xr[..., :half], xr[..., half:]
    rot = jnp.concatenate([x1 * cos - x2 * sin, x2 * cos + x1 * sin], axis=-1).astype(x.dtype)
    return jnp.concatenate([rot, x[..., ROT_DIM:]], axis=-1)


def masked_softmax(s, mask):
    s = jnp.where(mask, s, NEG_INF)
    p = jax.nn.softmax(s, axis=-1)
    return jnp.where(mask, p, 0.0)


def chunked_gmlp(h, w_in, ln_g, ln_b, w_s, b_s, w_out):
    B, T, _ = h.shape
    z = jax.nn.gelu(h @ w_in)
    u, v = jnp.split(z, 2, axis=-1)
    v = layernorm(v, ln_g, ln_b)
    v = v.reshape(B, T // GM_CHUNK, GM_CHUNK, GM_GROUPS, GM_GROUP_DIM)
    causal = jnp.tril(jnp.ones((GM_CHUNK, GM_CHUNK), dtype=bool))
    ws = jnp.where(causal[None], w_s, jnp.zeros((), w_s.dtype))
    sv = jnp.einsum('gts,bcsgd->bctgd', ws, v) + b_s.T[None, None, :, :, None]
    return (u * sv.reshape(B, T, GM_WIDTH)) @ w_out


def compress(k, pe, w1, w2):
    B, T, G, dk = k.shape
    nc = (T - CMP_LEN) // CMP_STRIDE + 1
    idx = np.arange(nc)[:, None] * CMP_STRIDE + np.arange(CMP_LEN)[None, :]
    blk = k[:, idx] + pe[None, None, :, None, :]
    blk = blk.transpose(0, 1, 3, 2, 4).reshape(B, nc, G, CMP_LEN * dk)
    return jax.nn.gelu(blk @ w1) @ w2


def native_sparse_attention(h, w_in, kc_pe, kc_w1, kc_w2, vc_pe, vc_w1, vc_w2, w_out):
    B, T, _ = h.shape
    G, N, Dh, H = NSA_KV_GROUPS, NSA_HPG, HEAD_DIM, NSA_HEADS
    sizes = [H * Dh] + [G * Dh] * 6 + [3 * H]
    splits = [int(s) for s in np.cumsum(sizes)[:-1]]
    q, kc, vc, ks, vs, kw, vw, gl = jnp.split(h @ w_in, splits, axis=-1)
    pos = jnp.arange(T)
    q = (rope(q.reshape(B, T, H, Dh), pos) * (HEAD_DIM ** -0.5)).reshape(B, T, G, N, Dh)
    nc = (T - CMP_LEN) // CMP_STRIDE + 1
    cmp_end_np = np.arange(nc) * CMP_STRIDE + CMP_LEN - 1
    cmp_end = jnp.asarray(cmp_end_np, dtype=jnp.int32)
    kc = rope(compress(kc.reshape(B, T, G, Dh), kc_pe, kc_w1, kc_w2), cmp_end)
    vc = compress(vc.reshape(B, T, G, Dh), vc_pe, vc_w1, vc_w2)
    ks = rope(ks.reshape(B, T, G, Dh), pos)
    kw = rope(kw.reshape(B, T, G, Dh), pos)
    ns = T // SEL_BLOCK
    sel_k = min(SEL_TOPK, ns)
    ks_b = ks.reshape(B, ns, SEL_BLOCK, G, Dh).transpose(0, 3, 1, 2, 4)
    vs_b = vs.reshape(B, ns, SEL_BLOCK, G, Dh).transpose(0, 3, 1, 2, 4)
    pad = ((0, 0), (WINDOW, 0), (0, 0), (0, 0))
    kw_pad = jnp.pad(kw, pad)
    vw_pad = jnp.pad(vw.reshape(B, T, G, Dh), pad)
    gates = jax.nn.sigmoid(gl).reshape(B, T, 3, G, N, 1)
    cs = np.arange(nc)[:, None] * CMP_STRIDE
    ss = np.arange(ns)[None, :] * SEL_BLOCK
    ov = np.clip(np.minimum(cs + CMP_LEN, ss + SEL_BLOCK) - np.maximum(cs, ss), 0, None) / CMP_STRIDE
    cmp_to_sel = jnp.asarray(ov, dtype=jnp.float32)
    bi = jnp.arange(B)[:, None, None, None]
    gi = jnp.arange(G)[None, :, None, None]
    blk_id = jnp.arange(ns)

    def query_block(qb):
        t0 = qb * Q_BLOCK
        tpos = t0 + jnp.arange(Q_BLOCK)
        qblk = lax.dynamic_slice_in_dim(q, t0, Q_BLOCK, axis=1)
        s_c = jnp.einsum('bqgnd,bcgd->bgnqc', qblk, kc, preferred_element_type=jnp.float32)
        p_c = masked_softmax(s_c, cmp_end[None, :] <= tpos[:, None])
        o_c = jnp.einsum('bgnqc,bcgd->bqgnd', p_c.astype(vc.dtype), vc)
        imp = jnp.einsum('bgnqc,cs->bgqs', p_c, cmp_to_sel)
        cur = tpos // SEL_BLOCK
        valid = blk_id[None, :] <= cur[:, None]
        forced = (blk_id[None, :] == 0) | (blk_id[None, :] == cur[:, None]) | (blk_id[None, :] == cur[:, None] - 1)
        score = jnp.where(valid, jnp.where(forced, FORCE_SCORE, imp), NEG_INF)
        top_s, top_i = lax.top_k(score, sel_k)
        kg = ks_b[bi, gi, top_i]
        vg = vs_b[bi, gi, top_i]
        s_s = jnp.einsum('bqgnd,bgqkjd->bgnqkj', qblk, kg, preferred_element_type=jnp.float32)
        kpos = top_i[..., None] * SEL_BLOCK + jnp.arange(SEL_BLOCK)
        m_s = (kpos <= tpos[None, None, :, None, None]) & (top_s > NEG_INF * 0.5)[..., None]
        p_s = masked_softmax(s_s.reshape(B, G, N, Q_BLOCK, -1),
                             m_s.reshape(B, G, 1, Q_BLOCK, -1)).reshape(s_s.shape)
        o_s = jnp.einsum('bgnqkj,bgqkjd->bqgnd', p_s.astype(vg.dtype), vg)
        kwin = lax.dynamic_slice_in_dim(kw_pad, t0, WINDOW + Q_BLOCK, axis=1)
        vwin = lax.dynamic_slice_in_dim(vw_pad, t0, WINDOW + Q_BLOCK, axis=1)
        kp = t0 - WINDOW + jnp.arange(WINDOW + Q_BLOCK)
        m_w = (kp[None, :] <= tpos[:, None]) & (kp[None, :] > tpos[:, None] - WINDOW) & (kp[None, :] >= 0)
        s_w = jnp.einsum('bqgnd,bkgd->bgnqk', qblk, kwin, preferred_element_type=jnp.float32)
        p_w = masked_softmax(s_w, m_w)
        o_w = jnp.einsum('bgnqk,bkgd->bqgnd', p_w.astype(vwin.dtype), vwin)
        g = lax.dynamic_slice_in_dim(gates, t0, Q_BLOCK, axis=1)
        o = g[:, :, 0] * o_c + g[:, :, 1] * o_s + g[:, :, 2] * o_w
        return o.reshape(B, Q_BLOCK, H * Dh)

    out = lax.map(query_block, jnp.arange(T // Q_BLOCK))
    out = out.transpose(1, 0, 2, 3).reshape(B, T, H * Dh)
    return out @ w_out


def sqrelu_mlp(h, w_up, w_down):
    return jnp.square(jax.nn.relu(h @ w_up)) @ w_down


def setup_inputs(seed: int = 0) -> dict:
    key = jax.random.key(seed)
    ks = jax.random.split(key, 32)
    f32 = jnp.float32

    def nrm(k, shape, scale):
        return jax.random.normal(k, shape, f32) * scale

    def gain(k, shape):
        return 1.0 + 0.05 * jax.random.normal(k, shape, f32)

    return {
        "x": nrm(ks[0], (BATCH, SEQ, D_MODEL), 1.0),
        "p": nrm(ks[1], (DEPTH, BATCH, SEQ, PLE_DIM), 1.0),
        "norm_mix": gain(ks[2], (DEPTH, D_MODEL)),
        "norm_ffn": gain(ks[3], (DEPTH, D_MODEL)),
        "norm_ple": gain(ks[4], (DEPTH, D_MODEL)),
        "ffn_up": nrm(ks[5], (DEPTH, D_MODEL, D_FF), D_MODEL ** -0.5),
        "ffn_down": nrm(ks[6], (DEPTH, D_FF, D_MODEL), D_FF ** -0.5),
        "ple_proj": nrm(ks[7], (DEPTH, PLE_DIM, D_MODEL), PLE_DIM ** -0.5),
        "ple_gate": nrm(ks[8], (DEPTH, D_MODEL, D_MODEL), D_MODEL ** -0.5),
        "gm_in": nrm(ks[9], (N_LAYERS_A, D_MODEL, 2 * GM_WIDTH), D_MODEL ** -0.5),
        "gm_ln_g": gain(ks[10], (N_LAYERS_A, GM_WIDTH)),
        "gm_ln_b": nrm(ks[11], (N_LAYERS_A, GM_WIDTH), 0.02),
        "gm_ws": nrm(ks[12], (N_LAYERS_A, GM_GROUPS, GM_CHUNK, GM_CHUNK), GM_CHUNK ** -0.5),
        "gm_bs": 1.0 + nrm(ks[13], (N_LAYERS_A, GM_GROUPS, GM_CHUNK), 0.1),
        "gm_out": nrm(ks[14], (N_LAYERS_A, GM_WIDTH, D_MODEL), GM_WIDTH ** -0.5),
        "nsa_in": nrm(ks[15], (N_LAYERS_B, D_MODEL, NSA_IN_DIM), D_MODEL ** -0.5),
        "nsa_kc_pe": nrm(ks[16], (N_LAYERS_B, CMP_LEN, HEAD_DIM), 0.1),
        "nsa_kc_w1": nrm(ks[17], (N_LAYERS_B, CMP_LEN * HEAD_DIM, CMP_HIDDEN), (CMP_LEN * HEAD_DIM) ** -0.5),
        "nsa_kc_w2": nrm(ks[18], (N_LAYERS_B, CMP_HIDDEN, HEAD_DIM), CMP_HIDDEN ** -0.5),
        "nsa_vc_pe": nrm(ks[19], (N_LAYERS_B, CMP_LEN, HEAD_DIM), 0.1),
        "nsa_vc_w1": nrm(ks[20], (N_LAYERS_B, CMP_LEN * HEAD_DIM, CMP_HIDDEN), (CMP_LEN * HEAD_DIM) ** -0.5),
        "nsa_vc_w2": nrm(ks[21], (N_LAYERS_B, CMP_HIDDEN, HEAD_DIM), CMP_HIDDEN ** -0.5),
        "nsa_out": nrm(ks[22], (N_LAYERS_B, NSA_HEADS * HEAD_DIM, D_MODEL), (NSA_HEADS * HEAD_DIM) ** -0.5),
        "final_norm": gain(ks[23], (D_MODEL,)),
    }


def reference(x, p, norm_mix, norm_ffn, norm_ple, ffn_up, ffn_down, ple_proj, ple_gate,
              gm_in, gm_ln_g, gm_ln_b, gm_ws, gm_bs, gm_out,
              nsa_in, nsa_kc_pe, nsa_kc_w1, nsa_kc_w2, nsa_vc_pe, nsa_vc_w1, nsa_vc_w2, nsa_out,
              final_norm):
    for i in range(DEPTH):
        h = rmsnorm(x, norm_mix[i])
        j = i // N_MIXERS
        if i % N_MIXERS == 0:
            mix = chunked_gmlp(h, gm_in[j], gm_ln_g[j], gm_ln_b[j], gm_ws[j], gm_bs[j], gm_out[j])
        else:
            mix = native_sparse_attention(h, nsa_in[j], nsa_kc_pe[j], nsa_kc_w1[j], nsa_kc_w2[j],
                                          nsa_vc_pe[j], nsa_vc_w1[j], nsa_vc_w2[j], nsa_out[j])
        x = x + mix
        x = x + sqrelu_mlp(rmsnorm(x, norm_ffn[i]), ffn_up[i], ffn_down[i])
        gate = jax.nn.sigmoid(rmsnorm(x, norm_ple[i]) @ ple_gate[i])
        x = x + gate * (p[i] @ ple_proj[i])
    return rmsnorm(x, final_norm)
```

```python
import functools

import numpy as np
import jax
import jax.numpy as jnp
from jax import lax
from jax.experimental import pallas as pl
from jax.experimental.pallas import tpu as pltpu

F32 = jnp.float32
BF16 = jnp.bfloat16

EPS = 1e-6
HEAD_DIM = 128
NSA_HEADS = 16
NSA_KV_GROUPS = 4
NSA_HPG = NSA_HEADS // NSA_KV_GROUPS
ROT_DIM = HEAD_DIM // 4
ROPE_THETA = 500000.0
CMP_LEN = 32
CMP_STRIDE = 16
SEL_BLOCK = 64
SEL_TOPK = 16
WINDOW = 512
Q_BLOCK = 128
GM_CHUNK = 128
NEG_INF = -1e30
FORCE_SCORE = 1e9

LANES = 128
SEL_BIAS_BLOCKS = LANES
SEL_BIAS_KEYS = SEL_BIAS_BLOCKS * SEL_BLOCK
VMEM_LIMIT = 56 * 1024 * 1024


def _params(sem):
    return pltpu.CompilerParams(dimension_semantics=sem, vmem_limit_bytes=VMEM_LIMIT)


def _rmsnorm(x, g):
    return x * lax.rsqrt(jnp.mean(x * x, axis=-1, keepdims=True) + EPS) * g


def _dot(a, b):
    return jnp.dot(a, b, preferred_element_type=F32)


def _dot_nt(a, b):
    return lax.dot_general(a, b, (((1,), (1,)), ((), ())), preferred_element_type=F32)


def _rope(x, c, sa, sb):
    return x * c + pltpu.roll(x, LANES - ROT_DIM // 2, 1) * sa + pltpu.roll(x, ROT_DIM // 2, 1) * sb


def _rope_tables(pos):
    half = ROT_DIM // 2
    inv = jnp.power(jnp.float32(ROPE_THETA), -jnp.arange(half, dtype=F32) * 2.0 / ROT_DIM)
    ang = pos.astype(F32)[:, None] * inv[None, :]
    cos, sin = jnp.cos(ang), jnp.sin(ang)
    n = pos.shape[0]
    rest = HEAD_DIM - ROT_DIM
    c = jnp.concatenate([cos, cos, jnp.ones((n, rest), F32)], axis=1)
    sa = jnp.concatenate([-sin, jnp.zeros((n, half + rest), F32)], axis=1)
    sb = jnp.concatenate([jnp.zeros((n, half), F32), sin, jnp.zeros((n, rest), F32)], axis=1)
    return c, sa, sb


def _norm_mm_body(x_ref, g_ref, w_ref, o_ref, h_ref, *, act):
    @pl.when(pl.program_id(1) == 0)
    def _():
        h_ref[...] = _rmsnorm(x_ref[...], g_ref[...]).astype(BF16)

    acc = _dot(h_ref[...], w_ref[...])
    if act == "gelu":
        acc = jax.nn.gelu(acc)
    elif act == "sigmoid":
        acc = jax.nn.sigmoid(acc)
    o_ref[...] = acc.astype(o_ref.dtype)


def _norm_mm(x, g, w, *, act, out_dtype, tm, tn):
    m, d = x.shape
    n = w.shape[1]
    return pl.pallas_call(
        functools.partial(_norm_mm_body, act=act),
        grid=(m // tm, n // tn),
        in_specs=[
            pl.BlockSpec((tm, d), lambda i, j: (i, 0)),
            pl.BlockSpec((1, d), lambda i, j: (0, 0)),
            pl.BlockSpec((d, tn), lambda i, j: (0, j)),
        ],
        out_specs=pl.BlockSpec((tm, tn), lambda i, j: (i, j)),
        out_shape=jax.ShapeDtypeStruct((m, n), out_dtype),
        scratch_shapes=[pltpu.VMEM((tm, d), BF16)],
        compiler_params=_params(("parallel", "arbitrary")),
        name="norm_mm_" + str(act),
    )(x, g, w)


def _gmlp_body(z_ref, x_ref, lg_ref, lb_ref, ws_ref, bs_ref, wo_ref, o_ref, y_ref, *, tm, width):
    groups = ws_ref.shape[0]
    gd = width // groups
    u = z_ref[:, :width]
    v = z_ref[:, width:].astype(F32)
    mu = jnp.mean(v, axis=-1, keepdims=True)
    var = jnp.mean(jnp.square(v - mu), axis=-1, keepdims=True)
    vn = ((v - mu) * lax.rsqrt(var + EPS) * lg_ref[...] + lb_ref[...]).astype(BF16)
    r = lax.broadcasted_iota(jnp.int32, (GM_CHUNK, GM_CHUNK), 0)
    c = lax.broadcasted_iota(jnp.int32, (GM_CHUNK, GM_CHUNK), 1)
    causal = c <= r
    for g in range(groups):
        wg = jnp.where(causal, ws_ref[g], 0.0).astype(BF16)
        bg = bs_ref[:, g:g + 1]
        for ch in range(tm // GM_CHUNK):
            rows = slice(ch * GM_CHUNK, (ch + 1) * GM_CHUNK)
            cols = slice(g * gd, (g + 1) * gd)
            sv = _dot(wg, vn[rows, cols]) + bg
            y_ref[rows, cols] = (u[rows, cols].astype(F32) * sv).astype(BF16)
    o_ref[...] = x_ref[...] + _dot(y_ref[...], wo_ref[...])


def _gmlp_gate_out(z, x, ln_g, ln_b, ws, bs, w_out, *, tm):
    m, d = x.shape
    width = z.shape[1] // 2
    groups = ws.shape[0]
    return pl.pallas_call(
        functools.partial(_gmlp_body, tm=tm, width=width),
        grid=(m // tm,),
        in_specs=[
            pl.BlockSpec((tm, 2 * width), lambda i: (i, 0)),
            pl.BlockSpec((tm, d), lambda i: (i, 0)),
            pl.BlockSpec((1, width), lambda i: (0, 0)),
            pl.BlockSpec((1, width), lambda i: (0, 0)),
            pl.BlockSpec((groups, GM_CHUNK, GM_CHUNK), lambda i: (0, 0, 0)),
            pl.BlockSpec((GM_CHUNK, groups), lambda i: (0, 0)),
            pl.BlockSpec((width, d), lambda i: (0, 0)),
        ],
        out_specs=pl.BlockSpec((tm, d), lambda i: (i, 0)),
        out_shape=jax.ShapeDtypeStruct((m, d), F32),
        scratch_shapes=[pltpu.VMEM((tm, width), BF16)],
        compiler_params=_params(("parallel",)),
        name="gmlp_gate_out",
    )(z, x, ln_g, ln_b, ws, bs, w_out)


def _ffn_body(x_ref, g_ref, wu_ref, wd_ref, o_ref, h_ref):
    f = pl.program_id(1)

    @pl.when(f == 0)
    def _():
        x = x_ref[...]
        h_ref[...] = _rmsnorm(x, g_ref[...]).astype(BF16)
        o_ref[...] = x

    a = jnp.square(jnp.maximum(_dot(h_ref[...], wu_ref[...]), 0.0)).astype(BF16)
    o_ref[...] += _dot(a, wd_ref[...])


def _ffn(x, g, w_up, w_down, *, tm, tf):
    m, d = x.shape
    ff = w_up.shape[1]
    return pl.pallas_call(
        _ffn_body,
        grid=(m // tm, ff // tf),
        in_specs=[
            pl.BlockSpec((tm, d), lambda i, f: (i, 0)),
            pl.BlockSpec((1, d), lambda i, f: (0, 0)),
            pl.BlockSpec((d, tf), lambda i, f: (0, f)),
            pl.BlockSpec((tf, d), lambda i, f: (f, 0)),
        ],
        out_specs=pl.BlockSpec((tm, d), lambda i, f: (i, 0)),
        out_shape=jax.ShapeDtypeStruct((m, d), F32),
        scratch_shapes=[pltpu.VMEM((tm, d), BF16)],
        compiler_params=_params(("parallel", "arbitrary")),
        name="ffn",
    )(x, g, w_up, w_down)


def _ple_body(x_ref, p_ref, g_ref, wg_ref, wp_ref, fg_ref, o_ref, *, final):
    x = x_ref[...]
    h = _rmsnorm(x, g_ref[...]).astype(BF16)
    gate = jax.nn.sigmoid(_dot(h, wg_ref[...]))
    y = x + gate * _dot(p_ref[...].astype(BF16), wp_ref[...])
    if final:
        y = _rmsnorm(y, fg_ref[...])
    o_ref[...] = y


def _ple(x, p, g, w_gate, w_proj, final_g, *, final, tm):
    m, d = x.shape
    pd = p.shape[1]
    return pl.pallas_call(
        functools.partial(_ple_body, final=final),
        grid=(m // tm,),
        in_specs=[
            pl.BlockSpec((tm, d), lambda i: (i, 0)),
            pl.BlockSpec((tm, pd), lambda i: (i, 0)),
            pl.BlockSpec((1, d), lambda i: (0, 0)),
            pl.BlockSpec((d, d), lambda i: (0, 0)),
            pl.BlockSpec((pd, d), lambda i: (0, 0)),
            pl.BlockSpec((1, d), lambda i: (0, 0)),
        ],
        out_specs=pl.BlockSpec((tm, d), lambda i: (i, 0)),
        out_shape=jax.ShapeDtypeStruct((m, d), F32),
        compiler_params=_params(("parallel",)),
        name="ple",
    )(x, p, g, w_gate, w_proj, final_g)


def _mm_res_body(a_ref, w_ref, x_ref, o_ref):
    o_ref[...] = x_ref[...] + _dot(a_ref[...], w_ref[...])


def _mm_residual(a, w, x, *, tm, tn):
    m, k = a.shape
    n = w.shape[1]
    return pl.pallas_call(
        _mm_res_body,
        grid=(m // tm, n // tn),
        in_specs=[
            pl.BlockSpec((tm, k), lambda i, j: (i, 0)),
            pl.BlockSpec((k, tn), lambda i, j: (0, j)),
            pl.BlockSpec((tm, tn), lambda i, j: (i, j)),
        ],
        out_specs=pl.BlockSpec((tm, tn), lambda i, j: (i, j)),
        out_shape=jax.ShapeDtypeStruct((m, n), F32),
        compiler_params=_params(("parallel", "arbitrary")),
        name="mm_residual",
    )(a, w, x)


NSA_PROJ_TN = 4 * HEAD_DIM
NSA_Q_TILES = NSA_HEADS * HEAD_DIM // NSA_PROJ_TN
NSA_KS_TILE = NSA_Q_TILES + 2
NSA_KW_TILE = NSA_Q_TILES + 4
HM_KC, HM_VC, HM_KS, HM_VS, HM_KW, HM_VW = (NSA_HEADS + NSA_KV_GROUPS * i for i in range(6))


def _nsa_proj_body(x_ref, g_ref, w_ref, c_ref, sa_ref, sb_ref, o_ref, h_ref):
    j = pl.program_id(1)

    @pl.when(j == 0)
    def _():
        h_ref[...] = _rmsnorm(x_ref[...], g_ref[...]).astype(BF16)

    acc = _dot(h_ref[...], w_ref[...])
    heads = NSA_PROJ_TN // HEAD_DIM
    is_q = j < NSA_Q_TILES
    is_rope = is_q | (j == NSA_KS_TILE) | (j == NSA_KW_TILE)

    @pl.when(is_rope)
    def _():
        scale = jnp.where(is_q, HEAD_DIM ** -0.5, 1.0).astype(F32)
        c, sa, sb = c_ref[...], sa_ref[...], sb_ref[...]
        for hh in range(heads):
            seg = acc[:, hh * HEAD_DIM:(hh + 1) * HEAD_DIM]
            o_ref[hh] = (_rope(seg, c, sa, sb) * scale).astype(BF16)

    @pl.when(jnp.logical_not(is_rope))
    def _():
        for hh in range(heads):
            o_ref[hh] = acc[:, hh * HEAD_DIM:(hh + 1) * HEAD_DIM].astype(BF16)


def _nsa_proj(x, g, w, tables, seq, *, tm):
    m, d = x.shape
    n = w.shape[1]
    heads = NSA_PROJ_TN // HEAD_DIM
    tpb = seq // tm
    tab_spec = pl.BlockSpec((tm, HEAD_DIM), lambda i, j: (i % tpb, 0))
    return pl.pallas_call(
        _nsa_proj_body,
        grid=(m // tm, n // NSA_PROJ_TN),
        in_specs=[
            pl.BlockSpec((tm, d), lambda i, j: (i, 0)),
            pl.BlockSpec((1, d), lambda i, j: (0, 0)),
            pl.BlockSpec((d, NSA_PROJ_TN), lambda i, j: (0, j)),
            tab_spec, tab_spec, tab_spec,
        ],
        out_specs=pl.BlockSpec((heads, tm, HEAD_DIM), lambda i, j: (j, i, 0)),
        out_shape=jax.ShapeDtypeStruct((n // HEAD_DIM, m, HEAD_DIM), BF16),
        scratch_shapes=[pltpu.VMEM((tm, d), BF16)],
        compiler_params=_params(("parallel", "arbitrary")),
        name="nsa_proj",
    )(x, g, w, *tables)


def _compress_body(kr_ref, pe_ref, w1_ref, w2_ref, c_ref, sa_ref, sb_ref, o_ref):
    kr = kr_ref[0].astype(F32)
    rows = kr.shape[0]
    a = _dot((kr + pe_ref[0, 0]).astype(BF16), w1_ref[0, 0])
    b = _dot((kr + pe_ref[0, 1]).astype(BF16), w1_ref[0, 1])
    hid = jax.nn.gelu(a + pltpu.roll(b, rows - 1, 0))
    out = _dot(hid.astype(BF16), w2_ref[0])
    is_k = pl.program_id(0) == 0

    @pl.when(is_k)
    def _():
        o_ref[0, 0, 0] = _rope(out, c_ref[...], sa_ref[...], sb_ref[...]).astype(BF16)

    @pl.when(jnp.logical_not(is_k))
    def _():
        o_ref[0, 0, 0] = out.astype(BF16)


def _compress(hm, pe, w1, w2, tables, batch, seq):
    rows = seq // CMP_STRIDE
    kdim = CMP_STRIDE * HEAD_DIM
    hid = w1.shape[-1]
    kr = hm.reshape(hm.shape[0], batch * rows, kdim)
    tab_spec = pl.BlockSpec((rows, HEAD_DIM), lambda w, b, g: (0, 0))
    return pl.pallas_call(
        _compress_body,
        grid=(2, batch, NSA_KV_GROUPS),
        in_specs=[
            pl.BlockSpec((1, rows, kdim), lambda w, b, g: (HM_KC + NSA_KV_GROUPS * w + g, b, 0)),
            pl.BlockSpec((1, 2, 1, kdim), lambda w, b, g: (w, 0, 0, 0)),
            pl.BlockSpec((1, 2, kdim, hid), lambda w, b, g: (w, 0, 0, 0)),
            pl.BlockSpec((1, hid, HEAD_DIM), lambda w, b, g: (w, 0, 0)),
            tab_spec, tab_spec, tab_spec,
        ],
        out_specs=pl.BlockSpec((1, 1, 1, rows, HEAD_DIM), lambda w, b, g: (w, b, g, 0, 0)),
        out_shape=jax.ShapeDtypeStruct((2, batch, NSA_KV_GROUPS, rows, HEAD_DIM), BF16),
        compiler_params=_params(("parallel", "parallel", "parallel")),
        name="nsa_compress",
    )(kr, pe, w1, w2, *tables)


def _flash_step(qa, k, v, mask, carry):
    m, l, acc = carry
    s = _dot_nt(qa, k)
    s = jnp.where(mask, s, NEG_INF)
    m_new = jnp.maximum(m, jnp.max(s, axis=-1, keepdims=True))
    alpha = jnp.exp(m - m_new)
    p = jnp.exp(s - m_new)
    l = alpha * l + jnp.sum(p, axis=-1, keepdims=True)
    acc = alpha * acc + _dot(p.astype(BF16), v)
    return m_new, l, acc


def _attn_body(q_ref, ks_ref, vs_ref, kw_ref, vw_ref, kc_ref, vc_ref, gate_ref, e_ref, w_ref,
               o_ref, qaug_ref, *, ns, tk):
    qi = pl.program_id(2)
    t0 = qi * Q_BLOCK
    rows = NSA_HPG * Q_BLOCK
    q4 = q_ref[...].reshape(rows, HEAD_DIM)
    tq = t0 + (lax.broadcasted_iota(jnp.int32, (rows, 1), 0) & (Q_BLOCK - 1))

    kc = kc_ref[0, 0, 0]
    ncp = kc.shape[0]
    s_c = _dot_nt(q4, kc)
    cend = lax.broadcasted_iota(jnp.int32, (rows, ncp), 1) * CMP_STRIDE + (CMP_LEN - 1)
    cmask = cend <= tq
    s_c = jnp.where(cmask, s_c, NEG_INF)
    e_c = jnp.exp(s_c - jnp.max(s_c, axis=-1, keepdims=True))
    p_c = jnp.where(cmask, e_c * (1.0 / jnp.sum(e_c, axis=-1, keepdims=True)), 0.0)
    o_c = _dot(p_c.astype(BF16), vc_ref[0, 0, 0])

    psum = p_c[0:Q_BLOCK]
    for n in range(1, NSA_HPG):
        psum = psum + p_c[n * Q_BLOCK:(n + 1) * Q_BLOCK]
    p_hi = psum.astype(BF16)
    p_lo = (psum - p_hi.astype(F32)).astype(BF16)
    imp = _dot(p_hi, w_ref[...]) + _dot(p_lo, w_ref[...])

    nsp = imp.shape[1]
    blk = lax.broadcasted_iota(jnp.int32, (Q_BLOCK, nsp), 1)
    blk_f = blk.astype(F32)
    cur = (t0 + lax.broadcasted_iota(jnp.int32, (Q_BLOCK, nsp), 0)) // SEL_BLOCK
    valid = blk <= cur
    forced = (blk == 0) | (blk == cur) | (blk == cur - 1)
    score = jnp.where(valid, jnp.where(forced, FORCE_SCORE, imp), NEG_INF)
    picked = jnp.zeros((Q_BLOCK, nsp), dtype=jnp.bool_)
    for _ in range(min(SEL_TOPK, ns)):
        top = jnp.max(score, axis=-1, keepdims=True)
        first = jnp.min(jnp.where(score == top, blk_f, float(nsp)), axis=-1, keepdims=True)
        hit = blk_f == first
        picked = picked | hit
        score = jnp.where(hit, -jnp.inf, score)
    bias = jnp.where(picked & valid, 0.0, NEG_INF).astype(BF16)

    for h in range(nsp // SEL_BIAS_BLOCKS):
        qaug_ref[h, :, 0:HEAD_DIM] = q4
        bh = bias[:, h * SEL_BIAS_BLOCKS:(h + 1) * SEL_BIAS_BLOCKS]
        for n in range(NSA_HPG):
            qaug_ref[h, n * Q_BLOCK:(n + 1) * Q_BLOCK, HEAD_DIM:HEAD_DIM + SEL_BIAS_BLOCKS] = bh

    init = (jnp.full((rows, 1), NEG_INF, F32), jnp.zeros((rows, 1), F32),
            jnp.zeros((rows, HEAD_DIM), F32))

    def sel_body(j, carry):
        k0 = pl.multiple_of(j * tk, tk)
        e0 = pl.multiple_of(k0 % SEL_BIAS_KEYS, tk)
        kaug = jnp.concatenate([ks_ref[0, pl.ds(k0, tk), :], e_ref[pl.ds(e0, tk), :]], axis=1)
        qa = qaug_ref[k0 // SEL_BIAS_KEYS]
        kpos = k0 + lax.broadcasted_iota(jnp.int32, (rows, tk), 1)
        return _flash_step(qa, kaug, vs_ref[0, pl.ds(k0, tk), :], kpos <= tq, carry)

    _, l_s, acc_s = lax.fori_loop(0, (t0 + Q_BLOCK + tk - 1) // tk, sel_body, init)
    o_s = acc_s * (1.0 / l_s)

    carry = init
    for w in range(WINDOW // Q_BLOCK, -1, -1):
        kp0 = t0 - WINDOW + w * Q_BLOCK

        def win_step(c, kp0=kp0):
            k0 = pl.multiple_of(kp0, Q_BLOCK)
            kpos = kp0 + lax.broadcasted_iota(jnp.int32, (rows, Q_BLOCK), 1)
            mask = (kpos <= tq) & (kpos > tq - WINDOW)
            return _flash_step(q4, kw_ref[0, pl.ds(k0, Q_BLOCK), :], vw_ref[0, pl.ds(k0, Q_BLOCK), :],
                               mask, c)

        carry = lax.cond(kp0 >= 0, win_step, lambda c: c, carry)
    _, l_w, acc_w = carry
    o_w = acc_w * (1.0 / l_w)

    gates = gate_ref[...]
    for n in range(NSA_HPG):
        r = slice(n * Q_BLOCK, (n + 1) * Q_BLOCK)
        o = (gates[:, n:n + 1] * o_c[r]
             + gates[:, NSA_HPG + n:NSA_HPG + n + 1] * o_s[r]
             + gates[:, 2 * NSA_HPG + n:2 * NSA_HPG + n + 1] * o_w[r])
        o_ref[:, n * HEAD_DIM:(n + 1) * HEAD_DIM] = o.astype(BF16)


def _nsa_attention(hm, cmp, gates, onehot, cmp_to_sel, batch, seq, *, tk):
    nq = seq // Q_BLOCK
    ns = seq // SEL_BLOCK
    ncp = seq // CMP_STRIDE
    nsp = cmp_to_sel.shape[1]
    m = batch * seq
    once = pl.Buffered(1)

    def slab(first):
        return pl.BlockSpec((1, seq, HEAD_DIM), lambda b, g, qi: (first + g, b, 0), pipeline_mode=once)

    def cmp_spec(which):
        return pl.BlockSpec((1, 1, 1, ncp, HEAD_DIM), lambda b, g, qi: (which, b, g, 0, 0))

    return pl.pallas_call(
        functools.partial(_attn_body, ns=ns, tk=tk),
        grid=(batch, NSA_KV_GROUPS, nq),
        in_specs=[
            pl.BlockSpec((NSA_HPG, Q_BLOCK, HEAD_DIM), lambda b, g, qi: (g, b * nq + qi, 0)),
            slab(HM_KS), slab(HM_VS), slab(HM_KW), slab(HM_VW),
            cmp_spec(0), cmp_spec(1),
            pl.BlockSpec((Q_BLOCK, LANES), lambda b, g, qi: (b * nq + qi, g)),
            pl.BlockSpec(onehot.shape, lambda b, g, qi: (0, 0), pipeline_mode=once),
            pl.BlockSpec((ncp, nsp), lambda b, g, qi: (0, 0), pipeline_mode=once),
        ],
        out_specs=pl.BlockSpec((Q_BLOCK, NSA_HPG * HEAD_DIM), lambda b, g, qi: (b * nq + qi, g)),
        out_shape=jax.ShapeDtypeStruct((m, NSA_HEADS * HEAD_DIM), BF16),
        scratch_shapes=[pltpu.VMEM((nsp // SEL_BIAS_BLOCKS, NSA_HPG * Q_BLOCK, 2 * HEAD_DIM), BF16)],
        compiler_params=_params(("parallel", "parallel", "arbitrary")),
        name="nsa_attention",
    )(hm, hm, hm, hm, hm, cmp, cmp, gates, onehot, cmp_to_sel)


def _nsa_constants(seq):
    nc = (seq - CMP_LEN) // CMP_STRIDE + 1
    ns = seq // SEL_BLOCK
    ncp = seq // CMP_STRIDE
    nsp = -(-ns // SEL_BIAS_BLOCKS) * SEL_BIAS_BLOCKS
    cs = np.arange(nc)[:, None] * CMP_STRIDE
    ss = np.arange(ns)[None, :] * SEL_BLOCK
    ov = np.clip(np.minimum(cs + CMP_LEN, ss + SEL_BLOCK) - np.maximum(cs, ss), 0, None) / CMP_STRIDE
    cmp_to_sel = np.zeros((ncp, nsp), np.float32)
    cmp_to_sel[:nc, :ns] = ov
    keys = np.arange(min(seq, SEL_BIAS_KEYS))
    onehot = (keys[:, None] // SEL_BLOCK == np.arange(SEL_BIAS_BLOCKS)[None, :]).astype(np.float32)
    return jnp.asarray(cmp_to_sel, BF16), jnp.asarray(onehot, BF16)


def _gate_weight(w_gl):
    d = w_gl.shape[0]
    w = w_gl.reshape(d, 3, NSA_KV_GROUPS, NSA_HPG).transpose(0, 2, 1, 3).reshape(d, NSA_KV_GROUPS, 3 * NSA_HPG)
    w = jnp.pad(w, ((0, 0), (0, 0), (0, LANES - 3 * NSA_HPG)))
    return w.reshape(d, NSA_KV_GROUPS * LANES)


def _nsa_layer(x, g_mix, w_in, kc_pe, kc_w1, kc_w2, vc_pe, vc_w1, vc_w2, w_out, batch, seq):
    n_main = NSA_HEADS * HEAD_DIM + 6 * NSA_KV_GROUPS * HEAD_DIM
    pos_tables = _rope_tables(jnp.arange(seq))
    hm = _nsa_proj(x, g_mix, w_in[:, :n_main].astype(BF16), pos_tables, seq, tm=1024)
    gates = _norm_mm(x, g_mix, _gate_weight(w_in[:, n_main:]).astype(BF16), act="sigmoid",
                     out_dtype=F32, tm=1024, tn=NSA_KV_GROUPS * LANES)
    rows = seq // CMP_STRIDE
    cmp_tables = _rope_tables(jnp.arange(rows) * CMP_STRIDE + CMP_LEN - 1)
    half = CMP_LEN // 2
    pe = jnp.stack([kc_pe, vc_pe]).reshape(2, 2, 1, half * HEAD_DIM)
    w1 = jnp.stack([kc_w1, vc_w1]).astype(BF16)
    w1 = w1.reshape(2, 2, half * HEAD_DIM, w1.shape[-1])
    w2 = jnp.stack([kc_w2, vc_w2]).astype(BF16)
    cmp = _compress(hm, pe, w1, w2, cmp_tables, batch, seq)
    cmp_to_sel, onehot = _nsa_constants(seq)
    att = _nsa_attention(hm, cmp, gates, onehot, cmp_to_sel, batch, seq, tk=512)
    return _mm_residual(att, w_out.astype(BF16), x, tm=1024, tn=512)


def kernel(x, p, norm_mix, norm_ffn, norm_ple, ffn_up, ffn_down, ple_proj, ple_gate, gm_in, gm_ln_g, gm_ln_b, gm_ws, gm_bs, gm_out, nsa_in, nsa_kc_pe, nsa_kc_w1, nsa_kc_w2, nsa_vc_pe, nsa_vc_w1, nsa_vc_w2, nsa_out, final_norm):
    batch, seq, d = x.shape
    m = batch * seq
    depth = p.shape[0]
    xf = x.reshape(m, d)
    row = lambda v: v.reshape(1, -1)
    for i in range(depth):
        j = i // 2
        if i % 2 == 0:
            z = _norm_mm(xf, row(norm_mix[i]), gm_in[j].astype(BF16), act="gelu", out_dtype=BF16,
                         tm=1024, tn=512)
            xf = _gmlp_gate_out(z, xf, row(gm_ln_g[j]), row(gm_ln_b[j]), gm_ws[j], gm_bs[j].T,
                                gm_out[j].astype(BF16), tm=512)
        else:
            xf = _nsa_layer(xf, row(norm_mix[i]), nsa_in[j], nsa_kc_pe[j], nsa_kc_w1[j], nsa_kc_w2[j],
                            nsa_vc_pe[j], nsa_vc_w1[j], nsa_vc_w2[j], nsa_out[j], batch, seq)
        xf = _ffn(xf, row(norm_ffn[i]), ffn_up[i].astype(BF16), ffn_down[i].astype(BF16), tm=512, tf=512)
        last = i == depth - 1
        xf = _ple(xf, p[i].reshape(m, -1), row(norm_ple[i]), ple_gate[i].astype(BF16),
                  ple_proj[i].astype(BF16), row(final_norm), final=last, tm=256)
    return xf.reshape(batch, seq, d)
```

```python
import functools

import numpy as np
import jax
import jax.numpy as jnp
from jax import lax
from jax.experimental import pallas as pl
from jax.experimental.pallas import tpu as pltpu

F32 = jnp.float32
BF16 = jnp.bfloat16

EPS = 1e-6
HEAD_DIM = 128
NSA_HEADS = 16
NSA_KV_GROUPS = 4
NSA_HPG = NSA_HEADS // NSA_KV_GROUPS
ROT_DIM = HEAD_DIM // 4
ROPE_THETA = 500000.0
CMP_LEN = 32
CMP_STRIDE = 16
SEL_BLOCK = 64
SEL_TOPK = 16
WINDOW = 512
Q_BLOCK = 128
GM_CHUNK = 128
NEG_INF = -1e30
FORCE_SCORE = 1e9
LOG2_E = 1.4426950408889634

LANES = 128
SEL_BIAS_BLOCKS = LANES
SEL_BIAS_KEYS = SEL_BIAS_BLOCKS * SEL_BLOCK
VMEM_LIMIT = 56 * 1024 * 1024


def _params(sem):
    return pltpu.CompilerParams(dimension_semantics=sem, vmem_limit_bytes=VMEM_LIMIT)


def _rmsnorm(x, g):
    return x * lax.rsqrt(jnp.mean(x * x, axis=-1, keepdims=True) + EPS) * g


def _dot(a, b):
    return jnp.dot(a, b, preferred_element_type=F32)


def _rope(x, c, sa, sb):
    return x * c + pltpu.roll(x, LANES - ROT_DIM // 2, 1) * sa + pltpu.roll(x, ROT_DIM // 2, 1) * sb


def _rope_tables(pos):
    half = ROT_DIM // 2
    inv = jnp.power(jnp.float32(ROPE_THETA), -jnp.arange(half, dtype=F32) * 2.0 / ROT_DIM)
    ang = pos.astype(F32)[:, None] * inv[None, :]
    cos, sin = jnp.cos(ang), jnp.sin(ang)
    n = pos.shape[0]
    rest = HEAD_DIM - ROT_DIM
    c = jnp.concatenate([cos, cos, jnp.ones((n, rest), F32)], axis=1)
    sa = jnp.concatenate([-sin, jnp.zeros((n, half + rest), F32)], axis=1)
    sb = jnp.concatenate([jnp.zeros((n, half), F32), sin, jnp.zeros((n, rest), F32)], axis=1)
    return c, sa, sb


def _norm_mm_body(x_ref, g_ref, w_ref, o_ref, h_ref, *, act):
    @pl.when(pl.program_id(1) == 0)
    def _():
        h_ref[...] = _rmsnorm(x_ref[...], g_ref[...]).astype(BF16)

    acc = _dot(h_ref[...], w_ref[...])
    if act == "gelu":
        acc = jax.nn.gelu(acc)
    elif act == "sigmoid":
        acc = jax.nn.sigmoid(acc)
    o_ref[...] = acc.astype(o_ref.dtype)


def _norm_mm(x, g, w, *, act, out_dtype, tm, tn):
    m, d = x.shape
    n = w.shape[1]
    return pl.pallas_call(
        functools.partial(_norm_mm_body, act=act),
        grid=(m // tm, n // tn),
        in_specs=[
            pl.BlockSpec((tm, d), lambda i, j: (i, 0)),
            pl.BlockSpec((1, d), lambda i, j: (0, 0)),
            pl.BlockSpec((d, tn), lambda i, j: (0, j)),
        ],
        out_specs=pl.BlockSpec((tm, tn), lambda i, j: (i, j)),
        out_shape=jax.ShapeDtypeStruct((m, n), out_dtype),
        scratch_shapes=[pltpu.VMEM((tm, d), BF16)],
        compiler_params=_params(("parallel", "arbitrary")),
        name="norm_mm_" + str(act),
    )(x, g, w)


def _gmlp_body(z_ref, x_ref, lg_ref, lb_ref, ws_ref, bs_ref, wo_ref, o_ref, y_ref, *, tm, width):
    groups = ws_ref.shape[0]
    gd = width // groups
    u = z_ref[:, :width]
    v = z_ref[:, width:].astype(F32)
    mu = jnp.mean(v, axis=-1, keepdims=True)
    var = jnp.mean(jnp.square(v - mu), axis=-1, keepdims=True)
    vn = ((v - mu) * lax.rsqrt(var + EPS) * lg_ref[...] + lb_ref[...]).astype(BF16)
    r = lax.broadcasted_iota(jnp.int32, (GM_CHUNK, GM_CHUNK), 0)
    c = lax.broadcasted_iota(jnp.int32, (GM_CHUNK, GM_CHUNK), 1)
    causal = c <= r
    for g in range(groups):
        wg = jnp.where(causal, ws_ref[g], 0.0).astype(BF16)
        bg = bs_ref[:, g:g + 1]
        for ch in range(tm // GM_CHUNK):
            rows = slice(ch * GM_CHUNK, (ch + 1) * GM_CHUNK)
            cols = slice(g * gd, (g + 1) * gd)
            sv = _dot(wg, vn[rows, cols]) + bg
            y_ref[rows, cols] = (u[rows, cols].astype(F32) * sv).astype(BF16)
    o_ref[...] = x_ref[...] + _dot(y_ref[...], wo_ref[...])


def _gmlp_gate_out(z, x, ln_g, ln_b, ws, bs, w_out, *, tm):
    m, d = x.shape
    width = z.shape[1] // 2
    groups = ws.shape[0]
    return pl.pallas_call(
        functools.partial(_gmlp_body, tm=tm, width=width),
        grid=(m // tm,),
        in_specs=[
            pl.BlockSpec((tm, 2 * width), lambda i: (i, 0)),
            pl.BlockSpec((tm, d), lambda i: (i, 0)),
            pl.BlockSpec((1, width), lambda i: (0, 0)),
            pl.BlockSpec((1, width), lambda i: (0, 0)),
            pl.BlockSpec((groups, GM_CHUNK, GM_CHUNK), lambda i: (0, 0, 0)),
            pl.BlockSpec((GM_CHUNK, groups), lambda i: (0, 0)),
            pl.BlockSpec((width, d), lambda i: (0, 0)),
        ],
        out_specs=pl.BlockSpec((tm, d), lambda i: (i, 0)),
        out_shape=jax.ShapeDtypeStruct((m, d), F32),
        scratch_shapes=[pltpu.VMEM((tm, width), BF16)],
        compiler_params=_params(("parallel",)),
        name="gmlp_gate_out",
    )(z, x, ln_g, ln_b, ws, bs, w_out)


def _ffn_body(x_ref, g_ref, wu_ref, wd_ref, o_ref, h_ref):
    f = pl.program_id(1)

    @pl.when(f == 0)
    def _():
        x = x_ref[...]
        h_ref[...] = _rmsnorm(x, g_ref[...]).astype(BF16)
        o_ref[...] = x

    a = jnp.square(jnp.maximum(_dot(h_ref[...], wu_ref[...]), 0.0)).astype(BF16)
    o_ref[...] += _dot(a, wd_ref[...])


def _ffn(x, g, w_up, w_down, *, tm, tf):
    m, d = x.shape
    ff = w_up.shape[1]
    return pl.pallas_call(
        _ffn_body,
        grid=(m // tm, ff // tf),
        in_specs=[
            pl.BlockSpec((tm, d), lambda i, f: (i, 0)),
            pl.BlockSpec((1, d), lambda i, f: (0, 0)),
            pl.BlockSpec((d, tf), lambda i, f: (0, f)),
            pl.BlockSpec((tf, d), lambda i, f: (f, 0)),
        ],
        out_specs=pl.BlockSpec((tm, d), lambda i, f: (i, 0)),
        out_shape=jax.ShapeDtypeStruct((m, d), F32),
        scratch_shapes=[pltpu.VMEM((tm, d), BF16)],
        compiler_params=_params(("parallel", "arbitrary")),
        name="ffn",
    )(x, g, w_up, w_down)


def _ple_body(x_ref, p_ref, g_ref, wg_ref, wp_ref, fg_ref, o_ref, *, final):
    x = x_ref[...]
    h = _rmsnorm(x, g_ref[...]).astype(BF16)
    gate = jax.nn.sigmoid(_dot(h, wg_ref[...]))
    y = x + gate * _dot(p_ref[...].astype(BF16), wp_ref[...])
    if final:
        y = _rmsnorm(y, fg_ref[...])
    o_ref[...] = y


def _ple(x, p, g, w_gate, w_proj, final_g, *, final, tm):
    m, d = x.shape
    pd = p.shape[1]
    return pl.pallas_call(
        functools.partial(_ple_body, final=final),
        grid=(m // tm,),
        in_specs=[
            pl.BlockSpec((tm, d), lambda i: (i, 0)),
            pl.BlockSpec((tm, pd), lambda i: (i, 0)),
            pl.BlockSpec((1, d), lambda i: (0, 0)),
            pl.BlockSpec((d, d), lambda i: (0, 0)),
            pl.BlockSpec((pd, d), lambda i: (0, 0)),
            pl.BlockSpec((1, d), lambda i: (0, 0)),
        ],
        out_specs=pl.BlockSpec((tm, d), lambda i: (i, 0)),
        out_shape=jax.ShapeDtypeStruct((m, d), F32),
        compiler_params=_params(("parallel",)),
        name="ple",
    )(x, p, g, w_gate, w_proj, final_g)


def _mm_res_body(a_ref, w_ref, x_ref, o_ref):
    o_ref[...] = x_ref[...] + _dot(a_ref[...], w_ref[...])


def _mm_residual(a, w, x, *, tm, tn):
    m, k = a.shape
    n = w.shape[1]
    return pl.pallas_call(
        _mm_res_body,
        grid=(m // tm, n // tn),
        in_specs=[
            pl.BlockSpec((tm, k), lambda i, j: (i, 0)),
            pl.BlockSpec((k, tn), lambda i, j: (0, j)),
            pl.BlockSpec((tm, tn), lambda i, j: (i, j)),
        ],
        out_specs=pl.BlockSpec((tm, tn), lambda i, j: (i, j)),
        out_shape=jax.ShapeDtypeStruct((m, n), F32),
        compiler_params=_params(("parallel", "arbitrary")),
        name="mm_residual",
    )(a, w, x)


NSA_PROJ_TN = 4 * HEAD_DIM
NSA_Q_TILES = NSA_HEADS * HEAD_DIM // NSA_PROJ_TN
NSA_KS_TILE = NSA_Q_TILES + 2
NSA_KW_TILE = NSA_Q_TILES + 4
HM_KC, HM_VC, HM_KS, HM_VS, HM_KW, HM_VW = (NSA_HEADS + NSA_KV_GROUPS * i for i in range(6))


def _nsa_proj_body(x_ref, g_ref, w_ref, c_ref, sa_ref, sb_ref, o_ref, h_ref):
    j = pl.program_id(1)

    @pl.when(j == 0)
    def _():
        h_ref[...] = _rmsnorm(x_ref[...], g_ref[...]).astype(BF16)

    acc = _dot(h_ref[...], w_ref[...])
    heads = NSA_PROJ_TN // HEAD_DIM
    is_q = j < NSA_Q_TILES
    is_rope = is_q | (j == NSA_KS_TILE) | (j == NSA_KW_TILE)

    @pl.when(is_rope)
    def _():
        scale = jnp.where(is_q, HEAD_DIM ** -0.5 * LOG2_E, 1.0).astype(F32)
        c, sa, sb = c_ref[...], sa_ref[...], sb_ref[...]
        for hh in range(heads):
            seg = acc[:, hh * HEAD_DIM:(hh + 1) * HEAD_DIM]
            o_ref[hh] = (_rope(seg, c, sa, sb) * scale).astype(BF16)

    @pl.when(jnp.logical_not(is_rope))
    def _():
        for hh in range(heads):
            o_ref[hh] = acc[:, hh * HEAD_DIM:(hh + 1) * HEAD_DIM].astype(BF16)


def _nsa_proj(x, g, w, tables, seq, *, tm):
    m, d = x.shape
    n = w.shape[1]
    heads = NSA_PROJ_TN // HEAD_DIM
    tpb = seq // tm
    tab_spec = pl.BlockSpec((tm, HEAD_DIM), lambda i, j: (i % tpb, 0))
    return pl.pallas_call(
        _nsa_proj_body,
        grid=(m // tm, n // NSA_PROJ_TN),
        in_specs=[
            pl.BlockSpec((tm, d), lambda i, j: (i, 0)),
            pl.BlockSpec((1, d), lambda i, j: (0, 0)),
            pl.BlockSpec((d, NSA_PROJ_TN), lambda i, j: (0, j)),
            tab_spec, tab_spec, tab_spec,
        ],
        out_specs=pl.BlockSpec((heads, tm, HEAD_DIM), lambda i, j: (j, i, 0)),
        out_shape=jax.ShapeDtypeStruct((n // HEAD_DIM, m, HEAD_DIM), BF16),
        scratch_shapes=[pltpu.VMEM((tm, d), BF16)],
        compiler_params=_params(("parallel", "arbitrary")),
        name="nsa_proj",
    )(x, g, w, *tables)


CMP_CHUNK = LANES


def _compress_body(kr_ref, pe_ref, w1_ref, w2_ref, c_ref, sa_ref, sb_ref, o_ref):
    kr = kr_ref[0].astype(F32)
    rows = kr.shape[0]
    a = _dot((kr + pe_ref[0, 0]).astype(BF16), w1_ref[0, 0])
    b = _dot((kr + pe_ref[0, 1]).astype(BF16), w1_ref[0, 1])
    hid = jax.nn.gelu(a + pltpu.roll(b, rows - 1, 0))
    out = _dot(hid.astype(BF16), w2_ref[0])
    is_k = pl.program_id(0) == 0

    @pl.when(is_k)
    def _():
        roped = _rope(out, c_ref[...], sa_ref[...], sb_ref[...]).astype(BF16)
        for j in range(rows // CMP_CHUNK):
            o_ref[0, 0, 0, j] = roped[j * CMP_CHUNK:(j + 1) * CMP_CHUNK]

    @pl.when(jnp.logical_not(is_k))
    def _():
        for j in range(rows // CMP_CHUNK):
            o_ref[0, 0, 0, j] = out[j * CMP_CHUNK:(j + 1) * CMP_CHUNK].T.astype(BF16)


def _compress(kr, pe, w1, w2, tables, batch, seq):
    rows = seq // CMP_STRIDE
    kdim = CMP_STRIDE * HEAD_DIM
    hid = w1.shape[-1]
    tab_spec = pl.BlockSpec((rows, HEAD_DIM), lambda w, b, g: (0, 0))
    chunks = rows // CMP_CHUNK
    return pl.pallas_call(
        _compress_body,
        grid=(2, batch, NSA_KV_GROUPS),
        in_specs=[
            pl.BlockSpec((1, rows, kdim), lambda w, b, g: (NSA_KV_GROUPS * w + g, b, 0)),
            pl.BlockSpec((1, 2, 1, kdim), lambda w, b, g: (w, 0, 0, 0)),
            pl.BlockSpec((1, 2, kdim, hid), lambda w, b, g: (w, 0, 0, 0)),
            pl.BlockSpec((1, hid, HEAD_DIM), lambda w, b, g: (w, 0, 0)),
            tab_spec, tab_spec, tab_spec,
        ],
        out_specs=pl.BlockSpec((1, 1, 1, chunks, CMP_CHUNK, HEAD_DIM), lambda w, b, g: (w, b, g, 0, 0, 0)),
        out_shape=jax.ShapeDtypeStruct((2, batch, NSA_KV_GROUPS, chunks, CMP_CHUNK, HEAD_DIM), BF16),
        compiler_params=_params(("parallel", "parallel", "parallel")),
        name="nsa_compress",
    )(kr, pe, w1, w2, *tables)


def _attn_body(q_ref, ks_ref, vs_ref, kw_ref, vw_ref, kc_ref, vc_ref, gate_ref, e_ref,
               o_ref, qaug_ref, sc_ref, ps_ref, sa_ref, sb_ref, sd_ref, *, ns, tk):
    qi = pl.program_id(2)
    t0 = qi * Q_BLOCK
    cols = NSA_HPG * Q_BLOCK
    halves = qaug_ref.shape[0]
    for n in range(NSA_HPG):
        qt = q_ref[n].astype(F32).T.astype(BF16)
        for h in range(halves):
            qaug_ref[h, 0:HEAD_DIM, n * Q_BLOCK:(n + 1) * Q_BLOCK] = qt
    q_t = qaug_ref[0, 0:HEAD_DIM, :]
    tq = t0 + (lax.broadcasted_iota(jnp.int32, (1, cols), 1) & (Q_BLOCK - 1))

    nch_total = sc_ref.shape[0] // CMP_CHUNK
    nch = jnp.minimum((t0 + Q_BLOCK - CMP_LEN) // (CMP_STRIDE * CMP_CHUNK) + 1, nch_total)
    crow = lax.broadcasted_iota(jnp.int32, (CMP_CHUNK, cols), 0)

    def cmp_scores(j, m):
        c0 = pl.multiple_of(j * CMP_CHUNK, CMP_CHUNK)
        s = _dot(kc_ref[0, 0, 0, j], q_t)
        s = jnp.where((c0 + crow) * CMP_STRIDE + (CMP_LEN - 1) <= tq, s, NEG_INF)
        sc_ref[pl.ds(c0, CMP_CHUNK), :] = s
        return jnp.maximum(m, jnp.max(s, axis=0, keepdims=True))

    m_c = lax.fori_loop(0, nch, cmp_scores, jnp.full((1, cols), NEG_INF, F32))

    def cmp_probs(j, carry):
        l, acc = carry
        c0 = pl.multiple_of(j * CMP_CHUNK, CMP_CHUNK)
        p = jnp.exp2(sc_ref[pl.ds(c0, CMP_CHUNK), :] - m_c)
        sc_ref[pl.ds(c0, CMP_CHUNK), :] = p
        return l + jnp.sum(p, axis=0, keepdims=True), acc + _dot(vc_ref[0, 0, 0, j], p.astype(BF16))

    l_c, acc_c = lax.fori_loop(0, nch, cmp_probs,
                               (jnp.zeros((1, cols), F32), jnp.zeros((HEAD_DIM, cols), F32)))
    inv_c = jnp.where(tq >= CMP_LEN - 1, 1.0 / l_c, 0.0)
    o_c = acc_c * inv_c

    def cmp_headsum(j, _):
        c0 = pl.multiple_of(j * CMP_CHUNK, CMP_CHUNK)
        p = sc_ref[pl.ds(c0, CMP_CHUNK), :] * inv_c
        ps = p[:, 0:Q_BLOCK]
        for n in range(1, NSA_HPG):
            ps = ps + p[:, n * Q_BLOCK:(n + 1) * Q_BLOCK]
        ps_ref[pl.ds(c0, CMP_CHUNK), :] = ps
        return 0

    lax.fori_loop(0, nch, cmp_headsum, 0)

    def cmp_zero(j, _):
        ps_ref[pl.ds(pl.multiple_of(j * CMP_CHUNK, CMP_CHUNK), CMP_CHUNK), :] = jnp.zeros((CMP_CHUNK, Q_BLOCK), F32)
        return 0

    lax.fori_loop(nch, nch_total, cmp_zero, 0)

    ratio = SEL_BLOCK // CMP_STRIDE
    r = [ps_ref[pl.ds(j, ns, stride=ratio), :] for j in range(ratio)]
    prev = pltpu.roll(r[ratio - 1], 1, 0)
    prev = jnp.where(lax.broadcasted_iota(jnp.int32, (ns, Q_BLOCK), 0) == 0, 0.0, prev)
    imp = 2.0 * (r[0] + r[1] + r[2]) + r[3] + prev
    nsp = halves * SEL_BIAS_BLOCKS
    if nsp > ns:
        imp = jnp.concatenate([imp, jnp.zeros((nsp - ns, Q_BLOCK), F32)], axis=0)

    blk = lax.broadcasted_iota(jnp.int32, (nsp, Q_BLOCK), 0)
    blk_f = blk.astype(F32)
    cur = (t0 + lax.broadcasted_iota(jnp.int32, (nsp, Q_BLOCK), 1)) // SEL_BLOCK
    valid = blk <= cur
    forced = (blk == 0) | (blk == cur) | (blk == cur - 1)
    score = jnp.where(valid, jnp.where(forced, FORCE_SCORE, imp), NEG_INF)
    picked = jnp.zeros((nsp, Q_BLOCK), dtype=jnp.bool_)
    for _ in range(min(SEL_TOPK, ns)):
        top = jnp.max(score, axis=0, keepdims=True)
        first = jnp.min(jnp.where(score == top, blk_f, float(nsp)), axis=0, keepdims=True)
        hit = blk_f == first
        picked = picked | hit
        score = jnp.where(hit, -jnp.inf, score)
    bias = jnp.where(picked & valid, 0.0, NEG_INF).astype(BF16)

    for h in range(halves):
        bh = bias[h * SEL_BIAS_BLOCKS:(h + 1) * SEL_BIAS_BLOCKS]
        for n in range(NSA_HPG):
            qaug_ref[h, HEAD_DIM:HEAD_DIM + SEL_BIAS_BLOCKS, n * Q_BLOCK:(n + 1) * Q_BLOCK] = bh

    init = (jnp.full((1, cols), NEG_INF, F32), jnp.zeros((1, cols), F32),
            jnp.zeros((HEAD_DIM, cols), F32))

    def sel_scores(j, s_ref, diagonal=False):
        k0 = pl.multiple_of(j * tk, tk)
        e0 = pl.multiple_of(k0 % SEL_BIAS_KEYS, tk)
        kaug = jnp.concatenate([ks_ref[0, pl.ds(k0, tk), :], e_ref[pl.ds(e0, tk), :]], axis=1)
        s = _dot(kaug, qaug_ref[k0 // SEL_BIAS_KEYS])
        if diagonal:
            lk = lax.broadcasted_iota(jnp.int32, (tk, cols), 0)
            lq = lax.broadcasted_iota(jnp.int32, (tk, cols), 1) & (Q_BLOCK - 1)
            s = jnp.where(lk - lq <= t0 - k0, s, NEG_INF)
        s_ref[...] = s
        return jnp.max(s, axis=0, keepdims=True)

    def sel_update(s_ref, s_max, j, carry):
        m, l, acc = carry
        m_new = jnp.maximum(m, s_max)
        alpha = jnp.exp2(m - m_new)
        p = jnp.exp2(s_ref[...] - m_new)
        l = alpha * l + jnp.sum(p, axis=0, keepdims=True)
        acc = alpha * acc + _dot(vs_ref[0, j], p.astype(BF16))
        return m_new, l, acc

    n_full = t0 // tk
    last_full = jnp.maximum(n_full - 1, 0)
    max_d = sel_scores(n_full, sd_ref, diagonal=True)
    max_a = sel_scores(0, sa_ref)

    def sel_pair(jj, carry):
        max_a, state = carry
        j = 2 * jj
        max_b = sel_scores(j + 1, sb_ref)
        state = sel_update(sa_ref, max_a, j, state)
        max_a = sel_scores(jnp.minimum(j + 2, last_full), sa_ref)
        return max_a, sel_update(sb_ref, max_b, j + 1, state)

    max_a, state = lax.fori_loop(0, n_full // 2, sel_pair, (max_a, init))
    state = lax.cond(n_full % 2 == 1, lambda c: sel_update(sa_ref, max_a, n_full - 1, c), lambda c: c, state)
    _, l_s, acc_s = sel_update(sd_ref, max_d, n_full, state)
    o_s = acc_s * (1.0 / l_s)

    n_win = WINDOW // Q_BLOCK + 1
    win_keys = n_win * Q_BLOCK

    def window(start, masker):
        s = masker(_dot(kw_ref[0, pl.ds(pl.multiple_of(start, Q_BLOCK), win_keys), :], q_t))
        p = jnp.exp2(s - jnp.max(s, axis=0, keepdims=True))
        inv = 1.0 / jnp.sum(p, axis=0, keepdims=True)
        first = start // Q_BLOCK
        v_t = jnp.concatenate([vw_ref[0, first + w] for w in range(n_win)], axis=1)
        return _dot(v_t, p.astype(BF16)) * inv

    def window_steady():
        lk = lax.broadcasted_iota(jnp.int32, (Q_BLOCK, cols), 0)
        lq = lax.broadcasted_iota(jnp.int32, (Q_BLOCK, cols), 1) & (Q_BLOCK - 1)

        def masker(s):
            return jnp.concatenate([jnp.where(lk > lq, s[:Q_BLOCK], NEG_INF), s[Q_BLOCK:WINDOW],
                                    jnp.where(lk <= lq, s[WINDOW:], NEG_INF)], axis=0)

        return window(t0 - WINDOW, masker)

    def window_start():
        kpos = lax.broadcasted_iota(jnp.int32, (win_keys, cols), 0)
        return window(0, lambda s: jnp.where((kpos <= tq) & (kpos > tq - WINDOW), s, NEG_INF))

    if kw_ref.shape[1] > win_keys:
        o_w = lax.cond(t0 >= WINDOW, window_steady, window_start)
    else:
        o_w = window_start()

    g_t = gate_ref[...].T
    for n in range(NSA_HPG):
        c = slice(n * Q_BLOCK, (n + 1) * Q_BLOCK)
        o = (g_t[n:n + 1] * o_c[:, c] + g_t[NSA_HPG + n:NSA_HPG + n + 1] * o_s[:, c]
             + g_t[2 * NSA_HPG + n:2 * NSA_HPG + n + 1] * o_w[:, c])
        o_ref[:, n * HEAD_DIM:(n + 1) * HEAD_DIM] = o.T.astype(BF16)


def _nsa_attention(hm, vs_t, vw_t, cmp, gates, onehot, batch, seq, *, tk):
    nq = seq // Q_BLOCK
    ns = seq // SEL_BLOCK
    ncp = seq // CMP_STRIDE
    halves = -(-ns // SEL_BIAS_BLOCKS)
    m = batch * seq
    once = pl.Buffered(1)

    def slab(first):
        return pl.BlockSpec((1, seq, HEAD_DIM), lambda b, g, qi: (first + g, b, 0), pipeline_mode=once)

    def slab_t(tile):
        return pl.BlockSpec((1, seq // tile, HEAD_DIM, tile), lambda b, g, qi: (g, b, 0, 0), pipeline_mode=once)

    def cmp_spec(which):
        return pl.BlockSpec((1, 1, 1, ncp // CMP_CHUNK, CMP_CHUNK, HEAD_DIM),
                            lambda b, g, qi: (which, b, g, 0, 0, 0))

    return pl.pallas_call(
        functools.partial(_attn_body, ns=ns, tk=tk),
        grid=(batch, NSA_KV_GROUPS, nq),
        in_specs=[
            pl.BlockSpec((NSA_HPG, Q_BLOCK, HEAD_DIM), lambda b, g, qi: (g, b * nq + qi, 0)),
            slab(HM_KS), slab_t(tk), slab(HM_KW), slab_t(Q_BLOCK),
            cmp_spec(0), cmp_spec(1),
            pl.BlockSpec((Q_BLOCK, LANES), lambda b, g, qi: (b * nq + qi, g)),
            pl.BlockSpec(onehot.shape, lambda b, g, qi: (0, 0), pipeline_mode=once),
        ],
        out_specs=pl.BlockSpec((Q_BLOCK, NSA_HPG * HEAD_DIM), lambda b, g, qi: (b * nq + qi, g)),
        out_shape=jax.ShapeDtypeStruct((m, NSA_HEADS * HEAD_DIM), BF16),
        scratch_shapes=[
            pltpu.VMEM((halves, 2 * HEAD_DIM, NSA_HPG * Q_BLOCK), BF16),
            pltpu.VMEM((ncp, NSA_HPG * Q_BLOCK), F32),
            pltpu.VMEM((ncp, Q_BLOCK), F32),
            pltpu.VMEM((tk, NSA_HPG * Q_BLOCK), F32),
            pltpu.VMEM((tk, NSA_HPG * Q_BLOCK), F32),
            pltpu.VMEM((tk, NSA_HPG * Q_BLOCK), F32),
        ],
        compiler_params=_params(("parallel", "parallel", "arbitrary")),
        name="nsa_attention",
    )(hm, hm, vs_t, hm, vw_t, cmp, cmp, gates, onehot)


def _sel_onehot(seq):
    keys = np.arange(min(seq, SEL_BIAS_KEYS))
    onehot = (keys[:, None] // SEL_BLOCK == np.arange(SEL_BIAS_BLOCKS)[None, :]).astype(np.float32)
    return jnp.asarray(onehot, BF16)


def _tiles_transposed(slabs, tile):
    h, m, d = slabs.shape
    return slabs.reshape(h, m // tile, tile, d).transpose(0, 1, 3, 2)


def _gate_weight(w_gl):
    d = w_gl.shape[0]
    w = w_gl.reshape(d, 3, NSA_KV_GROUPS, NSA_HPG).transpose(0, 2, 1, 3).reshape(d, NSA_KV_GROUPS, 3 * NSA_HPG)
    w = jnp.pad(w, ((0, 0), (0, 0), (0, LANES - 3 * NSA_HPG)))
    return w.reshape(d, NSA_KV_GROUPS * LANES)


def _nsa_layer(x, g_mix, w_in, kc_pe, kc_w1, kc_w2, vc_pe, vc_w1, vc_w2, w_out, batch, seq, *, tk=1024):
    assert SEL_BLOCK == 4 * CMP_STRIDE and CMP_LEN == 2 * CMP_STRIDE and seq % max(tk, CMP_STRIDE * CMP_CHUNK) == 0
    n_main = NSA_HEADS * HEAD_DIM + 6 * NSA_KV_GROUPS * HEAD_DIM
    pos_tables = _rope_tables(jnp.arange(seq))
    hm = _nsa_proj(x, g_mix, w_in[:, :n_main].astype(BF16), pos_tables, seq, tm=1024)
    gates = _norm_mm(x, g_mix, _gate_weight(w_in[:, n_main:]).astype(BF16), act="sigmoid",
                     out_dtype=F32, tm=1024, tn=NSA_KV_GROUPS * LANES)
    rows = seq // CMP_STRIDE
    cmp_tables = _rope_tables(jnp.arange(rows) * CMP_STRIDE + CMP_LEN - 1)
    half = CMP_LEN // 2
    pe = jnp.stack([kc_pe, vc_pe]).reshape(2, 2, 1, half * HEAD_DIM)
    w1 = jnp.stack([kc_w1, vc_w1]).astype(BF16)
    w1 = w1.reshape(2, 2, half * HEAD_DIM, w1.shape[-1])
    w2 = jnp.stack([kc_w2, vc_w2]).astype(BF16)
    kr = hm[HM_KC:HM_KS].reshape(2 * NSA_KV_GROUPS, batch * rows, CMP_STRIDE * HEAD_DIM)
    cmp = _compress(kr, pe, w1, w2, cmp_tables, batch, seq)
    vs_t = _tiles_transposed(hm[HM_VS:HM_KW], tk)
    vw_t = _tiles_transposed(hm[HM_VW:], Q_BLOCK)
    att = _nsa_attention(hm, vs_t, vw_t, cmp, gates, _sel_onehot(seq), batch, seq, tk=tk)
    return _mm_residual(att, w_out.astype(BF16), x, tm=1024, tn=512)


def kernel(x, p, norm_mix, norm_ffn, norm_ple, ffn_up, ffn_down, ple_proj, ple_gate, gm_in, gm_ln_g, gm_ln_b, gm_ws, gm_bs, gm_out, nsa_in, nsa_kc_pe, nsa_kc_w1, nsa_kc_w2, nsa_vc_pe, nsa_vc_w1, nsa_vc_w2, nsa_out, final_norm):
    batch, seq, d = x.shape
    m = batch * seq
    depth = p.shape[0]
    xf = x.reshape(m, d)
    row = lambda v: v.reshape(1, -1)
    for i in range(depth):
        j = i // 2
        if i % 2 == 0:
            z = _norm_mm(xf, row(norm_mix[i]), gm_in[j].astype(BF16), act="gelu", out_dtype=BF16,
                         tm=1024, tn=512)
            xf = _gmlp_gate_out(z, xf, row(gm_ln_g[j]), row(gm_ln_b[j]), gm_ws[j], gm_bs[j].T,
                                gm_out[j].astype(BF16), tm=512)
        else:
            xf = _nsa_layer(xf, row(norm_mix[i]), nsa_in[j], nsa_kc_pe[j], nsa_kc_w1[j], nsa_kc_w2[j],
                            nsa_vc_pe[j], nsa_vc_w1[j], nsa_vc_w2[j], nsa_out[j], batch, seq)
        xf = _ffn(xf, row(norm_ffn[i]), ffn_up[i].astype(BF16), ffn_down[i].astype(BF16), tm=512, tf=512)
        last = i == depth - 1
        xf = _ple(xf, p[i].reshape(m, -1), row(norm_ple[i]), ple_gate[i].astype(BF16),
                  ple_proj[i].astype(BF16), row(final_norm), final=last, tm=256)
    return xf.reshape(batch, seq, d)
```

```python
import functools

import numpy as np
import jax
import jax.numpy as jnp
from jax import lax
from jax.experimental import pallas as pl
from jax.experimental.pallas import tpu as pltpu

F32 = jnp.float32
BF16 = jnp.bfloat16

EPS = 1e-6
HEAD_DIM = 128
NSA_HEADS = 16
NSA_KV_GROUPS = 4
NSA_HPG = NSA_HEADS // NSA_KV_GROUPS
ROT_DIM = HEAD_DIM // 4
ROPE_THETA = 500000.0
CMP_LEN = 32
CMP_STRIDE = 16
SEL_BLOCK = 64
SEL_TOPK = 16
WINDOW = 512
Q_BLOCK = 128
GM_CHUNK = 128
NEG_INF = -1e30
LOG2_E = 1.4426950408889634

LANES = 128
SEL_BIAS_BLOCKS = LANES
SEL_BIAS_KEYS = SEL_BIAS_BLOCKS * SEL_BLOCK
VMEM_LIMIT = 56 * 1024 * 1024


def _params(sem):
    return pltpu.CompilerParams(dimension_semantics=sem, vmem_limit_bytes=VMEM_LIMIT)


def _rmsnorm(x, g):
    return x * lax.rsqrt(jnp.mean(x * x, axis=-1, keepdims=True) + EPS) * g


def _dot(a, b):
    return jnp.dot(a, b, preferred_element_type=F32)


def _rope(x, c, sa, sb):
    return x * c + pltpu.roll(x, LANES - ROT_DIM // 2, 1) * sa + pltpu.roll(x, ROT_DIM // 2, 1) * sb


def _rope_tables(pos):
    half = ROT_DIM // 2
    inv = jnp.power(jnp.float32(ROPE_THETA), -jnp.arange(half, dtype=F32) * 2.0 / ROT_DIM)
    ang = pos.astype(F32)[:, None] * inv[None, :]
    cos, sin = jnp.cos(ang), jnp.sin(ang)
    n = pos.shape[0]
    rest = HEAD_DIM - ROT_DIM
    c = jnp.concatenate([cos, cos, jnp.ones((n, rest), F32)], axis=1)
    sa = jnp.concatenate([-sin, jnp.zeros((n, half + rest), F32)], axis=1)
    sb = jnp.concatenate([jnp.zeros((n, half), F32), sin, jnp.zeros((n, rest), F32)], axis=1)
    return c, sa, sb


def _norm_mm_body(x_ref, g_ref, w_ref, o_ref, h_ref, *, act):
    @pl.when(pl.program_id(1) == 0)
    def _():
        h_ref[...] = _rmsnorm(x_ref[...], g_ref[...]).astype(BF16)

    acc = _dot(h_ref[...], w_ref[...])
    if act == "gelu":
        acc = jax.nn.gelu(acc)
    elif act == "sigmoid":
        acc = jax.nn.sigmoid(acc)
    o_ref[...] = acc.astype(o_ref.dtype)


def _norm_mm(x, g, w, *, act, out_dtype, tm, tn):
    m, d = x.shape
    n = w.shape[1]
    return pl.pallas_call(
        functools.partial(_norm_mm_body, act=act),
        grid=(m // tm, n // tn),
        in_specs=[
            pl.BlockSpec((tm, d), lambda i, j: (i, 0)),
            pl.BlockSpec((1, d), lambda i, j: (0, 0)),
            pl.BlockSpec((d, tn), lambda i, j: (0, j)),
        ],
        out_specs=pl.BlockSpec((tm, tn), lambda i, j: (i, j)),
        out_shape=jax.ShapeDtypeStruct((m, n), out_dtype),
        scratch_shapes=[pltpu.VMEM((tm, d), BF16)],
        compiler_params=_params(("parallel", "arbitrary")),
        name="norm_mm_" + str(act),
    )(x, g, w)


def _gmlp_body(z_ref, x_ref, lg_ref, lb_ref, ws_ref, bs_ref, wo_ref, o_ref, y_ref, *, tm, width):
    groups = ws_ref.shape[0]
    gd = width // groups
    u = z_ref[:, :width]
    v = z_ref[:, width:].astype(F32)
    mu = jnp.mean(v, axis=-1, keepdims=True)
    var = jnp.mean(jnp.square(v - mu), axis=-1, keepdims=True)
    vn = ((v - mu) * lax.rsqrt(var + EPS) * lg_ref[...] + lb_ref[...]).astype(BF16)
    r = lax.broadcasted_iota(jnp.int32, (GM_CHUNK, GM_CHUNK), 0)
    c = lax.broadcasted_iota(jnp.int32, (GM_CHUNK, GM_CHUNK), 1)
    causal = c <= r
    for g in range(groups):
        wg = jnp.where(causal, ws_ref[g], 0.0).astype(BF16)
        bg = bs_ref[:, g:g + 1]
        for ch in range(tm // GM_CHUNK):
            rows = slice(ch * GM_CHUNK, (ch + 1) * GM_CHUNK)
            cols = slice(g * gd, (g + 1) * gd)
            sv = _dot(wg, vn[rows, cols]) + bg
            y_ref[rows, cols] = (u[rows, cols].astype(F32) * sv).astype(BF16)
    o_ref[...] = x_ref[...] + _dot(y_ref[...], wo_ref[...])


def _gmlp_gate_out(z, x, ln_g, ln_b, ws, bs, w_out, *, tm):
    m, d = x.shape
    width = z.shape[1] // 2
    groups = ws.shape[0]
    return pl.pallas_call(
        functools.partial(_gmlp_body, tm=tm, width=width),
        grid=(m // tm,),
        in_specs=[
            pl.BlockSpec((tm, 2 * width), lambda i: (i, 0)),
            pl.BlockSpec((tm, d), lambda i: (i, 0)),
            pl.BlockSpec((1, width), lambda i: (0, 0)),
            pl.BlockSpec((1, width), lambda i: (0, 0)),
            pl.BlockSpec((groups, GM_CHUNK, GM_CHUNK), lambda i: (0, 0, 0)),
            pl.BlockSpec((GM_CHUNK, groups), lambda i: (0, 0)),
            pl.BlockSpec((width, d), lambda i: (0, 0)),
        ],
        out_specs=pl.BlockSpec((tm, d), lambda i: (i, 0)),
        out_shape=jax.ShapeDtypeStruct((m, d), F32),
        scratch_shapes=[pltpu.VMEM((tm, width), BF16)],
        compiler_params=_params(("parallel",)),
        name="gmlp_gate_out",
    )(z, x, ln_g, ln_b, ws, bs, w_out)


def _ffn_body(x_ref, g_ref, wu_ref, wd_ref, o_ref, h_ref):
    f = pl.program_id(1)

    @pl.when(f == 0)
    def _():
        x = x_ref[...]
        h_ref[...] = _rmsnorm(x, g_ref[...]).astype(BF16)
        o_ref[...] = x

    a = jnp.square(jnp.maximum(_dot(h_ref[...], wu_ref[...]), 0.0)).astype(BF16)
    o_ref[...] += _dot(a, wd_ref[...])


def _ffn(x, g, w_up, w_down, *, tm, tf):
    m, d = x.shape
    ff = w_up.shape[1]
    return pl.pallas_call(
        _ffn_body,
        grid=(m // tm, ff // tf),
        in_specs=[
            pl.BlockSpec((tm, d), lambda i, f: (i, 0)),
            pl.BlockSpec((1, d), lambda i, f: (0, 0)),
            pl.BlockSpec((d, tf), lambda i, f: (0, f)),
            pl.BlockSpec((tf, d), lambda i, f: (f, 0)),
        ],
        out_specs=pl.BlockSpec((tm, d), lambda i, f: (i, 0)),
        out_shape=jax.ShapeDtypeStruct((m, d), F32),
        scratch_shapes=[pltpu.VMEM((tm, d), BF16)],
        compiler_params=_params(("parallel", "arbitrary")),
        name="ffn",
    )(x, g, w_up, w_down)


def _ple_body(x_ref, p_ref, g_ref, wg_ref, wp_ref, fg_ref, o_ref, *, final):
    x = x_ref[...]
    h = _rmsnorm(x, g_ref[...]).astype(BF16)
    gate = jax.nn.sigmoid(_dot(h, wg_ref[...]))
    y = x + gate * _dot(p_ref[...].astype(BF16), wp_ref[...])
    if final:
        y = _rmsnorm(y, fg_ref[...])
    o_ref[...] = y


def _ple(x, p, g, w_gate, w_proj, final_g, *, final, tm):
    m, d = x.shape
    pd = p.shape[1]
    return pl.pallas_call(
        functools.partial(_ple_body, final=final),
        grid=(m // tm,),
        in_specs=[
            pl.BlockSpec((tm, d), lambda i: (i, 0)),
            pl.BlockSpec((tm, pd), lambda i: (i, 0)),
            pl.BlockSpec((1, d), lambda i: (0, 0)),
            pl.BlockSpec((d, d), lambda i: (0, 0)),
            pl.BlockSpec((pd, d), lambda i: (0, 0)),
            pl.BlockSpec((1, d), lambda i: (0, 0)),
        ],
        out_specs=pl.BlockSpec((tm, d), lambda i: (i, 0)),
        out_shape=jax.ShapeDtypeStruct((m, d), F32),
        compiler_params=_params(("parallel",)),
        name="ple",
    )(x, p, g, w_gate, w_proj, final_g)


def _mm_res_body(a_ref, w_ref, x_ref, o_ref):
    o_ref[...] = x_ref[...] + _dot(a_ref[...], w_ref[...])


def _mm_residual(a, w, x, *, tm, tn):
    m, k = a.shape
    n = w.shape[1]
    return pl.pallas_call(
        _mm_res_body,
        grid=(m // tm, n // tn),
        in_specs=[
            pl.BlockSpec((tm, k), lambda i, j: (i, 0)),
            pl.BlockSpec((k, tn), lambda i, j: (0, j)),
            pl.BlockSpec((tm, tn), lambda i, j: (i, j)),
        ],
        out_specs=pl.BlockSpec((tm, tn), lambda i, j: (i, j)),
        out_shape=jax.ShapeDtypeStruct((m, n), F32),
        compiler_params=_params(("parallel", "arbitrary")),
        name="mm_residual",
    )(a, w, x)


NSA_PROJ_TN = 4 * HEAD_DIM
NSA_Q_TILES = NSA_HEADS * HEAD_DIM // NSA_PROJ_TN
NSA_KS_TILE = NSA_Q_TILES + 2
NSA_KW_TILE = NSA_Q_TILES + 4
HM_KC, HM_VC, HM_KS, HM_VS, HM_KW, HM_VW = (NSA_HEADS + NSA_KV_GROUPS * i for i in range(6))


def _nsa_proj_body(x_ref, g_ref, w_ref, c_ref, sa_ref, sb_ref, o_ref, h_ref):
    j = pl.program_id(1)

    @pl.when(j == 0)
    def _():
        h_ref[...] = _rmsnorm(x_ref[...], g_ref[...]).astype(BF16)

    acc = _dot(h_ref[...], w_ref[...])
    heads = NSA_PROJ_TN // HEAD_DIM
    is_q = j < NSA_Q_TILES
    is_rope = is_q | (j == NSA_KS_TILE) | (j == NSA_KW_TILE)

    @pl.when(is_rope)
    def _():
        scale = jnp.where(is_q, HEAD_DIM ** -0.5 * LOG2_E, 1.0).astype(F32)
        c, sa, sb = c_ref[...], sa_ref[...], sb_ref[...]
        for hh in range(heads):
            seg = acc[:, hh * HEAD_DIM:(hh + 1) * HEAD_DIM]
            o_ref[hh] = (_rope(seg, c, sa, sb) * scale).astype(BF16)

    @pl.when(jnp.logical_not(is_rope))
    def _():
        for hh in range(heads):
            o_ref[hh] = acc[:, hh * HEAD_DIM:(hh + 1) * HEAD_DIM].astype(BF16)


def _nsa_proj(x, g, w, tables, seq, *, tm):
    m, d = x.shape
    n = w.shape[1]
    heads = NSA_PROJ_TN // HEAD_DIM
    tpb = seq // tm
    tab_spec = pl.BlockSpec((tm, HEAD_DIM), lambda i, j: (i % tpb, 0))
    return pl.pallas_call(
        _nsa_proj_body,
        grid=(m // tm, n // NSA_PROJ_TN),
        in_specs=[
            pl.BlockSpec((tm, d), lambda i, j: (i, 0)),
            pl.BlockSpec((1, d), lambda i, j: (0, 0)),
            pl.BlockSpec((d, NSA_PROJ_TN), lambda i, j: (0, j)),
            tab_spec, tab_spec, tab_spec,
        ],
        out_specs=pl.BlockSpec((heads, tm, HEAD_DIM), lambda i, j: (j, i, 0)),
        out_shape=jax.ShapeDtypeStruct((n // HEAD_DIM, m, HEAD_DIM), BF16),
        scratch_shapes=[pltpu.VMEM((tm, d), BF16)],
        compiler_params=_params(("parallel", "arbitrary")),
        name="nsa_proj",
    )(x, g, w, *tables)


CMP_CHUNK = LANES


def _compress_body(kr_ref, pe_ref, w1_ref, w2_ref, c_ref, sa_ref, sb_ref, o_ref):
    kr = kr_ref[0].astype(F32)
    rows = kr.shape[0]
    a = _dot((kr + pe_ref[0, 0]).astype(BF16), w1_ref[0, 0])
    b = _dot((kr + pe_ref[0, 1]).astype(BF16), w1_ref[0, 1])
    hid = jax.nn.gelu(a + pltpu.roll(b, rows - 1, 0))
    out = _dot(hid.astype(BF16), w2_ref[0])
    is_k = pl.program_id(0) == 0

    @pl.when(is_k)
    def _():
        roped = _rope(out, c_ref[...], sa_ref[...], sb_ref[...]).astype(BF16)
        for j in range(rows // CMP_CHUNK):
            o_ref[0, 0, 0, j] = roped[j * CMP_CHUNK:(j + 1) * CMP_CHUNK]

    @pl.when(jnp.logical_not(is_k))
    def _():
        for j in range(rows // CMP_CHUNK):
            o_ref[0, 0, 0, j] = out[j * CMP_CHUNK:(j + 1) * CMP_CHUNK].T.astype(BF16)


def _compress(kr, pe, w1, w2, tables, batch, seq):
    rows = seq // CMP_STRIDE
    kdim = CMP_STRIDE * HEAD_DIM
    hid = w1.shape[-1]
    tab_spec = pl.BlockSpec((rows, HEAD_DIM), lambda w, b, g: (0, 0))
    chunks = rows // CMP_CHUNK
    return pl.pallas_call(
        _compress_body,
        grid=(2, batch, NSA_KV_GROUPS),
        in_specs=[
            pl.BlockSpec((1, rows, kdim), lambda w, b, g: (NSA_KV_GROUPS * w + g, b, 0)),
            pl.BlockSpec((1, 2, 1, kdim), lambda w, b, g: (w, 0, 0, 0)),
            pl.BlockSpec((1, 2, kdim, hid), lambda w, b, g: (w, 0, 0, 0)),
            pl.BlockSpec((1, hid, HEAD_DIM), lambda w, b, g: (w, 0, 0)),
            tab_spec, tab_spec, tab_spec,
        ],
        out_specs=pl.BlockSpec((1, 1, 1, chunks, CMP_CHUNK, HEAD_DIM), lambda w, b, g: (w, b, g, 0, 0, 0)),
        out_shape=jax.ShapeDtypeStruct((2, batch, NSA_KV_GROUPS, chunks, CMP_CHUNK, HEAD_DIM), BF16),
        compiler_params=_params(("parallel", "parallel", "parallel")),
        name="nsa_compress",
    )(kr, pe, w1, w2, *tables)


def _attn_body(q_ref, ks_ref, vs_ref, kw_ref, vw_ref, kc_ref, vc_ref, gate_ref, e_ref,
               o_ref, qaug_ref, sc_ref, ps_ref, sa_ref, sb_ref, sd_ref, *, ns, tk):
    qi = pl.program_id(2)
    t0 = qi * Q_BLOCK
    cols = NSA_HPG * Q_BLOCK
    halves = qaug_ref.shape[0]
    for n in range(NSA_HPG):
        qt = q_ref[n].astype(F32).T.astype(BF16)
        for h in range(halves):
            qaug_ref[h, 0:HEAD_DIM, n * Q_BLOCK:(n + 1) * Q_BLOCK] = qt
    q_t = qaug_ref[0, 0:HEAD_DIM, :]
    tq = t0 + (lax.broadcasted_iota(jnp.int32, (1, cols), 1) & (Q_BLOCK - 1))

    sub = min(2, sc_ref.shape[0] // CMP_CHUNK)
    step = sub * CMP_CHUNK
    nch_total = sc_ref.shape[0] // step
    nch = jnp.minimum((t0 + Q_BLOCK - CMP_LEN) // (CMP_STRIDE * step) + 1, nch_total)
    crow = lax.broadcasted_iota(jnp.int32, (step, cols), 0)

    def cmp_scores(j, m):
        c0 = pl.multiple_of(j * step, step)
        kc = jnp.concatenate([kc_ref[0, 0, 0, sub * j + i] for i in range(sub)], axis=0)
        s = _dot(kc, q_t)
        s = jnp.where((c0 + crow) * CMP_STRIDE + (CMP_LEN - 1) <= tq, s, NEG_INF)
        sc_ref[pl.ds(c0, step), :] = s
        return jnp.maximum(m, jnp.max(s, axis=0, keepdims=True))

    m_c = lax.fori_loop(0, nch, cmp_scores, jnp.full((1, cols), NEG_INF, F32))

    def cmp_probs(j, carry):
        l, acc = carry
        c0 = pl.multiple_of(j * step, step)
        p = jnp.exp2(sc_ref[pl.ds(c0, step), :] - m_c)
        sc_ref[pl.ds(c0, step), :] = p
        vc_t = jnp.concatenate([vc_ref[0, 0, 0, sub * j + i] for i in range(sub)], axis=1)
        return l + jnp.sum(p, axis=0, keepdims=True), acc + _dot(vc_t, p.astype(BF16))

    l_c, acc_c = lax.fori_loop(0, nch, cmp_probs,
                               (jnp.zeros((1, cols), F32), jnp.zeros((HEAD_DIM, cols), F32)))
    inv_c = jnp.where(tq >= CMP_LEN - 1, 1.0 / l_c, 0.0)
    o_c = acc_c * inv_c

    def cmp_headsum(j, _):
        c0 = pl.multiple_of(j * step, step)
        p = sc_ref[pl.ds(c0, step), :] * inv_c
        ps = p[:, 0:Q_BLOCK]
        for n in range(1, NSA_HPG):
            ps = ps + p[:, n * Q_BLOCK:(n + 1) * Q_BLOCK]
        ps_ref[pl.ds(c0, step), :] = ps
        return 0

    lax.fori_loop(0, nch, cmp_headsum, 0)

    def cmp_zero(j, _):
        ps_ref[pl.ds(pl.multiple_of(j * step, step), step), :] = jnp.zeros((step, Q_BLOCK), F32)
        return 0

    lax.fori_loop(nch, nch_total, cmp_zero, 0)

    n_win = WINDOW // Q_BLOCK + 1
    win_keys = n_win * Q_BLOCK
    w0 = pl.multiple_of(jnp.maximum(t0 - WINDOW, 0), Q_BLOCK)
    rel = (lax.broadcasted_iota(jnp.int32, (win_keys, cols), 0) + (w0 - t0)
           - (lax.broadcasted_iota(jnp.int32, (win_keys, cols), 1) & (Q_BLOCK - 1)))
    s_w = jnp.where((rel <= 0) & (rel > -WINDOW), _dot(kw_ref[0, pl.ds(w0, win_keys), :], q_t), NEG_INF)
    p_w = jnp.exp2(s_w - jnp.max(s_w, axis=0, keepdims=True))
    vw_t = jnp.concatenate([vw_ref[0, w0 // Q_BLOCK + w] for w in range(n_win)], axis=1)
    o_w = _dot(vw_t, p_w.astype(BF16)) * (1.0 / jnp.sum(p_w, axis=0, keepdims=True))

    ratio = SEL_BLOCK // CMP_STRIDE
    r = [ps_ref[pl.ds(j, ns, stride=ratio), :] for j in range(ratio)]
    prev = pltpu.roll(r[ratio - 1], 1, 0)
    prev = jnp.where(lax.broadcasted_iota(jnp.int32, (ns, Q_BLOCK), 0) == 0, 0.0, prev)
    imp = 2.0 * (r[0] + r[1] + r[2]) + r[3] + prev
    nsp = halves * SEL_BIAS_BLOCKS
    if nsp > ns:
        imp = jnp.concatenate([imp, jnp.zeros((nsp - ns, Q_BLOCK), F32)], axis=0)

    blk = lax.broadcasted_iota(jnp.int32, (nsp, Q_BLOCK), 0)
    blk_f = blk.astype(F32)
    cur = (t0 + lax.broadcasted_iota(jnp.int32, (nsp, Q_BLOCK), 1)) // SEL_BLOCK
    valid = blk <= cur
    forced = (blk == 0) | (blk == cur) | (blk == cur - 1)
    n_forced = 3
    score = jnp.where(valid & jnp.logical_not(forced), imp, NEG_INF)
    picked = forced
    for _ in range(min(SEL_TOPK, ns) - n_forced):
        top = jnp.max(score, axis=0, keepdims=True)
        first = jnp.min(jnp.where(score == top, blk_f, float(nsp)), axis=0, keepdims=True)
        hit = blk_f == first
        picked = picked | hit
        score = jnp.where(hit, -jnp.inf, score)
    bias = jnp.where(picked & valid, 0.0, NEG_INF).astype(BF16)

    for h in range(halves):
        bh = bias[h * SEL_BIAS_BLOCKS:(h + 1) * SEL_BIAS_BLOCKS]
        for n in range(NSA_HPG):
            qaug_ref[h, HEAD_DIM:HEAD_DIM + SEL_BIAS_BLOCKS, n * Q_BLOCK:(n + 1) * Q_BLOCK] = bh

    init = (jnp.full((1, cols), NEG_INF, F32), jnp.zeros((1, cols), F32),
            jnp.zeros((HEAD_DIM, cols), F32))

    def sel_scores(j, s_ref, diagonal=False):
        k0 = pl.multiple_of(j * tk, tk)
        e0 = pl.multiple_of(k0 % SEL_BIAS_KEYS, tk)
        kaug = jnp.concatenate([ks_ref[0, pl.ds(k0, tk), :], e_ref[pl.ds(e0, tk), :]], axis=1)
        s = _dot(kaug, qaug_ref[k0 // SEL_BIAS_KEYS])
        if diagonal:
            lk = lax.broadcasted_iota(jnp.int32, (tk, cols), 0)
            lq = lax.broadcasted_iota(jnp.int32, (tk, cols), 1) & (Q_BLOCK - 1)
            s = jnp.where(lk - lq <= t0 - k0, s, NEG_INF)
        s_ref[...] = s
        return jnp.max(s, axis=0, keepdims=True)

    def sel_update(s_ref, s_max, j, carry):
        m, l, acc = carry
        m_new = jnp.maximum(m, s_max)
        alpha = jnp.exp2(m - m_new)
        p = jnp.exp2(s_ref[...] - m_new)
        l = alpha * l + jnp.sum(p, axis=0, keepdims=True)
        acc = alpha * acc + _dot(vs_ref[0, j], p.astype(BF16))
        return m_new, l, acc

    n_full = t0 // tk
    last_full = jnp.maximum(n_full - 1, 0)
    max_d = sel_scores(n_full, sd_ref, diagonal=True)
    max_a = sel_scores(0, sa_ref)

    def sel_pair(jj, carry):
        max_a, state = carry
        j = 2 * jj
        max_b = sel_scores(j + 1, sb_ref)
        state = sel_update(sa_ref, max_a, j, state)
        max_a = sel_scores(jnp.minimum(j + 2, last_full), sa_ref)
        return max_a, sel_update(sb_ref, max_b, j + 1, state)

    max_a, state = lax.fori_loop(0, n_full // 2, sel_pair, (max_a, init))
    state = lax.cond(n_full % 2 == 1, lambda c: sel_update(sa_ref, max_a, n_full - 1, c), lambda c: c, state)
    _, l_s, acc_s = sel_update(sd_ref, max_d, n_full, state)
    o_s = acc_s * (1.0 / l_s)

    g_t = gate_ref[...].T
    for n in range(NSA_HPG):
        c = slice(n * Q_BLOCK, (n + 1) * Q_BLOCK)
        o = (g_t[n:n + 1] * o_c[:, c] + g_t[NSA_HPG + n:NSA_HPG + n + 1] * o_s[:, c]
             + g_t[2 * NSA_HPG + n:2 * NSA_HPG + n + 1] * o_w[:, c])
        o_ref[:, n * HEAD_DIM:(n + 1) * HEAD_DIM] = o.T.astype(BF16)


def _nsa_attention(hm, vs_t, vw_t, cmp, gates, onehot, batch, seq, *, tk):
    nq = seq // Q_BLOCK
    ns = seq // SEL_BLOCK
    ncp = seq // CMP_STRIDE
    halves = -(-ns // SEL_BIAS_BLOCKS)
    m = batch * seq
    once = pl.Buffered(1)

    def slab(first):
        return pl.BlockSpec((1, seq, HEAD_DIM), lambda b, g, qi: (first + g, b, 0), pipeline_mode=once)

    def slab_t(tile):
        return pl.BlockSpec((1, seq // tile, HEAD_DIM, tile), lambda b, g, qi: (g, b, 0, 0), pipeline_mode=once)

    def cmp_spec(which):
        return pl.BlockSpec((1, 1, 1, ncp // CMP_CHUNK, CMP_CHUNK, HEAD_DIM),
                            lambda b, g, qi: (which, b, g, 0, 0, 0))

    return pl.pallas_call(
        functools.partial(_attn_body, ns=ns, tk=tk),
        grid=(batch, NSA_KV_GROUPS, nq),
        in_specs=[
            pl.BlockSpec((NSA_HPG, Q_BLOCK, HEAD_DIM), lambda b, g, qi: (g, b * nq + qi, 0)),
            slab(HM_KS), slab_t(tk), slab(HM_KW), slab_t(Q_BLOCK),
            cmp_spec(0), cmp_spec(1),
            pl.BlockSpec((Q_BLOCK, LANES), lambda b, g, qi: (b * nq + qi, g)),
            pl.BlockSpec(onehot.shape, lambda b, g, qi: (0, 0), pipeline_mode=once),
        ],
        out_specs=pl.BlockSpec((Q_BLOCK, NSA_HPG * HEAD_DIM), lambda b, g, qi: (b * nq + qi, g)),
        out_shape=jax.ShapeDtypeStruct((m, NSA_HEADS * HEAD_DIM), BF16),
        scratch_shapes=[
            pltpu.VMEM((halves, 2 * HEAD_DIM, NSA_HPG * Q_BLOCK), BF16),
            pltpu.VMEM((ncp, NSA_HPG * Q_BLOCK), F32),
            pltpu.VMEM((ncp, Q_BLOCK), F32),
            pltpu.VMEM((tk, NSA_HPG * Q_BLOCK), F32),
            pltpu.VMEM((tk, NSA_HPG * Q_BLOCK), F32),
            pltpu.VMEM((tk, NSA_HPG * Q_BLOCK), F32),
        ],
        compiler_params=_params(("parallel", "parallel", "arbitrary")),
        name="nsa_attention",
    )(hm, hm, vs_t, hm, vw_t, cmp, cmp, gates, onehot)


def _sel_onehot(seq):
    keys = np.arange(min(seq, SEL_BIAS_KEYS))
    onehot = (keys[:, None] // SEL_BLOCK == np.arange(SEL_BIAS_BLOCKS)[None, :]).astype(np.float32)
    return jnp.asarray(onehot, BF16)


def _tiles_transposed(slabs, tile):
    h, m, d = slabs.shape
    return slabs.reshape(h, m // tile, tile, d).transpose(0, 1, 3, 2)


def _gate_weight(w_gl):
    d = w_gl.shape[0]
    w = w_gl.reshape(d, 3, NSA_KV_GROUPS, NSA_HPG).transpose(0, 2, 1, 3).reshape(d, NSA_KV_GROUPS, 3 * NSA_HPG)
    w = jnp.pad(w, ((0, 0), (0, 0), (0, LANES - 3 * NSA_HPG)))
    return w.reshape(d, NSA_KV_GROUPS * LANES)


def _nsa_layer(x, g_mix, w_in, kc_pe, kc_w1, kc_w2, vc_pe, vc_w1, vc_w2, w_out, batch, seq, *, tk=1024):
    assert SEL_BLOCK == 4 * CMP_STRIDE and CMP_LEN == 2 * CMP_STRIDE and seq % max(tk, CMP_STRIDE * CMP_CHUNK) == 0
    n_main = NSA_HEADS * HEAD_DIM + 6 * NSA_KV_GROUPS * HEAD_DIM
    pos_tables = _rope_tables(jnp.arange(seq))
    hm = _nsa_proj(x, g_mix, w_in[:, :n_main].astype(BF16), pos_tables, seq, tm=1024)
    gates = _norm_mm(x, g_mix, _gate_weight(w_in[:, n_main:]).astype(BF16), act="sigmoid",
                     out_dtype=F32, tm=1024, tn=NSA_KV_GROUPS * LANES)
    rows = seq // CMP_STRIDE
    cmp_tables = _rope_tables(jnp.arange(rows) * CMP_STRIDE + CMP_LEN - 1)
    half = CMP_LEN // 2
    pe = jnp.stack([kc_pe, vc_pe]).reshape(2, 2, 1, half * HEAD_DIM)
    w1 = jnp.stack([kc_w1, vc_w1]).astype(BF16)
    w1 = w1.reshape(2, 2, half * HEAD_DIM, w1.shape[-1])
    w2 = jnp.stack([kc_w2, vc_w2]).astype(BF16)
    kr = hm[HM_KC:HM_KS].reshape(2 * NSA_KV_GROUPS, batch * rows, CMP_STRIDE * HEAD_DIM)
    cmp = _compress(kr, pe, w1, w2, cmp_tables, batch, seq)
    vs_t = _tiles_transposed(hm[HM_VS:HM_KW], tk)
    vw_t = _tiles_transposed(hm[HM_VW:], Q_BLOCK)
    att = _nsa_attention(hm, vs_t, vw_t, cmp, gates, _sel_onehot(seq), batch, seq, tk=tk)
    return _mm_residual(att, w_out.astype(BF16), x, tm=1024, tn=512)


def kernel(x, p, norm_mix, norm_ffn, norm_ple, ffn_up, ffn_down, ple_proj, ple_gate, gm_in, gm_ln_g, gm_ln_b, gm_ws, gm_bs, gm_out, nsa_in, nsa_kc_pe, nsa_kc_w1, nsa_kc_w2, nsa_vc_pe, nsa_vc_w1, nsa_vc_w2, nsa_out, final_norm):
    batch, seq, d = x.shape
    m = batch * seq
    depth = p.shape[0]
    xf = x.reshape(m, d)
    row = lambda v: v.reshape(1, -1)
    for i in range(depth):
        j = i // 2
        if i % 2 == 0:
            z = _norm_mm(xf, row(norm_mix[i]), gm_in[j].astype(BF16), act="gelu", out_dtype=BF16,
                         tm=1024, tn=512)
            xf = _gmlp_gate_out(z, xf, row(gm_ln_g[j]), row(gm_ln_b[j]), gm_ws[j], gm_bs[j].T,
                                gm_out[j].astype(BF16), tm=512)
        else:
            xf = _nsa_layer(xf, row(norm_mix[i]), nsa_in[j], nsa_kc_pe[j], nsa_kc_w1[j], nsa_kc_w2[j],
                            nsa_vc_pe[j], nsa_vc_w1[j], nsa_vc_w2[j], nsa_out[j], batch, seq)
        xf = _ffn(xf, row(norm_ffn[i]), ffn_up[i].astype(BF16), ffn_down[i].astype(BF16), tm=1024, tf=512)
        last = i == depth - 1
        xf = _ple(xf, p[i].reshape(m, -1), row(norm_ple[i]), ple_gate[i].astype(BF16),
                  ple_proj[i].astype(BF16), row(final_norm), final=last, tm=256)
    return xf.reshape(batch, seq, d)
```

```python
import functools

import numpy as np
import jax
import jax.numpy as jnp
from jax import lax
from jax.experimental import pallas as pl
from jax.experimental.pallas import tpu as pltpu

F32 = jnp.float32
BF16 = jnp.bfloat16

EPS = 1e-6
HEAD_DIM = 128
NSA_HEADS = 16
NSA_KV_GROUPS = 4
NSA_HPG = NSA_HEADS // NSA_KV_GROUPS
ROT_DIM = HEAD_DIM // 4
ROPE_THETA = 500000.0
CMP_LEN = 32
CMP_STRIDE = 16
SEL_BLOCK = 64
SEL_TOPK = 16
WINDOW = 512
Q_BLOCK = 128
GM_CHUNK = 128
NEG_INF = -1e30
LOG2_E = 1.4426950408889634

LANES = 128
SEL_BIAS_BLOCKS = LANES
SEL_BIAS_KEYS = SEL_BIAS_BLOCKS * SEL_BLOCK
VMEM_LIMIT = 56 * 1024 * 1024


def _params(sem):
    return pltpu.CompilerParams(dimension_semantics=sem, vmem_limit_bytes=VMEM_LIMIT)


def _rmsnorm(x, g):
    return x * lax.rsqrt(jnp.mean(x * x, axis=-1, keepdims=True) + EPS) * g


def _dot(a, b):
    return jnp.dot(a, b, preferred_element_type=F32)


def _rope(x, c, sa, sb):
    return x * c + pltpu.roll(x, LANES - ROT_DIM // 2, 1) * sa + pltpu.roll(x, ROT_DIM // 2, 1) * sb


def _rope_tables(pos):
    half = ROT_DIM // 2
    inv = jnp.power(jnp.float32(ROPE_THETA), -jnp.arange(half, dtype=F32) * 2.0 / ROT_DIM)
    ang = pos.astype(F32)[:, None] * inv[None, :]
    cos, sin = jnp.cos(ang), jnp.sin(ang)
    n = pos.shape[0]
    rest = HEAD_DIM - ROT_DIM
    c = jnp.concatenate([cos, cos, jnp.ones((n, rest), F32)], axis=1)
    sa = jnp.concatenate([-sin, jnp.zeros((n, half + rest), F32)], axis=1)
    sb = jnp.concatenate([jnp.zeros((n, half), F32), sin, jnp.zeros((n, rest), F32)], axis=1)
    return c, sa, sb


def _norm_mm_body(x_ref, g_ref, w_ref, o_ref, h_ref, *, act):
    @pl.when(pl.program_id(1) == 0)
    def _():
        h_ref[...] = _rmsnorm(x_ref[...], g_ref[...]).astype(BF16)

    acc = _dot(h_ref[...], w_ref[...])
    if act == "gelu":
        acc = jax.nn.gelu(acc)
    elif act == "sigmoid":
        acc = jax.nn.sigmoid(acc)
    o_ref[...] = acc.astype(o_ref.dtype)


def _norm_mm(x, g, w, *, act, out_dtype, tm, tn):
    m, d = x.shape
    n = w.shape[1]
    return pl.pallas_call(
        functools.partial(_norm_mm_body, act=act),
        grid=(m // tm, n // tn),
        in_specs=[
            pl.BlockSpec((tm, d), lambda i, j: (i, 0)),
            pl.BlockSpec((1, d), lambda i, j: (0, 0)),
            pl.BlockSpec((d, tn), lambda i, j: (0, j)),
        ],
        out_specs=pl.BlockSpec((tm, tn), lambda i, j: (i, j)),
        out_shape=jax.ShapeDtypeStruct((m, n), out_dtype),
        scratch_shapes=[pltpu.VMEM((tm, d), BF16)],
        compiler_params=_params(("parallel", "arbitrary")),
        name="norm_mm_" + str(act),
    )(x, g, w)


def _gmlp_body(z_ref, x_ref, lg_ref, lb_ref, ws_ref, bs_ref, wo_ref, o_ref, y_ref, *, tm, width):
    groups = ws_ref.shape[0]
    gd = width // groups
    u = z_ref[:, :width]
    v = z_ref[:, width:].astype(F32)
    mu = jnp.mean(v, axis=-1, keepdims=True)
    var = jnp.mean(jnp.square(v - mu), axis=-1, keepdims=True)
    vn = ((v - mu) * lax.rsqrt(var + EPS) * lg_ref[...] + lb_ref[...]).astype(BF16)
    r = lax.broadcasted_iota(jnp.int32, (GM_CHUNK, GM_CHUNK), 0)
    c = lax.broadcasted_iota(jnp.int32, (GM_CHUNK, GM_CHUNK), 1)
    causal = c <= r
    for g in range(groups):
        wg = jnp.where(causal, ws_ref[g], 0.0).astype(BF16)
        bg = bs_ref[:, g:g + 1]
        for ch in range(tm // GM_CHUNK):
            rows = slice(ch * GM_CHUNK, (ch + 1) * GM_CHUNK)
            cols = slice(g * gd, (g + 1) * gd)
            sv = _dot(wg, vn[rows, cols]) + bg
            y_ref[rows, cols] = (u[rows, cols].astype(F32) * sv).astype(BF16)
    o_ref[...] = x_ref[...] + _dot(y_ref[...], wo_ref[...])


def _gmlp_gate_out(z, x, ln_g, ln_b, ws, bs, w_out, *, tm):
    m, d = x.shape
    width = z.shape[1] // 2
    groups = ws.shape[0]
    return pl.pallas_call(
        functools.partial(_gmlp_body, tm=tm, width=width),
        grid=(m // tm,),
        in_specs=[
            pl.BlockSpec((tm, 2 * width), lambda i: (i, 0)),
            pl.BlockSpec((tm, d), lambda i: (i, 0)),
            pl.BlockSpec((1, width), lambda i: (0, 0)),
            pl.BlockSpec((1, width), lambda i: (0, 0)),
            pl.BlockSpec((groups, GM_CHUNK, GM_CHUNK), lambda i: (0, 0, 0)),
            pl.BlockSpec((GM_CHUNK, groups), lambda i: (0, 0)),
            pl.BlockSpec((width, d), lambda i: (0, 0)),
        ],
        out_specs=pl.BlockSpec((tm, d), lambda i: (i, 0)),
        out_shape=jax.ShapeDtypeStruct((m, d), F32),
        scratch_shapes=[pltpu.VMEM((tm, width), BF16)],
        compiler_params=_params(("parallel",)),
        name="gmlp_gate_out",
    )(z, x, ln_g, ln_b, ws, bs, w_out)


def _ffn_body(x_ref, g_ref, wu_ref, wd_ref, o_ref, h_ref):
    f = pl.program_id(1)

    @pl.when(f == 0)
    def _():
        x = x_ref[...]
        h_ref[...] = _rmsnorm(x, g_ref[...]).astype(BF16)
        o_ref[...] = x

    a = jnp.square(jnp.maximum(_dot(h_ref[...], wu_ref[...]), 0.0)).astype(BF16)
    o_ref[...] += _dot(a, wd_ref[...])


def _ffn(x, g, w_up, w_down, *, tm, tf):
    m, d = x.shape
    ff = w_up.shape[1]
    return pl.pallas_call(
        _ffn_body,
        grid=(m // tm, ff // tf),
        in_specs=[
            pl.BlockSpec((tm, d), lambda i, f: (i, 0)),
            pl.BlockSpec((1, d), lambda i, f: (0, 0)),
            pl.BlockSpec((d, tf), lambda i, f: (0, f)),
            pl.BlockSpec((tf, d), lambda i, f: (f, 0)),
        ],
        out_specs=pl.BlockSpec((tm, d), lambda i, f: (i, 0)),
        out_shape=jax.ShapeDtypeStruct((m, d), F32),
        scratch_shapes=[pltpu.VMEM((tm, d), BF16)],
        compiler_params=_params(("parallel", "arbitrary")),
        name="ffn",
    )(x, g, w_up, w_down)


def _ple_body(x_ref, p_ref, g_ref, wg_ref, wp_ref, fg_ref, o_ref, *, final):
    x = x_ref[...]
    h = _rmsnorm(x, g_ref[...]).astype(BF16)
    gate = jax.nn.sigmoid(_dot(h, wg_ref[...]))
    y = x + gate * _dot(p_ref[...].astype(BF16), wp_ref[...])
    if final:
        y = _rmsnorm(y, fg_ref[...])
    o_ref[...] = y


def _ple(x, p, g, w_gate, w_proj, final_g, *, final, tm):
    m, d = x.shape
    pd = p.shape[1]
    return pl.pallas_call(
        functools.partial(_ple_body, final=final),
        grid=(m // tm,),
        in_specs=[
            pl.BlockSpec((tm, d), lambda i: (i, 0)),
            pl.BlockSpec((tm, pd), lambda i: (i, 0)),
            pl.BlockSpec((1, d), lambda i: (0, 0)),
            pl.BlockSpec((d, d), lambda i: (0, 0)),
            pl.BlockSpec((pd, d), lambda i: (0, 0)),
            pl.BlockSpec((1, d), lambda i: (0, 0)),
        ],
        out_specs=pl.BlockSpec((tm, d), lambda i: (i, 0)),
        out_shape=jax.ShapeDtypeStruct((m, d), F32),
        compiler_params=_params(("parallel",)),
        name="ple",
    )(x, p, g, w_gate, w_proj, final_g)


def _mm_res_body(a_ref, w_ref, x_ref, o_ref):
    o_ref[...] = x_ref[...] + _dot(a_ref[...], w_ref[...])


def _mm_residual(a, w, x, *, tm, tn):
    m, k = a.shape
    n = w.shape[1]
    return pl.pallas_call(
        _mm_res_body,
        grid=(m // tm, n // tn),
        in_specs=[
            pl.BlockSpec((tm, k), lambda i, j: (i, 0)),
            pl.BlockSpec((k, tn), lambda i, j: (0, j)),
            pl.BlockSpec((tm, tn), lambda i, j: (i, j)),
        ],
        out_specs=pl.BlockSpec((tm, tn), lambda i, j: (i, j)),
        out_shape=jax.ShapeDtypeStruct((m, n), F32),
        compiler_params=_params(("parallel", "arbitrary")),
        name="mm_residual",
    )(a, w, x)


NSA_PROJ_TN = 4 * HEAD_DIM
NSA_Q_TILES = NSA_HEADS * HEAD_DIM // NSA_PROJ_TN
NSA_KS_TILE = NSA_Q_TILES + 2
NSA_KW_TILE = NSA_Q_TILES + 4
HM_KC, HM_VC, HM_KS, HM_VS, HM_KW, HM_VW = (NSA_HEADS + NSA_KV_GROUPS * i for i in range(6))


def _nsa_proj_body(x_ref, g_ref, w_ref, c_ref, sa_ref, sb_ref, o_ref, h_ref):
    j = pl.program_id(1)

    @pl.when(j == 0)
    def _():
        h_ref[...] = _rmsnorm(x_ref[...], g_ref[...]).astype(BF16)

    acc = _dot(h_ref[...], w_ref[...])
    heads = NSA_PROJ_TN // HEAD_DIM
    is_q = j < NSA_Q_TILES
    is_rope = is_q | (j == NSA_KS_TILE) | (j == NSA_KW_TILE)

    @pl.when(is_rope)
    def _():
        scale = jnp.where(is_q, HEAD_DIM ** -0.5 * LOG2_E, 1.0).astype(F32)
        c, sa, sb = c_ref[...], sa_ref[...], sb_ref[...]
        for hh in range(heads):
            seg = acc[:, hh * HEAD_DIM:(hh + 1) * HEAD_DIM]
            o_ref[hh] = (_rope(seg, c, sa, sb) * scale).astype(BF16)

    @pl.when(jnp.logical_not(is_rope))
    def _():
        for hh in range(heads):
            o_ref[hh] = acc[:, hh * HEAD_DIM:(hh + 1) * HEAD_DIM].astype(BF16)


def _nsa_proj(x, g, w, tables, seq, *, tm):
    m, d = x.shape
    n = w.shape[1]
    heads = NSA_PROJ_TN // HEAD_DIM
    tpb = seq // tm
    tab_spec = pl.BlockSpec((tm, HEAD_DIM), lambda i, j: (i % tpb, 0))
    return pl.pallas_call(
        _nsa_proj_body,
        grid=(m // tm, n // NSA_PROJ_TN),
        in_specs=[
            pl.BlockSpec((tm, d), lambda i, j: (i, 0)),
            pl.BlockSpec((1, d), lambda i, j: (0, 0)),
            pl.BlockSpec((d, NSA_PROJ_TN), lambda i, j: (0, j)),
            tab_spec, tab_spec, tab_spec,
        ],
        out_specs=pl.BlockSpec((heads, tm, HEAD_DIM), lambda i, j: (j, i, 0)),
        out_shape=jax.ShapeDtypeStruct((n // HEAD_DIM, m, HEAD_DIM), BF16),
        scratch_shapes=[pltpu.VMEM((tm, d), BF16)],
        compiler_params=_params(("parallel", "arbitrary")),
        name="nsa_proj",
    )(x, g, w, *tables)


CMP_CHUNK = LANES


def _compress_body(kr_ref, pe_ref, w1_ref, w2_ref, c_ref, sa_ref, sb_ref, o_ref):
    kr = kr_ref[0].astype(F32)
    rows = kr.shape[0]
    a = _dot((kr + pe_ref[0, 0]).astype(BF16), w1_ref[0, 0])
    b = _dot((kr + pe_ref[0, 1]).astype(BF16), w1_ref[0, 1])
    hid = jax.nn.gelu(a + pltpu.roll(b, rows - 1, 0))
    out = _dot(hid.astype(BF16), w2_ref[0])
    is_k = pl.program_id(0) == 0

    @pl.when(is_k)
    def _():
        roped = _rope(out, c_ref[...], sa_ref[...], sb_ref[...]).astype(BF16)
        for j in range(rows // CMP_CHUNK):
            o_ref[0, 0, 0, j] = roped[j * CMP_CHUNK:(j + 1) * CMP_CHUNK]

    @pl.when(jnp.logical_not(is_k))
    def _():
        for j in range(rows // CMP_CHUNK):
            o_ref[0, 0, 0, j] = out[j * CMP_CHUNK:(j + 1) * CMP_CHUNK].T.astype(BF16)


def _compress(kr, pe, w1, w2, tables, batch, seq):
    rows = seq // CMP_STRIDE
    kdim = CMP_STRIDE * HEAD_DIM
    hid = w1.shape[-1]
    tab_spec = pl.BlockSpec((rows, HEAD_DIM), lambda w, b, g: (0, 0))
    chunks = rows // CMP_CHUNK
    return pl.pallas_call(
        _compress_body,
        grid=(2, batch, NSA_KV_GROUPS),
        in_specs=[
            pl.BlockSpec((1, rows, kdim), lambda w, b, g: (NSA_KV_GROUPS * w + g, b, 0)),
            pl.BlockSpec((1, 2, 1, kdim), lambda w, b, g: (w, 0, 0, 0)),
            pl.BlockSpec((1, 2, kdim, hid), lambda w, b, g: (w, 0, 0, 0)),
            pl.BlockSpec((1, hid, HEAD_DIM), lambda w, b, g: (w, 0, 0)),
            tab_spec, tab_spec, tab_spec,
        ],
        out_specs=pl.BlockSpec((1, 1, 1, chunks, CMP_CHUNK, HEAD_DIM), lambda w, b, g: (w, b, g, 0, 0, 0)),
        out_shape=jax.ShapeDtypeStruct((2, batch, NSA_KV_GROUPS, chunks, CMP_CHUNK, HEAD_DIM), BF16),
        compiler_params=_params(("parallel", "parallel", "parallel")),
        name="nsa_compress",
    )(kr, pe, w1, w2, *tables)


def _attn_body(q_ref, ks_ref, vs_ref, kw_ref, vw_ref, kc_ref, vc_ref, gate_ref, e_ref,
               o_ref, qaug_ref, sc_ref, ps_ref, sa_ref, sb_ref, sd_ref, *, ns, tk):
    qi = pl.program_id(2)
    t0 = qi * Q_BLOCK
    cols = NSA_HPG * Q_BLOCK
    halves = qaug_ref.shape[0]
    for n in range(NSA_HPG):
        qt = q_ref[n].astype(F32).T.astype(BF16)
        for h in range(halves):
            qaug_ref[h, 0:HEAD_DIM, n * Q_BLOCK:(n + 1) * Q_BLOCK] = qt
    q_t = qaug_ref[0, 0:HEAD_DIM, :]
    tq = t0 + (lax.broadcasted_iota(jnp.int32, (1, cols), 1) & (Q_BLOCK - 1))

    sub = min(2, sc_ref.shape[0] // CMP_CHUNK)
    step = sub * CMP_CHUNK
    nch_total = sc_ref.shape[0] // step
    nch = jnp.minimum((t0 + Q_BLOCK - CMP_LEN) // (CMP_STRIDE * step) + 1, nch_total)
    crow = lax.broadcasted_iota(jnp.int32, (step, cols), 0)

    def cmp_scores(j, m):
        c0 = pl.multiple_of(j * step, step)
        kc = jnp.concatenate([kc_ref[0, 0, 0, sub * j + i] for i in range(sub)], axis=0)
        s = _dot(kc, q_t)
        s = jnp.where((c0 + crow) * CMP_STRIDE + (CMP_LEN - 1) <= tq, s, NEG_INF)
        sc_ref[pl.ds(c0, step), :] = s
        return jnp.maximum(m, jnp.max(s, axis=0, keepdims=True))

    m_c = lax.fori_loop(0, nch, cmp_scores, jnp.full((1, cols), NEG_INF, F32))

    def cmp_probs(j, carry):
        l, acc = carry
        c0 = pl.multiple_of(j * step, step)
        p = jnp.exp2(sc_ref[pl.ds(c0, step), :] - m_c)
        sc_ref[pl.ds(c0, step), :] = p
        vc_t = jnp.concatenate([vc_ref[0, 0, 0, sub * j + i] for i in range(sub)], axis=1)
        return l + jnp.sum(p, axis=0, keepdims=True), acc + _dot(vc_t, p.astype(BF16))

    l_c, acc_c = lax.fori_loop(0, nch, cmp_probs,
                               (jnp.zeros((1, cols), F32), jnp.zeros((HEAD_DIM, cols), F32)))
    inv_c = jnp.where(tq >= CMP_LEN - 1, 1.0 / l_c, 0.0)
    o_c = acc_c * inv_c

    def cmp_headsum(j, _):
        c0 = pl.multiple_of(j * step, step)
        p = sc_ref[pl.ds(c0, step), :] * inv_c
        ps = p[:, 0:Q_BLOCK]
        for n in range(1, NSA_HPG):
            ps = ps + p[:, n * Q_BLOCK:(n + 1) * Q_BLOCK]
        ps_ref[pl.ds(c0, step), :] = ps
        return 0

    lax.fori_loop(0, nch, cmp_headsum, 0)

    def cmp_zero(j, _):
        ps_ref[pl.ds(pl.multiple_of(j * step, step), step), :] = jnp.zeros((step, Q_BLOCK), F32)
        return 0

    lax.fori_loop(nch, nch_total, cmp_zero, 0)

    n_win = WINDOW // Q_BLOCK + 1
    win_keys = n_win * Q_BLOCK
    w0 = pl.multiple_of(jnp.maximum(t0 - WINDOW, 0), Q_BLOCK)
    rel = (lax.broadcasted_iota(jnp.int32, (win_keys, cols), 0) + (w0 - t0)
           - (lax.broadcasted_iota(jnp.int32, (win_keys, cols), 1) & (Q_BLOCK - 1)))
    s_w = jnp.where((rel <= 0) & (rel > -WINDOW), _dot(kw_ref[0, pl.ds(w0, win_keys), :], q_t), NEG_INF)
    p_w = jnp.exp2(s_w - jnp.max(s_w, axis=0, keepdims=True))
    vw_t = jnp.concatenate([vw_ref[0, w0 // Q_BLOCK + w] for w in range(n_win)], axis=1)
    o_w = _dot(vw_t, p_w.astype(BF16)) * (1.0 / jnp.sum(p_w, axis=0, keepdims=True))

    ratio = SEL_BLOCK // CMP_STRIDE
    r = [ps_ref[pl.ds(j, ns, stride=ratio), :] for j in range(ratio)]
    prev = pltpu.roll(r[ratio - 1], 1, 0)
    prev = jnp.where(lax.broadcasted_iota(jnp.int32, (ns, Q_BLOCK), 0) == 0, 0.0, prev)
    imp = 2.0 * (r[0] + r[1] + r[2]) + r[3] + prev
    nsp = halves * SEL_BIAS_BLOCKS
    if nsp > ns:
        imp = jnp.concatenate([imp, jnp.zeros((nsp - ns, Q_BLOCK), F32)], axis=0)

    blk = lax.broadcasted_iota(jnp.int32, (nsp, Q_BLOCK), 0)
    blk_f = blk.astype(F32)
    cur = (t0 + lax.broadcasted_iota(jnp.int32, (nsp, Q_BLOCK), 1)) // SEL_BLOCK
    valid = blk <= cur
    forced = (blk == 0) | (blk == cur) | (blk == cur - 1)
    n_forced = 3
    score = jnp.where(valid & jnp.logical_not(forced), imp, NEG_INF)
    picked = forced
    for _ in range(min(SEL_TOPK, ns) - n_forced):
        top = jnp.max(score, axis=0, keepdims=True)
        first = jnp.min(jnp.where(score == top, blk_f, float(nsp)), axis=0, keepdims=True)
        hit = blk_f == first
        picked = picked | hit
        score = jnp.where(hit, -jnp.inf, score)
    bias = jnp.where(picked & valid, 0.0, NEG_INF).astype(BF16)

    for h in range(halves):
        bh = bias[h * SEL_BIAS_BLOCKS:(h + 1) * SEL_BIAS_BLOCKS]
        for n in range(NSA_HPG):
            qaug_ref[h, HEAD_DIM:HEAD_DIM + SEL_BIAS_BLOCKS, n * Q_BLOCK:(n + 1) * Q_BLOCK] = bh

    init = (jnp.full((1, cols), NEG_INF, F32), jnp.zeros((1, cols), F32),
            jnp.zeros((HEAD_DIM, cols), F32))

    def sel_scores(j, s_ref, diagonal=False):
        k0 = pl.multiple_of(j * tk, tk)
        e0 = pl.multiple_of(k0 % SEL_BIAS_KEYS, tk)
        kaug = jnp.concatenate([ks_ref[0, pl.ds(k0, tk), :], e_ref[pl.ds(e0, tk), :]], axis=1)
        s = _dot(kaug, qaug_ref[k0 // SEL_BIAS_KEYS])
        if diagonal:
            lk = lax.broadcasted_iota(jnp.int32, (tk, cols), 0)
            lq = lax.broadcasted_iota(jnp.int32, (tk, cols), 1) & (Q_BLOCK - 1)
            s = jnp.where(lk - lq <= t0 - k0, s, NEG_INF)
        s_ref[...] = s
        return jnp.max(s, axis=0, keepdims=True)

    def sel_update(s_ref, s_max, j, carry):
        m, l, acc = carry
        m_new = jnp.maximum(m, s_max)
        alpha = jnp.exp2(m - m_new)
        p = jnp.exp2(s_ref[...] - m_new)
        l = alpha * l + jnp.sum(p, axis=0, keepdims=True)
        acc = alpha * acc + _dot(vs_ref[0, j], p.astype(BF16))
        return m_new, l, acc

    n_full = t0 // tk
    last_full = jnp.maximum(n_full - 1, 0)
    max_d = sel_scores(n_full, sd_ref, diagonal=True)
    max_a = sel_scores(0, sa_ref)

    def sel_pair(j, carry):
        max_a, state = carry
        max_b = sel_scores(j + 1, sb_ref)
        state = sel_update(sa_ref, max_a, j, state)
        max_a = sel_scores(jnp.minimum(j + 2, last_full), sa_ref)
        return max_a, sel_update(sb_ref, max_b, j + 1, state)

    unroll = 4
    carry = lax.fori_loop(0, n_full // unroll,
                          lambda jj, c: sel_pair(unroll * jj + 2, sel_pair(unroll * jj, c)), (max_a, init))
    rest = n_full % unroll
    carry = lax.cond(rest >= 2, lambda c: sel_pair(n_full - rest, c), lambda c: c, carry)
    max_a, state = carry
    state = lax.cond(rest % 2 == 1, lambda c: sel_update(sa_ref, max_a, n_full - 1, c), lambda c: c, state)
    _, l_s, acc_s = sel_update(sd_ref, max_d, n_full, state)
    o_s = acc_s * (1.0 / l_s)

    g_t = gate_ref[...].T
    for n in range(NSA_HPG):
        c = slice(n * Q_BLOCK, (n + 1) * Q_BLOCK)
        o = (g_t[n:n + 1] * o_c[:, c] + g_t[NSA_HPG + n:NSA_HPG + n + 1] * o_s[:, c]
             + g_t[2 * NSA_HPG + n:2 * NSA_HPG + n + 1] * o_w[:, c])
        o_ref[:, n * HEAD_DIM:(n + 1) * HEAD_DIM] = o.T.astype(BF16)


def _nsa_attention(hm, vs_t, vw_t, cmp, gates, onehot, batch, seq, *, tk):
    nq = seq // Q_BLOCK
    ns = seq // SEL_BLOCK
    ncp = seq // CMP_STRIDE
    halves = -(-ns // SEL_BIAS_BLOCKS)
    m = batch * seq
    once = pl.Buffered(1)

    def slab(first):
        return pl.BlockSpec((1, seq, HEAD_DIM), lambda b, g, qi: (first + g, b, 0), pipeline_mode=once)

    def slab_t(arr):
        return pl.BlockSpec((1, arr.shape[1] // batch) + arr.shape[2:], lambda b, g, qi: (g, b, 0, 0),
                            pipeline_mode=once)

    def cmp_spec(which):
        return pl.BlockSpec((1, 1, 1, ncp // CMP_CHUNK, CMP_CHUNK, HEAD_DIM),
                            lambda b, g, qi: (which, b, g, 0, 0, 0))

    return pl.pallas_call(
        functools.partial(_attn_body, ns=ns, tk=tk),
        grid=(batch, NSA_KV_GROUPS, nq),
        in_specs=[
            pl.BlockSpec((NSA_HPG, Q_BLOCK, HEAD_DIM), lambda b, g, qi: (g, b * nq + qi, 0)),
            slab(HM_KS), slab_t(vs_t), slab(HM_KW), slab_t(vw_t),
            cmp_spec(0), cmp_spec(1),
            pl.BlockSpec((Q_BLOCK, LANES), lambda b, g, qi: (b * nq + qi, g)),
            pl.BlockSpec(onehot.shape, lambda b, g, qi: (0, 0), pipeline_mode=once),
        ],
        out_specs=pl.BlockSpec((Q_BLOCK, NSA_HPG * HEAD_DIM), lambda b, g, qi: (b * nq + qi, g)),
        out_shape=jax.ShapeDtypeStruct((m, NSA_HEADS * HEAD_DIM), BF16),
        scratch_shapes=[
            pltpu.VMEM((halves, 2 * HEAD_DIM, NSA_HPG * Q_BLOCK), BF16),
            pltpu.VMEM((ncp, NSA_HPG * Q_BLOCK), F32),
            pltpu.VMEM((ncp, Q_BLOCK), F32),
            pltpu.VMEM((tk, NSA_HPG * Q_BLOCK), F32),
            pltpu.VMEM((tk, NSA_HPG * Q_BLOCK), F32),
            pltpu.VMEM((tk, NSA_HPG * Q_BLOCK), F32),
        ],
        compiler_params=_params(("parallel", "parallel", "arbitrary")),
        name="nsa_attention",
    )(hm, hm, vs_t, hm, vw_t, cmp, cmp, gates, onehot)


def _sel_onehot(seq):
    keys = np.arange(min(seq, SEL_BIAS_KEYS))
    onehot = (keys[:, None] // SEL_BLOCK == np.arange(SEL_BIAS_BLOCKS)[None, :]).astype(np.float32)
    return jnp.asarray(onehot, BF16)


def _tiles_transposed(slabs, tile):
    h, m, d = slabs.shape
    return slabs.reshape(h, m // tile, tile, d).transpose(0, 1, 3, 2)


def _gate_weight(w_gl):
    d = w_gl.shape[0]
    w = w_gl.reshape(d, 3, NSA_KV_GROUPS, NSA_HPG).transpose(0, 2, 1, 3).reshape(d, NSA_KV_GROUPS, 3 * NSA_HPG)
    w = jnp.pad(w, ((0, 0), (0, 0), (0, LANES - 3 * NSA_HPG)))
    return w.reshape(d, NSA_KV_GROUPS * LANES)


def _nsa_layer(x, g_mix, w_in, kc_pe, kc_w1, kc_w2, vc_pe, vc_w1, vc_w2, w_out, batch, seq, *, tk=1024):
    assert SEL_BLOCK == 4 * CMP_STRIDE and CMP_LEN == 2 * CMP_STRIDE and seq % max(tk, CMP_STRIDE * CMP_CHUNK) == 0
    n_main = NSA_HEADS * HEAD_DIM + 6 * NSA_KV_GROUPS * HEAD_DIM
    pos_tables = _rope_tables(jnp.arange(seq))
    hm = _nsa_proj(x, g_mix, w_in[:, :n_main].astype(BF16), pos_tables, seq, tm=1024)
    gates = _norm_mm(x, g_mix, _gate_weight(w_in[:, n_main:]).astype(BF16), act="sigmoid",
                     out_dtype=F32, tm=1024, tn=NSA_KV_GROUPS * LANES)
    rows = seq // CMP_STRIDE
    cmp_tables = _rope_tables(jnp.arange(rows) * CMP_STRIDE + CMP_LEN - 1)
    half = CMP_LEN // 2
    pe = jnp.stack([kc_pe, vc_pe]).reshape(2, 2, 1, half * HEAD_DIM)
    w1 = jnp.stack([kc_w1, vc_w1]).astype(BF16)
    w1 = w1.reshape(2, 2, half * HEAD_DIM, w1.shape[-1])
    w2 = jnp.stack([kc_w2, vc_w2]).astype(BF16)
    kr = hm[HM_KC:HM_KS].reshape(2 * NSA_KV_GROUPS, batch * rows, CMP_STRIDE * HEAD_DIM)
    cmp = _compress(kr, pe, w1, w2, cmp_tables, batch, seq)
    vs_t = _tiles_transposed(hm[HM_VS:HM_KW], tk)
    vw_t = _tiles_transposed(hm[HM_VW:], Q_BLOCK)
    att = _nsa_attention(hm, vs_t, vw_t, cmp, gates, _sel_onehot(seq), batch, seq, tk=tk)
    return _mm_residual(att, w_out.astype(BF16), x, tm=1024, tn=512)


def kernel(x, p, norm_mix, norm_ffn, norm_ple, ffn_up, ffn_down, ple_proj, ple_gate, gm_in, gm_ln_g, gm_ln_b, gm_ws, gm_bs, gm_out, nsa_in, nsa_kc_pe, nsa_kc_w1, nsa_kc_w2, nsa_vc_pe, nsa_vc_w1, nsa_vc_w2, nsa_out, final_norm):
    batch, seq, d = x.shape
    m = batch * seq
    depth = p.shape[0]
    xf = x.reshape(m, d)
    row = lambda v: v.reshape(1, -1)
    for i in range(depth):
        j = i // 2
        if i % 2 == 0:
            z = _norm_mm(xf, row(norm_mix[i]), gm_in[j].astype(BF16), act="gelu", out_dtype=BF16,
                         tm=1024, tn=512)
            xf = _gmlp_gate_out(z, xf, row(gm_ln_g[j]), row(gm_ln_b[j]), gm_ws[j], gm_bs[j].T,
                                gm_out[j].astype(BF16), tm=512)
        else:
            xf = _nsa_layer(xf, row(norm_mix[i]), nsa_in[j], nsa_kc_pe[j], nsa_kc_w1[j], nsa_kc_w2[j],
                            nsa_vc_pe[j], nsa_vc_w1[j], nsa_vc_w2[j], nsa_out[j], batch, seq)
        xf = _ffn(xf, row(norm_ffn[i]), ffn_up[i].astype(BF16), ffn_down[i].astype(BF16), tm=1024, tf=512)
        last = i == depth - 1
        xf = _ple(xf, p[i].reshape(m, -1), row(norm_ple[i]), ple_gate[i].astype(BF16),
                  ple_proj[i].astype(BF16), row(final_norm), final=last, tm=256)
    return xf.reshape(batch, seq, d)
```

```python
import functools

import numpy as np
import jax
import jax.numpy as jnp
from jax import lax
from jax.experimental import pallas as pl
from jax.experimental.pallas import tpu as pltpu

F32 = jnp.float32
BF16 = jnp.bfloat16

EPS = 1e-6
HEAD_DIM = 128
NSA_HEADS = 16
NSA_KV_GROUPS = 4
NSA_HPG = NSA_HEADS // NSA_KV_GROUPS
ROT_DIM = HEAD_DIM // 4
ROPE_THETA = 500000.0
CMP_LEN = 32
CMP_STRIDE = 16
SEL_BLOCK = 64
SEL_TOPK = 16
WINDOW = 512
Q_BLOCK = 128
GM_CHUNK = 128
NEG_INF = -1e30
LOG2_E = 1.4426950408889634

LANES = 128
SEL_BIAS_BLOCKS = LANES
SEL_BIAS_KEYS = SEL_BIAS_BLOCKS * SEL_BLOCK
VMEM_LIMIT = 56 * 1024 * 1024


def _params(sem):
    return pltpu.CompilerParams(dimension_semantics=sem, vmem_limit_bytes=VMEM_LIMIT)


def _rmsnorm(x, g):
    return x * lax.rsqrt(jnp.mean(x * x, axis=-1, keepdims=True) + EPS) * g


def _dot(a, b):
    return jnp.dot(a, b, preferred_element_type=F32)


def _rope(x, c, sa, sb):
    return x * c + pltpu.roll(x, LANES - ROT_DIM // 2, 1) * sa + pltpu.roll(x, ROT_DIM // 2, 1) * sb


def _rope_tables(pos):
    half = ROT_DIM // 2
    inv = jnp.power(jnp.float32(ROPE_THETA), -jnp.arange(half, dtype=F32) * 2.0 / ROT_DIM)
    ang = pos.astype(F32)[:, None] * inv[None, :]
    cos, sin = jnp.cos(ang), jnp.sin(ang)
    n = pos.shape[0]
    rest = HEAD_DIM - ROT_DIM
    c = jnp.concatenate([cos, cos, jnp.ones((n, rest), F32)], axis=1)
    sa = jnp.concatenate([-sin, jnp.zeros((n, half + rest), F32)], axis=1)
    sb = jnp.concatenate([jnp.zeros((n, half), F32), sin, jnp.zeros((n, rest), F32)], axis=1)
    return c, sa, sb


def _norm_mm_body(x_ref, g_ref, w_ref, o_ref, h_ref, *, act):
    @pl.when(pl.program_id(1) == 0)
    def _():
        h_ref[...] = _rmsnorm(x_ref[...], g_ref[...]).astype(BF16)

    acc = _dot(h_ref[...], w_ref[...])
    if act == "gelu":
        acc = jax.nn.gelu(acc)
    elif act == "sigmoid":
        acc = jax.nn.sigmoid(acc)
    o_ref[...] = acc.astype(o_ref.dtype)


def _norm_mm(x, g, w, *, act, out_dtype, tm, tn):
    m, d = x.shape
    n = w.shape[1]
    return pl.pallas_call(
        functools.partial(_norm_mm_body, act=act),
        grid=(m // tm, n // tn),
        in_specs=[
            pl.BlockSpec((tm, d), lambda i, j: (i, 0)),
            pl.BlockSpec((1, d), lambda i, j: (0, 0)),
            pl.BlockSpec((d, tn), lambda i, j: (0, j)),
        ],
        out_specs=pl.BlockSpec((tm, tn), lambda i, j: (i, j)),
        out_shape=jax.ShapeDtypeStruct((m, n), out_dtype),
        scratch_shapes=[pltpu.VMEM((tm, d), BF16)],
        compiler_params=_params(("parallel", "arbitrary")),
        name="norm_mm_" + str(act),
    )(x, g, w)


def _gmlp_body(z_ref, x_ref, lg_ref, lb_ref, ws_ref, bs_ref, wo_ref, o_ref, y_ref, *, tm, width):
    groups = ws_ref.shape[0]
    gd = width // groups
    u = z_ref[:, :width]
    v = z_ref[:, width:].astype(F32)
    mu = jnp.mean(v, axis=-1, keepdims=True)
    var = jnp.mean(jnp.square(v - mu), axis=-1, keepdims=True)
    vn = ((v - mu) * lax.rsqrt(var + EPS) * lg_ref[...] + lb_ref[...]).astype(BF16)
    r = lax.broadcasted_iota(jnp.int32, (GM_CHUNK, GM_CHUNK), 0)
    c = lax.broadcasted_iota(jnp.int32, (GM_CHUNK, GM_CHUNK), 1)
    causal = c <= r
    for g in range(groups):
        wg = jnp.where(causal, ws_ref[g], 0.0).astype(BF16)
        bg = bs_ref[:, g:g + 1]
        for ch in range(tm // GM_CHUNK):
            rows = slice(ch * GM_CHUNK, (ch + 1) * GM_CHUNK)
            cols = slice(g * gd, (g + 1) * gd)
            sv = _dot(wg, vn[rows, cols]) + bg
            y_ref[rows, cols] = (u[rows, cols].astype(F32) * sv).astype(BF16)
    o_ref[...] = x_ref[...] + _dot(y_ref[...], wo_ref[...])


def _gmlp_gate_out(z, x, ln_g, ln_b, ws, bs, w_out, *, tm):
    m, d = x.shape
    width = z.shape[1] // 2
    groups = ws.shape[0]
    return pl.pallas_call(
        functools.partial(_gmlp_body, tm=tm, width=width),
        grid=(m // tm,),
        in_specs=[
            pl.BlockSpec((tm, 2 * width), lambda i: (i, 0)),
            pl.BlockSpec((tm, d), lambda i: (i, 0)),
            pl.BlockSpec((1, width), lambda i: (0, 0)),
            pl.BlockSpec((1, width), lambda i: (0, 0)),
            pl.BlockSpec((groups, GM_CHUNK, GM_CHUNK), lambda i: (0, 0, 0)),
            pl.BlockSpec((GM_CHUNK, groups), lambda i: (0, 0)),
            pl.BlockSpec((width, d), lambda i: (0, 0)),
        ],
        out_specs=pl.BlockSpec((tm, d), lambda i: (i, 0)),
        out_shape=jax.ShapeDtypeStruct((m, d), F32),
        scratch_shapes=[pltpu.VMEM((tm, width), BF16)],
        compiler_params=_params(("parallel",)),
        name="gmlp_gate_out",
    )(z, x, ln_g, ln_b, ws, bs, w_out)


def _ffn_body(x_ref, g_ref, wu_ref, wd_ref, o_ref, h_ref):
    f = pl.program_id(1)

    @pl.when(f == 0)
    def _():
        x = x_ref[...]
        h_ref[...] = _rmsnorm(x, g_ref[...]).astype(BF16)
        o_ref[...] = x

    a = jnp.square(jnp.maximum(_dot(h_ref[...], wu_ref[...]), 0.0)).astype(BF16)
    o_ref[...] += _dot(a, wd_ref[...])


def _ffn(x, g, w_up, w_down, *, tm, tf):
    m, d = x.shape
    ff = w_up.shape[1]
    return pl.pallas_call(
        _ffn_body,
        grid=(m // tm, ff // tf),
        in_specs=[
            pl.BlockSpec((tm, d), lambda i, f: (i, 0)),
            pl.BlockSpec((1, d), lambda i, f: (0, 0)),
            pl.BlockSpec((d, tf), lambda i, f: (0, f)),
            pl.BlockSpec((tf, d), lambda i, f: (f, 0)),
        ],
        out_specs=pl.BlockSpec((tm, d), lambda i, f: (i, 0)),
        out_shape=jax.ShapeDtypeStruct((m, d), F32),
        scratch_shapes=[pltpu.VMEM((tm, d), BF16)],
        compiler_params=_params(("parallel", "arbitrary")),
        name="ffn",
    )(x, g, w_up, w_down)


def _ple_body(x_ref, p_ref, g_ref, wg_ref, wp_ref, fg_ref, o_ref, *, final):
    x = x_ref[...]
    h = _rmsnorm(x, g_ref[...]).astype(BF16)
    gate = jax.nn.sigmoid(_dot(h, wg_ref[...]))
    y = x + gate * _dot(p_ref[...].astype(BF16), wp_ref[...])
    if final:
        y = _rmsnorm(y, fg_ref[...])
    o_ref[...] = y


def _ple(x, p, g, w_gate, w_proj, final_g, *, final, tm):
    m, d = x.shape
    pd = p.shape[1]
    return pl.pallas_call(
        functools.partial(_ple_body, final=final),
        grid=(m // tm,),
        in_specs=[
            pl.BlockSpec((tm, d), lambda i: (i, 0)),
            pl.BlockSpec((tm, pd), lambda i: (i, 0)),
            pl.BlockSpec((1, d), lambda i: (0, 0)),
            pl.BlockSpec((d, d), lambda i: (0, 0)),
            pl.BlockSpec((pd, d), lambda i: (0, 0)),
            pl.BlockSpec((1, d), lambda i: (0, 0)),
        ],
        out_specs=pl.BlockSpec((tm, d), lambda i: (i, 0)),
        out_shape=jax.ShapeDtypeStruct((m, d), F32),
        compiler_params=_params(("parallel",)),
        name="ple",
    )(x, p, g, w_gate, w_proj, final_g)


def _mm_res_body(a_ref, w_ref, x_ref, o_ref):
    o_ref[...] = x_ref[...] + _dot(a_ref[...], w_ref[...])


def _mm_residual(a, w, x, *, tm, tn):
    m, k = a.shape
    n = w.shape[1]
    return pl.pallas_call(
        _mm_res_body,
        grid=(m // tm, n // tn),
        in_specs=[
            pl.BlockSpec((tm, k), lambda i, j: (i, 0)),
            pl.BlockSpec((k, tn), lambda i, j: (0, j)),
            pl.BlockSpec((tm, tn), lambda i, j: (i, j)),
        ],
        out_specs=pl.BlockSpec((tm, tn), lambda i, j: (i, j)),
        out_shape=jax.ShapeDtypeStruct((m, n), F32),
        compiler_params=_params(("parallel", "arbitrary")),
        name="mm_residual",
    )(a, w, x)


NSA_PROJ_TN = 4 * HEAD_DIM
NSA_Q_TILES = NSA_HEADS * HEAD_DIM // NSA_PROJ_TN
NSA_KS_TILE = NSA_Q_TILES + 2
NSA_KW_TILE = NSA_Q_TILES + 4
HM_KC, HM_VC, HM_KS, HM_VS, HM_KW, HM_VW = (NSA_HEADS + NSA_KV_GROUPS * i for i in range(6))


def _nsa_proj_body(x_ref, g_ref, w_ref, c_ref, sa_ref, sb_ref, o_ref, h_ref):
    j = pl.program_id(1)

    @pl.when(j == 0)
    def _():
        h_ref[...] = _rmsnorm(x_ref[...], g_ref[...]).astype(BF16)

    acc = _dot(h_ref[...], w_ref[...])
    heads = NSA_PROJ_TN // HEAD_DIM
    is_q = j < NSA_Q_TILES
    is_rope = is_q | (j == NSA_KS_TILE) | (j == NSA_KW_TILE)

    @pl.when(is_rope)
    def _():
        scale = jnp.where(is_q, HEAD_DIM ** -0.5 * LOG2_E, 1.0).astype(F32)
        c, sa, sb = c_ref[...], sa_ref[...], sb_ref[...]
        for hh in range(heads):
            seg = acc[:, hh * HEAD_DIM:(hh + 1) * HEAD_DIM]
            o_ref[hh] = (_rope(seg, c, sa, sb) * scale).astype(BF16)

    @pl.when(jnp.logical_not(is_rope))
    def _():
        for hh in range(heads):
            o_ref[hh] = acc[:, hh * HEAD_DIM:(hh + 1) * HEAD_DIM].astype(BF16)


def _nsa_proj(x, g, w, tables, seq, *, tm):
    m, d = x.shape
    n = w.shape[1]
    heads = NSA_PROJ_TN // HEAD_DIM
    tpb = seq // tm
    tab_spec = pl.BlockSpec((tm, HEAD_DIM), lambda i, j: (i % tpb, 0))
    return pl.pallas_call(
        _nsa_proj_body,
        grid=(m // tm, n // NSA_PROJ_TN),
        in_specs=[
            pl.BlockSpec((tm, d), lambda i, j: (i, 0)),
            pl.BlockSpec((1, d), lambda i, j: (0, 0)),
            pl.BlockSpec((d, NSA_PROJ_TN), lambda i, j: (0, j)),
            tab_spec, tab_spec, tab_spec,
        ],
        out_specs=pl.BlockSpec((heads, tm, HEAD_DIM), lambda i, j: (j, i, 0)),
        out_shape=jax.ShapeDtypeStruct((n // HEAD_DIM, m, HEAD_DIM), BF16),
        scratch_shapes=[pltpu.VMEM((tm, d), BF16)],
        compiler_params=_params(("parallel", "arbitrary")),
        name="nsa_proj",
    )(x, g, w, *tables)


CMP_CHUNK = LANES


def _compress_body(kr_ref, pe_ref, w1_ref, w2_ref, c_ref, sa_ref, sb_ref, o_ref):
    kr = kr_ref[0].astype(F32)
    rows = kr.shape[0]
    a = _dot((kr + pe_ref[0, 0]).astype(BF16), w1_ref[0, 0])
    b = _dot((kr + pe_ref[0, 1]).astype(BF16), w1_ref[0, 1])
    hid = jax.nn.gelu(a + pltpu.roll(b, rows - 1, 0))
    out = _dot(hid.astype(BF16), w2_ref[0])
    is_k = pl.program_id(0) == 0

    @pl.when(is_k)
    def _():
        roped = _rope(out, c_ref[...], sa_ref[...], sb_ref[...]).astype(BF16)
        for j in range(rows // CMP_CHUNK):
            o_ref[0, 0, 0, j] = roped[j * CMP_CHUNK:(j + 1) * CMP_CHUNK]

    @pl.when(jnp.logical_not(is_k))
    def _():
        for j in range(rows // CMP_CHUNK):
            o_ref[0, 0, 0, j] = out[j * CMP_CHUNK:(j + 1) * CMP_CHUNK].T.astype(BF16)


def _compress(kr, pe, w1, w2, tables, batch, seq):
    rows = seq // CMP_STRIDE
    kdim = CMP_STRIDE * HEAD_DIM
    hid = w1.shape[-1]
    tab_spec = pl.BlockSpec((rows, HEAD_DIM), lambda w, b, g: (0, 0))
    chunks = rows // CMP_CHUNK
    return pl.pallas_call(
        _compress_body,
        grid=(2, batch, NSA_KV_GROUPS),
        in_specs=[
            pl.BlockSpec((1, rows, kdim), lambda w, b, g: (NSA_KV_GROUPS * w + g, b, 0)),
            pl.BlockSpec((1, 2, 1, kdim), lambda w, b, g: (w, 0, 0, 0)),
            pl.BlockSpec((1, 2, kdim, hid), lambda w, b, g: (w, 0, 0, 0)),
            pl.BlockSpec((1, hid, HEAD_DIM), lambda w, b, g: (w, 0, 0)),
            tab_spec, tab_spec, tab_spec,
        ],
        out_specs=pl.BlockSpec((1, 1, 1, chunks, CMP_CHUNK, HEAD_DIM), lambda w, b, g: (w, b, g, 0, 0, 0)),
        out_shape=jax.ShapeDtypeStruct((2, batch, NSA_KV_GROUPS, chunks, CMP_CHUNK, HEAD_DIM), BF16),
        compiler_params=_params(("parallel", "parallel", "parallel")),
        name="nsa_compress",
    )(kr, pe, w1, w2, *tables)


def _attn_body(q_ref, ks_ref, vs_ref, kw_ref, vw_ref, kc_ref, vc_ref, gate_ref, e_ref,
               o_ref, qaug_ref, ps_ref, sa_ref, sb_ref, sd_ref, *, ns, tk):
    qi = pl.program_id(2)
    t0 = qi * Q_BLOCK
    cols = NSA_HPG * Q_BLOCK
    halves = qaug_ref.shape[0]
    for n in range(NSA_HPG):
        qt = q_ref[n].astype(F32).T.astype(BF16)
        for h in range(halves):
            qaug_ref[h, 0:HEAD_DIM, n * Q_BLOCK:(n + 1) * Q_BLOCK] = qt
    q_t = qaug_ref[0, 0:HEAD_DIM, :]
    tq = t0 + (lax.broadcasted_iota(jnp.int32, (1, cols), 1) & (Q_BLOCK - 1))

    ncp = ps_ref.shape[0]
    step = min(2 * CMP_CHUNK, ncp)

    def compressed(rows):
        chunks = range(rows // CMP_CHUNK)
        s = _dot(jnp.concatenate([kc_ref[0, 0, 0, i] for i in chunks], axis=0), q_t)
        cend = lax.broadcasted_iota(jnp.int32, (rows, cols), 0) * CMP_STRIDE + (CMP_LEN - 1)
        s = jnp.where(cend <= tq, s, NEG_INF)
        p = jnp.exp2(s - jnp.max(s, axis=0, keepdims=True))
        acc = _dot(jnp.concatenate([vc_ref[0, 0, 0, i] for i in chunks], axis=1), p.astype(BF16))
        inv = jnp.where(tq >= CMP_LEN - 1, 1.0 / jnp.sum(p, axis=0, keepdims=True), 0.0)
        p = p * inv
        ps = p[:, 0:Q_BLOCK]
        for n in range(1, NSA_HPG):
            ps = ps + p[:, n * Q_BLOCK:(n + 1) * Q_BLOCK]
        ps_ref[0:rows, :] = ps
        if rows < ncp:
            ps_ref[rows:ncp, :] = jnp.zeros((ncp - rows, Q_BLOCK), F32)
        return acc * inv

    variants = [functools.partial(compressed, rows) for rows in range(step, ncp + 1, step)]
    reach = jnp.minimum((t0 + Q_BLOCK - CMP_LEN) // (CMP_STRIDE * step), len(variants) - 1)
    o_c = lax.switch(reach, variants) if len(variants) > 1 else variants[0]()

    ratio = SEL_BLOCK // CMP_STRIDE
    r = [ps_ref[pl.ds(j, ns, stride=ratio), :] for j in range(ratio)]
    prev = pltpu.roll(r[ratio - 1], 1, 0)
    prev = jnp.where(lax.broadcasted_iota(jnp.int32, (ns, Q_BLOCK), 0) == 0, 0.0, prev)
    imp = 2.0 * (r[0] + r[1] + r[2]) + r[3] + prev
    nsp = halves * SEL_BIAS_BLOCKS
    if nsp > ns:
        imp = jnp.concatenate([imp, jnp.zeros((nsp - ns, Q_BLOCK), F32)], axis=0)

    blk = lax.broadcasted_iota(jnp.int32, (nsp, Q_BLOCK), 0)
    blk_f = blk.astype(F32)
    cur = (t0 + lax.broadcasted_iota(jnp.int32, (nsp, Q_BLOCK), 1)) // SEL_BLOCK
    valid = blk <= cur
    forced = (blk == 0) | (blk == cur) | (blk == cur - 1)
    n_forced = 3
    rounds = min(SEL_TOPK, ns) - n_forced
    free = valid & jnp.logical_not(forced)
    score0 = jnp.where(free, imp, NEG_INF)

    score = score0
    for _ in range(rounds):
        score = jnp.where(score == jnp.max(score, axis=0, keepdims=True), -jnp.inf, score)
    removed = score == -jnp.inf
    n_removed = jnp.sum(jnp.where(removed & free, 1.0, 0.0), axis=0, keepdims=True)
    cur_row = (t0 + lax.broadcasted_iota(jnp.int32, (1, Q_BLOCK), 1)) // SEL_BLOCK
    n_free = jnp.clip(cur_row + 1 - n_forced, 0, rounds).astype(F32)
    n_tied = jnp.sum(jnp.where(n_removed == n_free, 0.0, 1.0))

    def tie_breaking_rounds():
        score, taken = score0, jnp.zeros((nsp, Q_BLOCK), F32)
        for _ in range(rounds):
            top = jnp.max(score, axis=0, keepdims=True)
            first = jnp.min(jnp.where(score == top, blk_f, float(nsp)), axis=0, keepdims=True)
            hit = blk_f == first
            taken = jnp.where(hit, 1.0, taken)
            score = jnp.where(hit, -jnp.inf, score)
        return taken

    taken = lax.cond(n_tied == 0.0, lambda: jnp.where(removed, 1.0, 0.0), tie_breaking_rounds)
    bias = jnp.where((forced | (taken > 0.0)) & valid, 0.0, NEG_INF).astype(BF16)

    for h in range(halves):
        bh = bias[h * SEL_BIAS_BLOCKS:(h + 1) * SEL_BIAS_BLOCKS]
        for n in range(NSA_HPG):
            qaug_ref[h, HEAD_DIM:HEAD_DIM + SEL_BIAS_BLOCKS, n * Q_BLOCK:(n + 1) * Q_BLOCK] = bh

    init = (jnp.full((1, cols), NEG_INF, F32), jnp.zeros((1, cols), F32),
            jnp.zeros((HEAD_DIM, cols), F32))

    n_win = WINDOW // Q_BLOCK + 1
    win_keys = n_win * Q_BLOCK
    w0 = pl.multiple_of(jnp.maximum(t0 - WINDOW, 0), Q_BLOCK)
    newest = tq - w0
    krow = lax.broadcasted_iota(jnp.int32, (win_keys, cols), 0)
    s_w = jnp.where(krow <= newest, _dot(kw_ref[0, pl.ds(w0, win_keys), :], q_t), NEG_INF)
    oldest = lax.broadcasted_iota(jnp.int32, (Q_BLOCK, cols), 0) > newest - WINDOW
    s_w = jnp.concatenate([jnp.where(oldest, s_w[:Q_BLOCK], NEG_INF), s_w[Q_BLOCK:]], axis=0)
    p_w = jnp.exp2(s_w - jnp.max(s_w, axis=0, keepdims=True))
    vw_t = jnp.concatenate([vw_ref[0, w0 // Q_BLOCK + w] for w in range(n_win)], axis=1)
    o_w = _dot(vw_t, p_w.astype(BF16)) * (1.0 / jnp.sum(p_w, axis=0, keepdims=True))

    def sel_scores(j, s_ref, diagonal=False):
        k0 = pl.multiple_of(j * tk, tk)
        e0 = pl.multiple_of(k0 % SEL_BIAS_KEYS, tk)
        kaug = jnp.concatenate([ks_ref[0, pl.ds(k0, tk), :], e_ref[pl.ds(e0, tk), :]], axis=1)
        s = _dot(kaug, qaug_ref[k0 // SEL_BIAS_KEYS])
        if diagonal:
            lk = lax.broadcasted_iota(jnp.int32, (tk, cols), 0)
            lq = lax.broadcasted_iota(jnp.int32, (tk, cols), 1) & (Q_BLOCK - 1)
            s = jnp.where(lk - lq <= t0 - k0, s, NEG_INF)
        s_ref[...] = s
        return jnp.max(s, axis=0, keepdims=True)

    def sel_update(s_ref, s_max, j, carry):
        m, l, acc = carry
        m_new = jnp.maximum(m, s_max)
        alpha = jnp.exp2(m - m_new)
        p = jnp.exp2(s_ref[...] - m_new)
        l = alpha * l + jnp.sum(p, axis=0, keepdims=True)
        acc = alpha * acc + _dot(vs_ref[0, j], p.astype(BF16))
        return m_new, l, acc

    n_full = t0 // tk
    last_full = jnp.maximum(n_full - 1, 0)
    max_d = sel_scores(n_full, sd_ref, diagonal=True)
    max_a = sel_scores(0, sa_ref)

    def sel_pair(j, carry):
        max_a, state = carry
        max_b = sel_scores(j + 1, sb_ref)
        state = sel_update(sa_ref, max_a, j, state)
        max_a = sel_scores(jnp.minimum(j + 2, last_full), sa_ref)
        return max_a, sel_update(sb_ref, max_b, j + 1, state)

    unroll = 4
    carry = lax.fori_loop(0, n_full // unroll,
                          lambda jj, c: sel_pair(unroll * jj + 2, sel_pair(unroll * jj, c)), (max_a, init))
    rest = n_full % unroll
    carry = lax.cond(rest >= 2, lambda c: sel_pair(n_full - rest, c), lambda c: c, carry)
    max_a, state = carry
    state = lax.cond(rest % 2 == 1, lambda c: sel_update(sa_ref, max_a, n_full - 1, c), lambda c: c, state)
    _, l_s, acc_s = sel_update(sd_ref, max_d, n_full, state)
    o_s = acc_s * (1.0 / l_s)

    g_t = gate_ref[...].T
    for n in range(NSA_HPG):
        c = slice(n * Q_BLOCK, (n + 1) * Q_BLOCK)
        o = (g_t[n:n + 1] * o_c[:, c] + g_t[NSA_HPG + n:NSA_HPG + n + 1] * o_s[:, c]
             + g_t[2 * NSA_HPG + n:2 * NSA_HPG + n + 1] * o_w[:, c])
        o_ref[:, n * HEAD_DIM:(n + 1) * HEAD_DIM] = o.T.astype(BF16)


def _nsa_attention(hm, vs_t, vw_t, cmp, gates, onehot, batch, seq, *, tk):
    nq = seq // Q_BLOCK
    ns = seq // SEL_BLOCK
    ncp = seq // CMP_STRIDE
    halves = -(-ns // SEL_BIAS_BLOCKS)
    m = batch * seq
    once = pl.Buffered(1)

    def slab(first):
        return pl.BlockSpec((1, seq, HEAD_DIM), lambda b, g, qi: (first + g, b, 0), pipeline_mode=once)

    def slab_t(arr):
        return pl.BlockSpec((1, arr.shape[1] // batch) + arr.shape[2:], lambda b, g, qi: (g, b, 0, 0),
                            pipeline_mode=once)

    def cmp_spec(which):
        return pl.BlockSpec((1, 1, 1, ncp // CMP_CHUNK, CMP_CHUNK, HEAD_DIM),
                            lambda b, g, qi: (which, b, g, 0, 0, 0))

    return pl.pallas_call(
        functools.partial(_attn_body, ns=ns, tk=tk),
        grid=(batch, NSA_KV_GROUPS, nq),
        in_specs=[
            pl.BlockSpec((NSA_HPG, Q_BLOCK, HEAD_DIM), lambda b, g, qi: (g, b * nq + qi, 0)),
            slab(HM_KS), slab_t(vs_t), slab(HM_KW), slab_t(vw_t),
            cmp_spec(0), cmp_spec(1),
            pl.BlockSpec((Q_BLOCK, LANES), lambda b, g, qi: (b * nq + qi, g)),
            pl.BlockSpec(onehot.shape, lambda b, g, qi: (0, 0), pipeline_mode=once),
        ],
        out_specs=pl.BlockSpec((Q_BLOCK, NSA_HPG * HEAD_DIM), lambda b, g, qi: (b * nq + qi, g)),
        out_shape=jax.ShapeDtypeStruct((m, NSA_HEADS * HEAD_DIM), BF16),
        scratch_shapes=[
            pltpu.VMEM((halves, 2 * HEAD_DIM, NSA_HPG * Q_BLOCK), BF16),
            pltpu.VMEM((ncp, Q_BLOCK), F32),
            pltpu.VMEM((tk, NSA_HPG * Q_BLOCK), F32),
            pltpu.VMEM((tk, NSA_HPG * Q_BLOCK), F32),
            pltpu.VMEM((tk, NSA_HPG * Q_BLOCK), F32),
        ],
        compiler_params=_params(("parallel", "parallel", "arbitrary")),
        name="nsa_attention",
    )(hm, hm, vs_t, hm, vw_t, cmp, cmp, gates, onehot)


def _sel_onehot(seq):
    keys = np.arange(min(seq, SEL_BIAS_KEYS))
    onehot = (keys[:, None] // SEL_BLOCK == np.arange(SEL_BIAS_BLOCKS)[None, :]).astype(np.float32)
    return jnp.asarray(onehot, BF16)


def _tiles_transposed(slabs, tile):
    h, m, d = slabs.shape
    return slabs.reshape(h, m // tile, tile, d).transpose(0, 1, 3, 2)


def _gate_weight(w_gl):
    d = w_gl.shape[0]
    w = w_gl.reshape(d, 3, NSA_KV_GROUPS, NSA_HPG).transpose(0, 2, 1, 3).reshape(d, NSA_KV_GROUPS, 3 * NSA_HPG)
    w = jnp.pad(w, ((0, 0), (0, 0), (0, LANES - 3 * NSA_HPG)))
    return w.reshape(d, NSA_KV_GROUPS * LANES)


def _nsa_layer(x, g_mix, w_in, kc_pe, kc_w1, kc_w2, vc_pe, vc_w1, vc_w2, w_out, batch, seq, *, tk=1024):
    assert SEL_BLOCK == 4 * CMP_STRIDE and CMP_LEN == 2 * CMP_STRIDE and seq % max(tk, CMP_STRIDE * CMP_CHUNK) == 0
    n_main = NSA_HEADS * HEAD_DIM + 6 * NSA_KV_GROUPS * HEAD_DIM
    pos_tables = _rope_tables(jnp.arange(seq))
    hm = _nsa_proj(x, g_mix, w_in[:, :n_main].astype(BF16), pos_tables, seq, tm=1024)
    gates = _norm_mm(x, g_mix, _gate_weight(w_in[:, n_main:]).astype(BF16), act="sigmoid",
                     out_dtype=F32, tm=1024, tn=NSA_KV_GROUPS * LANES)
    rows = seq // CMP_STRIDE
    cmp_tables = _rope_tables(jnp.arange(rows) * CMP_STRIDE + CMP_LEN - 1)
    half = CMP_LEN // 2
    pe = jnp.stack([kc_pe, vc_pe]).reshape(2, 2, 1, half * HEAD_DIM)
    w1 = jnp.stack([kc_w1, vc_w1]).astype(BF16)
    w1 = w1.reshape(2, 2, half * HEAD_DIM, w1.shape[-1])
    w2 = jnp.stack([kc_w2, vc_w2]).astype(BF16)
    kr = hm[HM_KC:HM_KS].reshape(2 * NSA_KV_GROUPS, batch * rows, CMP_STRIDE * HEAD_DIM)
    cmp = _compress(kr, pe, w1, w2, cmp_tables, batch, seq)
    vs_t = _tiles_transposed(hm[HM_VS:HM_KW], tk)
    vw_t = _tiles_transposed(hm[HM_VW:], Q_BLOCK)
    att = _nsa_attention(hm, vs_t, vw_t, cmp, gates, _sel_onehot(seq), batch, seq, tk=tk)
    return _mm_residual(att, w_out.astype(BF16), x, tm=1024, tn=512)


def kernel(x, p, norm_mix, norm_ffn, norm_ple, ffn_up, ffn_down, ple_proj, ple_gate, gm_in, gm_ln_g, gm_ln_b, gm_ws, gm_bs, gm_out, nsa_in, nsa_kc_pe, nsa_kc_w1, nsa_kc_w2, nsa_vc_pe, nsa_vc_w1, nsa_vc_w2, nsa_out, final_norm):
    batch, seq, d = x.shape
    m = batch * seq
    depth = p.shape[0]
    xf = x.reshape(m, d)
    row = lambda v: v.reshape(1, -1)
    for i in range(depth):
        j = i // 2
        if i % 2 == 0:
            z = _norm_mm(xf, row(norm_mix[i]), gm_in[j].astype(BF16), act="gelu", out_dtype=BF16,
                         tm=1024, tn=512)
            xf = _gmlp_gate_out(z, xf, row(gm_ln_g[j]), row(gm_ln_b[j]), gm_ws[j], gm_bs[j].T,
                                gm_out[j].astype(BF16), tm=512)
        else:
            xf = _nsa_layer(xf, row(norm_mix[i]), nsa_in[j], nsa_kc_pe[j], nsa_kc_w1[j], nsa_kc_w2[j],
                            nsa_vc_pe[j], nsa_vc_w1[j], nsa_vc_w2[j], nsa_out[j], batch, seq)
        xf = _ffn(xf, row(norm_ffn[i]), ffn_up[i].astype(BF16), ffn_down[i].astype(BF16), tm=1024, tf=512)
        last = i == depth - 1
        xf = _ple(xf, p[i].reshape(m, -1), row(norm_ple[i]), ple_gate[i].astype(BF16),
                  ple_proj[i].astype(BF16), row(final_norm), final=last, tm=256)
    return xf.reshape(batch, seq, d)
```

```python
import functools

import numpy as np
import jax
import jax.numpy as jnp
from jax import lax
from jax.experimental import pallas as pl
from jax.experimental.pallas import tpu as pltpu

F32 = jnp.float32
BF16 = jnp.bfloat16

EPS = 1e-6
HEAD_DIM = 128
NSA_HEADS = 16
NSA_KV_GROUPS = 4
NSA_HPG = NSA_HEADS // NSA_KV_GROUPS
ROT_DIM = HEAD_DIM // 4
ROPE_THETA = 500000.0
CMP_LEN = 32
CMP_STRIDE = 16
SEL_BLOCK = 64
SEL_TOPK = 16
WINDOW = 512
Q_BLOCK = 128
GM_CHUNK = 128
NEG_INF = -1e30
LOG2_E = 1.4426950408889634

LANES = 128
SEL_BIAS_BLOCKS = LANES
SEL_BIAS_KEYS = SEL_BIAS_BLOCKS * SEL_BLOCK
VMEM_LIMIT = 56 * 1024 * 1024


def _params(sem):
    return pltpu.CompilerParams(dimension_semantics=sem, vmem_limit_bytes=VMEM_LIMIT)


def _rmsnorm(x, g):
    return x * lax.rsqrt(jnp.mean(x * x, axis=-1, keepdims=True) + EPS) * g


def _dot(a, b):
    return jnp.dot(a, b, preferred_element_type=F32)


def _rope(x, c, sa, sb):
    return x * c + pltpu.roll(x, LANES - ROT_DIM // 2, 1) * sa + pltpu.roll(x, ROT_DIM // 2, 1) * sb


def _rope_tables(pos):
    half = ROT_DIM // 2
    inv = jnp.power(jnp.float32(ROPE_THETA), -jnp.arange(half, dtype=F32) * 2.0 / ROT_DIM)
    ang = pos.astype(F32)[:, None] * inv[None, :]
    cos, sin = jnp.cos(ang), jnp.sin(ang)
    n = pos.shape[0]
    rest = HEAD_DIM - ROT_DIM
    c = jnp.concatenate([cos, cos, jnp.ones((n, rest), F32)], axis=1)
    sa = jnp.concatenate([-sin, jnp.zeros((n, half + rest), F32)], axis=1)
    sb = jnp.concatenate([jnp.zeros((n, half), F32), sin, jnp.zeros((n, rest), F32)], axis=1)
    return c, sa, sb


def _norm_mm_body(x_ref, g_ref, w_ref, o_ref, h_ref, *, act):
    @pl.when(pl.program_id(1) == 0)
    def _():
        h_ref[...] = _rmsnorm(x_ref[...], g_ref[...]).astype(BF16)

    acc = _dot(h_ref[...], w_ref[...])
    if act == "gelu":
        acc = jax.nn.gelu(acc)
    elif act == "sigmoid":
        acc = jax.nn.sigmoid(acc)
    o_ref[...] = acc.astype(o_ref.dtype)


def _norm_mm(x, g, w, *, act, out_dtype, tm, tn):
    m, d = x.shape
    n = w.shape[1]
    return pl.pallas_call(
        functools.partial(_norm_mm_body, act=act),
        grid=(m // tm, n // tn),
        in_specs=[
            pl.BlockSpec((tm, d), lambda i, j: (i, 0)),
            pl.BlockSpec((1, d), lambda i, j: (0, 0)),
            pl.BlockSpec((d, tn), lambda i, j: (0, j)),
        ],
        out_specs=pl.BlockSpec((tm, tn), lambda i, j: (i, j)),
        out_shape=jax.ShapeDtypeStruct((m, n), out_dtype),
        scratch_shapes=[pltpu.VMEM((tm, d), BF16)],
        compiler_params=_params(("parallel", "arbitrary")),
        name="norm_mm_" + str(act),
    )(x, g, w)


def _mm_sigmoid_body(h_ref, w_ref, o_ref):
    o_ref[...] = jax.nn.sigmoid(_dot(h_ref[...], w_ref[...]))


def _mm_sigmoid(h, w, *, tm):
    m, d = h.shape
    n = w.shape[1]
    return pl.pallas_call(
        _mm_sigmoid_body,
        grid=(m // tm,),
        in_specs=[pl.BlockSpec((tm, d), lambda i: (i, 0)), pl.BlockSpec((d, n), lambda i: (0, 0))],
        out_specs=pl.BlockSpec((tm, n), lambda i: (i, 0)),
        out_shape=jax.ShapeDtypeStruct((m, n), F32),
        compiler_params=_params(("parallel",)),
        name="mm_sigmoid",
    )(h, w)


def _gmlp_body(z_ref, x_ref, lg_ref, lb_ref, ws_ref, bs_ref, wo_ref, o_ref, y_ref, *, tm, width):
    groups = ws_ref.shape[0]
    gd = width // groups
    u = z_ref[:, :width]
    v = z_ref[:, width:].astype(F32)
    mu = jnp.mean(v, axis=-1, keepdims=True)
    var = jnp.mean(jnp.square(v - mu), axis=-1, keepdims=True)
    vn = ((v - mu) * lax.rsqrt(var + EPS) * lg_ref[...] + lb_ref[...]).astype(BF16)
    r = lax.broadcasted_iota(jnp.int32, (GM_CHUNK, GM_CHUNK), 0)
    c = lax.broadcasted_iota(jnp.int32, (GM_CHUNK, GM_CHUNK), 1)
    causal = c <= r
    for g in range(groups):
        wg = jnp.where(causal, ws_ref[g], 0.0).astype(BF16)
        bg = bs_ref[:, g:g + 1]
        for ch in range(tm // GM_CHUNK):
            rows = slice(ch * GM_CHUNK, (ch + 1) * GM_CHUNK)
            cols = slice(g * gd, (g + 1) * gd)
            sv = _dot(wg, vn[rows, cols]) + bg
            y_ref[rows, cols] = (u[rows, cols].astype(F32) * sv).astype(BF16)
    o_ref[...] = x_ref[...] + _dot(y_ref[...], wo_ref[...])


def _gmlp_gate_out(z, x, ln_g, ln_b, ws, bs, w_out, *, tm):
    m, d = x.shape
    width = z.shape[1] // 2
    groups = ws.shape[0]
    return pl.pallas_call(
        functools.partial(_gmlp_body, tm=tm, width=width),
        grid=(m // tm,),
        in_specs=[
            pl.BlockSpec((tm, 2 * width), lambda i: (i, 0)),
            pl.BlockSpec((tm, d), lambda i: (i, 0)),
            pl.BlockSpec((1, width), lambda i: (0, 0)),
            pl.BlockSpec((1, width), lambda i: (0, 0)),
            pl.BlockSpec((groups, GM_CHUNK, GM_CHUNK), lambda i: (0, 0, 0)),
            pl.BlockSpec((GM_CHUNK, groups), lambda i: (0, 0)),
            pl.BlockSpec((width, d), lambda i: (0, 0)),
        ],
        out_specs=pl.BlockSpec((tm, d), lambda i: (i, 0)),
        out_shape=jax.ShapeDtypeStruct((m, d), F32),
        scratch_shapes=[pltpu.VMEM((tm, width), BF16)],
        compiler_params=_params(("parallel",)),
        name="gmlp_gate_out",
    )(z, x, ln_g, ln_b, ws, bs, w_out)


def _ffn_body(x_ref, g_ref, wu_ref, wd_ref, o_ref, h_ref):
    f = pl.program_id(1)

    @pl.when(f == 0)
    def _():
        x = x_ref[...]
        h_ref[...] = _rmsnorm(x, g_ref[...]).astype(BF16)
        o_ref[...] = x

    a = jnp.square(jnp.maximum(_dot(h_ref[...], wu_ref[...]), 0.0)).astype(BF16)
    o_ref[...] += _dot(a, wd_ref[...])


def _ffn(x, g, w_up, w_down, *, tm, tf):
    m, d = x.shape
    ff = w_up.shape[1]
    return pl.pallas_call(
        _ffn_body,
        grid=(m // tm, ff // tf),
        in_specs=[
            pl.BlockSpec((tm, d), lambda i, f: (i, 0)),
            pl.BlockSpec((1, d), lambda i, f: (0, 0)),
            pl.BlockSpec((d, tf), lambda i, f: (0, f)),
            pl.BlockSpec((tf, d), lambda i, f: (f, 0)),
        ],
        out_specs=pl.BlockSpec((tm, d), lambda i, f: (i, 0)),
        out_shape=jax.ShapeDtypeStruct((m, d), F32),
        scratch_shapes=[pltpu.VMEM((tm, d), BF16)],
        compiler_params=_params(("parallel", "arbitrary")),
        name="ffn",
    )(x, g, w_up, w_down)


def _ple_body(x_ref, p_ref, g_ref, wg_ref, wp_ref, pg_ref, *o_refs, post):
    x = x_ref[...]
    h = _rmsnorm(x, g_ref[...]).astype(BF16)
    gate = jax.nn.sigmoid(_dot(h, wg_ref[...]))
    y = x + gate * _dot(p_ref[...].astype(BF16), wp_ref[...])
    if post == "final":
        o_refs[0][...] = _rmsnorm(y, pg_ref[...])
    else:
        o_refs[0][...] = y
    if post == "next":
        o_refs[1][...] = _rmsnorm(y, pg_ref[...]).astype(BF16)


def _ple(x, p, g, w_gate, w_proj, post_g, *, post, tm):
    m, d = x.shape
    pd = p.shape[1]
    row_spec = pl.BlockSpec((tm, d), lambda i: (i, 0))
    out_specs, out_shape = row_spec, jax.ShapeDtypeStruct((m, d), F32)
    if post == "next":
        out_specs, out_shape = [row_spec, row_spec], [out_shape, jax.ShapeDtypeStruct((m, d), BF16)]
    return pl.pallas_call(
        functools.partial(_ple_body, post=post),
        grid=(m // tm,),
        in_specs=[
            row_spec,
            pl.BlockSpec((tm, pd), lambda i: (i, 0)),
            pl.BlockSpec((1, d), lambda i: (0, 0)),
            pl.BlockSpec((d, d), lambda i: (0, 0)),
            pl.BlockSpec((pd, d), lambda i: (0, 0)),
            pl.BlockSpec((1, d), lambda i: (0, 0)),
        ],
        out_specs=out_specs,
        out_shape=out_shape,
        compiler_params=_params(("parallel",)),
        name="ple_" + post,
    )(x, p, g, w_gate, w_proj, post_g)


def _mm_res_body(a_ref, w_ref, x_ref, o_ref):
    o_ref[...] = x_ref[...] + _dot(a_ref[...], w_ref[...])


def _mm_residual(a, w, x, *, tm, tn):
    m, k = a.shape
    n = w.shape[1]
    return pl.pallas_call(
        _mm_res_body,
        grid=(m // tm, n // tn),
        in_specs=[
            pl.BlockSpec((tm, k), lambda i, j: (i, 0)),
            pl.BlockSpec((k, tn), lambda i, j: (0, j)),
            pl.BlockSpec((tm, tn), lambda i, j: (i, j)),
        ],
        out_specs=pl.BlockSpec((tm, tn), lambda i, j: (i, j)),
        out_shape=jax.ShapeDtypeStruct((m, n), F32),
        compiler_params=_params(("parallel", "arbitrary")),
        name="mm_residual",
    )(a, w, x)


NSA_PROJ_TN = 4 * HEAD_DIM
NSA_PROJ_HEADS = NSA_PROJ_TN // HEAD_DIM
NSA_Q_TILES = NSA_HEADS * HEAD_DIM // NSA_PROJ_TN
ROT_Q, ROT_KS, ROT_KW = 0, NSA_HEADS, NSA_HEADS + NSA_KV_GROUPS


def _proj_rotated_body(h_ref, w_ref, c_ref, sa_ref, sb_ref, o_ref):
    acc = _dot(h_ref[...], w_ref[...])
    scale = jnp.where(pl.program_id(1) < NSA_Q_TILES, HEAD_DIM ** -0.5 * LOG2_E, 1.0).astype(F32)
    c, sa, sb = c_ref[...], sa_ref[...], sb_ref[...]
    for hh in range(NSA_PROJ_HEADS):
        seg = acc[:, hh * HEAD_DIM:(hh + 1) * HEAD_DIM]
        o_ref[hh] = (_rope(seg, c, sa, sb) * scale).astype(BF16)


def _proj_grouped_body(h_ref, w_ref, o_ref, rows_ref):
    acc = _dot(h_ref[...], w_ref[...])
    groups = acc.shape[0] // CMP_STRIDE
    for hh in range(NSA_PROJ_HEADS):
        rows_ref[hh] = acc[:, hh * HEAD_DIM:(hh + 1) * HEAD_DIM]
        o_ref[hh] = jnp.concatenate([rows_ref[hh, pl.ds(l, groups, stride=CMP_STRIDE), :]
                                     for l in range(CMP_STRIDE)], axis=1).astype(BF16)


def _proj_flipped_body(h_ref, w_ref, o_ref):
    acc = _dot(h_ref[...], w_ref[...])
    for hh in range(NSA_PROJ_HEADS):
        for t in range(acc.shape[0] // Q_BLOCK):
            o_ref[hh, t] = acc[t * Q_BLOCK:(t + 1) * Q_BLOCK, hh * HEAD_DIM:(hh + 1) * HEAD_DIM].T.astype(BF16)


def _proj(body, h, w, extra, extra_specs, out_block, out_dims, scratch, name, *, tm):
    m, d = h.shape
    n = w.shape[1]
    return pl.pallas_call(
        body,
        grid=(m // tm, n // NSA_PROJ_TN),
        in_specs=[pl.BlockSpec((tm, d), lambda i, j: (i, 0)),
                  pl.BlockSpec((d, NSA_PROJ_TN), lambda i, j: (0, j))] + extra_specs,
        out_specs=pl.BlockSpec((NSA_PROJ_HEADS,) + out_block, lambda i, j: (j, i) + (0,) * (len(out_block) - 1)),
        out_shape=jax.ShapeDtypeStruct((n // HEAD_DIM,) + out_dims, BF16),
        scratch_shapes=scratch,
        compiler_params=_params(("parallel", "arbitrary")),
        name=name,
    )(h, w, *extra)


def _nsa_projections(h, w_in, tables, seq, *, tm):
    m = h.shape[0]
    hd = HEAD_DIM
    q_end = NSA_HEADS * hd
    kv = NSA_KV_GROUPS * hd
    kc, vc, ks, vs, kw, vw = (slice(q_end + i * kv, q_end + (i + 1) * kv) for i in range(6))
    cast = lambda cols: jnp.concatenate([w_in[:, c] for c in cols], axis=1).astype(BF16)
    tpb = seq // tm
    tab_spec = pl.BlockSpec((tm, hd), lambda i, j: (i % tpb, 0))
    rotated = _proj(_proj_rotated_body, h, cast([slice(0, q_end), ks, kw]), tables, [tab_spec] * 3,
                    (tm, hd), (m, hd), [], "nsa_proj_rotated", tm=tm)
    grouped = _proj(_proj_grouped_body, h, cast([kc, vc]), (), [],
                    (tm // CMP_STRIDE, CMP_STRIDE * hd), (m // CMP_STRIDE, CMP_STRIDE * hd),
                    [pltpu.VMEM((NSA_PROJ_HEADS, tm, hd), F32)], "nsa_proj_grouped", tm=tm)
    flipped = _proj(_proj_flipped_body, h, cast([vs, vw]), (), [],
                    (tm // Q_BLOCK, hd, Q_BLOCK), (m // Q_BLOCK, hd, Q_BLOCK), [], "nsa_proj_flipped", tm=tm)
    return rotated, grouped, flipped


CMP_CHUNK = LANES


def _compress_body(kr_ref, pe_ref, w1_ref, w2_ref, c_ref, sa_ref, sb_ref, o_ref):
    kr = kr_ref[0].astype(F32)
    rows = kr.shape[0]
    a = _dot((kr + pe_ref[0, 0]).astype(BF16), w1_ref[0, 0])
    b = _dot((kr + pe_ref[0, 1]).astype(BF16), w1_ref[0, 1])
    hid = jax.nn.gelu(a + pltpu.roll(b, rows - 1, 0))
    out = _dot(hid.astype(BF16), w2_ref[0])
    is_k = pl.program_id(0) == 0

    @pl.when(is_k)
    def _():
        roped = _rope(out, c_ref[...], sa_ref[...], sb_ref[...]).astype(BF16)
        for j in range(rows // CMP_CHUNK):
            o_ref[0, 0, 0, j] = roped[j * CMP_CHUNK:(j + 1) * CMP_CHUNK]

    @pl.when(jnp.logical_not(is_k))
    def _():
        for j in range(rows // CMP_CHUNK):
            o_ref[0, 0, 0, j] = out[j * CMP_CHUNK:(j + 1) * CMP_CHUNK].T.astype(BF16)


def _compress(kr, pe, w1, w2, tables, batch, seq):
    rows = seq // CMP_STRIDE
    kdim = CMP_STRIDE * HEAD_DIM
    hid = w1.shape[-1]
    tab_spec = pl.BlockSpec((rows, HEAD_DIM), lambda w, b, g: (0, 0))
    chunks = rows // CMP_CHUNK
    return pl.pallas_call(
        _compress_body,
        grid=(2, batch, NSA_KV_GROUPS),
        in_specs=[
            pl.BlockSpec((1, rows, kdim), lambda w, b, g: (NSA_KV_GROUPS * w + g, b, 0)),
            pl.BlockSpec((1, 2, 1, kdim), lambda w, b, g: (w, 0, 0, 0)),
            pl.BlockSpec((1, 2, kdim, hid), lambda w, b, g: (w, 0, 0, 0)),
            pl.BlockSpec((1, hid, HEAD_DIM), lambda w, b, g: (w, 0, 0)),
            tab_spec, tab_spec, tab_spec,
        ],
        out_specs=pl.BlockSpec((1, 1, 1, chunks, CMP_CHUNK, HEAD_DIM), lambda w, b, g: (w, b, g, 0, 0, 0)),
        out_shape=jax.ShapeDtypeStruct((2, batch, NSA_KV_GROUPS, chunks, CMP_CHUNK, HEAD_DIM), BF16),
        compiler_params=_params(("parallel", "parallel", "parallel")),
        name="nsa_compress",
    )(kr, pe, w1, w2, *tables)


def _attn_body(q_ref, ks_ref, vs_ref, kw_ref, vw_ref, kc_ref, vc_ref, gate_ref, e_ref,
               o_ref, qaug_ref, ps_ref, sa_ref, sb_ref, sd_ref, *, ns, tk):
    qi = pl.program_id(2)
    t0 = qi * Q_BLOCK
    cols = NSA_HPG * Q_BLOCK
    halves = qaug_ref.shape[0]
    for n in range(NSA_HPG):
        qt = q_ref[n].astype(F32).T.astype(BF16)
        for h in range(halves):
            qaug_ref[h, 0:HEAD_DIM, n * Q_BLOCK:(n + 1) * Q_BLOCK] = qt
    q_t = qaug_ref[0, 0:HEAD_DIM, :]
    tq = t0 + (lax.broadcasted_iota(jnp.int32, (1, cols), 1) & (Q_BLOCK - 1))

    ncp = ps_ref.shape[0]
    step = min(2 * CMP_CHUNK, ncp)

    def compressed(rows):
        chunks = range(rows // CMP_CHUNK)
        s = _dot(jnp.concatenate([kc_ref[0, 0, 0, i] for i in chunks], axis=0), q_t)
        cend = lax.broadcasted_iota(jnp.int32, (rows, cols), 0) * CMP_STRIDE + (CMP_LEN - 1)
        s = jnp.where(cend <= tq, s, NEG_INF)
        p = jnp.exp2(s - jnp.max(s, axis=0, keepdims=True))
        acc = _dot(jnp.concatenate([vc_ref[0, 0, 0, i] for i in chunks], axis=1), p.astype(BF16))
        inv = jnp.where(tq >= CMP_LEN - 1, 1.0 / jnp.sum(p, axis=0, keepdims=True), 0.0)
        p = p * inv
        ps = p[:, 0:Q_BLOCK]
        for n in range(1, NSA_HPG):
            ps = ps + p[:, n * Q_BLOCK:(n + 1) * Q_BLOCK]
        ps_ref[0:rows, :] = ps
        if rows < ncp:
            ps_ref[rows:ncp, :] = jnp.zeros((ncp - rows, Q_BLOCK), F32)
        return acc * inv

    variants = [functools.partial(compressed, rows) for rows in range(step, ncp + 1, step)]
    reach = jnp.minimum((t0 + Q_BLOCK - CMP_LEN) // (CMP_STRIDE * step), len(variants) - 1)
    o_c = lax.switch(reach, variants) if len(variants) > 1 else variants[0]()

    ratio = SEL_BLOCK // CMP_STRIDE
    r = [ps_ref[pl.ds(j, ns, stride=ratio), :] for j in range(ratio)]
    prev = pltpu.roll(r[ratio - 1], 1, 0)
    prev = jnp.where(lax.broadcasted_iota(jnp.int32, (ns, Q_BLOCK), 0) == 0, 0.0, prev)
    imp = 2.0 * (r[0] + r[1] + r[2]) + r[3] + prev
    nsp = halves * SEL_BIAS_BLOCKS
    if nsp > ns:
        imp = jnp.concatenate([imp, jnp.zeros((nsp - ns, Q_BLOCK), F32)], axis=0)

    blk = lax.broadcasted_iota(jnp.int32, (nsp, Q_BLOCK), 0)
    blk_f = blk.astype(F32)
    cur = (t0 + lax.broadcasted_iota(jnp.int32, (nsp, Q_BLOCK), 1)) // SEL_BLOCK
    valid = blk <= cur
    forced = (blk == 0) | (blk == cur) | (blk == cur - 1)
    n_forced = 3
    rounds = min(SEL_TOPK, ns) - n_forced
    free = valid & jnp.logical_not(forced)
    score0 = jnp.where(free, imp, NEG_INF)

    score = score0
    for _ in range(rounds):
        score = jnp.where(score == jnp.max(score, axis=0, keepdims=True), -jnp.inf, score)
    removed = score == -jnp.inf
    n_removed = jnp.sum(jnp.where(removed & free, 1.0, 0.0), axis=0, keepdims=True)
    cur_row = (t0 + lax.broadcasted_iota(jnp.int32, (1, Q_BLOCK), 1)) // SEL_BLOCK
    n_free = jnp.clip(cur_row + 1 - n_forced, 0, rounds).astype(F32)
    n_tied = jnp.sum(jnp.where(n_removed == n_free, 0.0, 1.0))

    def tie_breaking_rounds():
        score, taken = score0, jnp.zeros((nsp, Q_BLOCK), F32)
        for _ in range(rounds):
            top = jnp.max(score, axis=0, keepdims=True)
            first = jnp.min(jnp.where(score == top, blk_f, float(nsp)), axis=0, keepdims=True)
            hit = blk_f == first
            taken = jnp.where(hit, 1.0, taken)
            score = jnp.where(hit, -jnp.inf, score)
        return taken

    taken = lax.cond(n_tied == 0.0, lambda: jnp.where(removed, 1.0, 0.0), tie_breaking_rounds)
    bias = jnp.where((forced | (taken > 0.0)) & valid, 0.0, NEG_INF).astype(BF16)

    for h in range(halves):
        bh = bias[h * SEL_BIAS_BLOCKS:(h + 1) * SEL_BIAS_BLOCKS]
        for n in range(NSA_HPG):
            qaug_ref[h, HEAD_DIM:HEAD_DIM + SEL_BIAS_BLOCKS, n * Q_BLOCK:(n + 1) * Q_BLOCK] = bh

    init = (jnp.full((1, cols), NEG_INF, F32), jnp.zeros((1, cols), F32),
            jnp.zeros((HEAD_DIM, cols), F32))

    n_win = WINDOW // Q_BLOCK + 1
    win_keys = n_win * Q_BLOCK
    w0 = pl.multiple_of(jnp.maximum(t0 - WINDOW, 0), Q_BLOCK)
    newest = tq - w0
    krow = lax.broadcasted_iota(jnp.int32, (win_keys, cols), 0)
    s_w = jnp.where(krow <= newest, _dot(kw_ref[0, pl.ds(w0, win_keys), :], q_t), NEG_INF)
    oldest = lax.broadcasted_iota(jnp.int32, (Q_BLOCK, cols), 0) > newest - WINDOW
    s_w = jnp.concatenate([jnp.where(oldest, s_w[:Q_BLOCK], NEG_INF), s_w[Q_BLOCK:]], axis=0)
    p_w = jnp.exp2(s_w - jnp.max(s_w, axis=0, keepdims=True))
    vw_t = jnp.concatenate([vw_ref[0, w0 // Q_BLOCK + w] for w in range(n_win)], axis=1)
    o_w = _dot(vw_t, p_w.astype(BF16)) * (1.0 / jnp.sum(p_w, axis=0, keepdims=True))

    def sel_scores(j, s_ref, diagonal=False):
        k0 = pl.multiple_of(j * tk, tk)
        e0 = pl.multiple_of(k0 % SEL_BIAS_KEYS, tk)
        kaug = jnp.concatenate([ks_ref[0, pl.ds(k0, tk), :], e_ref[pl.ds(e0, tk), :]], axis=1)
        s = _dot(kaug, qaug_ref[k0 // SEL_BIAS_KEYS])
        if diagonal:
            lk = lax.broadcasted_iota(jnp.int32, (tk, cols), 0)
            lq = lax.broadcasted_iota(jnp.int32, (tk, cols), 1) & (Q_BLOCK - 1)
            s = jnp.where(lk - lq <= t0 - k0, s, NEG_INF)
        s_ref[...] = s
        return jnp.max(s, axis=0, keepdims=True)

    def sel_update(s_ref, s_max, j, carry):
        m, l, acc = carry
        m_new = jnp.maximum(m, s_max)
        alpha = jnp.exp2(m - m_new)
        p = jnp.exp2(s_ref[...] - m_new)
        l = alpha * l + jnp.sum(p, axis=0, keepdims=True)
        v_t = jnp.concatenate([vs_ref[0, (tk // Q_BLOCK) * j + i] for i in range(tk // Q_BLOCK)], axis=1)
        acc = alpha * acc + _dot(v_t, p.astype(BF16))
        return m_new, l, acc

    n_full = t0 // tk
    last_full = jnp.maximum(n_full - 1, 0)
    max_d = sel_scores(n_full, sd_ref, diagonal=True)
    max_a = sel_scores(0, sa_ref)

    def sel_pair(j, carry):
        max_a, state = carry
        max_b = sel_scores(j + 1, sb_ref)
        state = sel_update(sa_ref, max_a, j, state)
        max_a = sel_scores(jnp.minimum(j + 2, last_full), sa_ref)
        return max_a, sel_update(sb_ref, max_b, j + 1, state)

    unroll = 4
    carry = lax.fori_loop(0, n_full // unroll,
                          lambda jj, c: sel_pair(unroll * jj + 2, sel_pair(unroll * jj, c)), (max_a, init))
    rest = n_full % unroll
    carry = lax.cond(rest >= 2, lambda c: sel_pair(n_full - rest, c), lambda c: c, carry)
    max_a, state = carry
    state = lax.cond(rest % 2 == 1, lambda c: sel_update(sa_ref, max_a, n_full - 1, c), lambda c: c, state)
    _, l_s, acc_s = sel_update(sd_ref, max_d, n_full, state)
    o_s = acc_s * (1.0 / l_s)

    g_t = gate_ref[...].T
    for n in range(NSA_HPG):
        c = slice(n * Q_BLOCK, (n + 1) * Q_BLOCK)
        o = (g_t[n:n + 1] * o_c[:, c] + g_t[NSA_HPG + n:NSA_HPG + n + 1] * o_s[:, c]
             + g_t[2 * NSA_HPG + n:2 * NSA_HPG + n + 1] * o_w[:, c])
        o_ref[:, n * HEAD_DIM:(n + 1) * HEAD_DIM] = o.T.astype(BF16)


def _nsa_attention(rotated, flipped, cmp, gates, onehot, batch, seq, *, tk):
    nq = seq // Q_BLOCK
    ns = seq // SEL_BLOCK
    ncp = seq // CMP_STRIDE
    halves = -(-ns // SEL_BIAS_BLOCKS)
    m = batch * seq
    once = pl.Buffered(1)

    def slab(first):
        return pl.BlockSpec((1, seq, HEAD_DIM), lambda b, g, qi: (first + g, b, 0), pipeline_mode=once)

    def slab_t(first):
        return pl.BlockSpec((1, seq // Q_BLOCK, HEAD_DIM, Q_BLOCK), lambda b, g, qi: (first + g, b, 0, 0),
                            pipeline_mode=once)

    def cmp_spec(which):
        return pl.BlockSpec((1, 1, 1, ncp // CMP_CHUNK, CMP_CHUNK, HEAD_DIM),
                            lambda b, g, qi: (which, b, g, 0, 0, 0))

    return pl.pallas_call(
        functools.partial(_attn_body, ns=ns, tk=tk),
        grid=(batch, NSA_KV_GROUPS, nq),
        in_specs=[
            pl.BlockSpec((NSA_HPG, Q_BLOCK, HEAD_DIM), lambda b, g, qi: (g, b * nq + qi, 0)),
            slab(ROT_KS), slab_t(0), slab(ROT_KW), slab_t(NSA_KV_GROUPS),
            cmp_spec(0), cmp_spec(1),
            pl.BlockSpec((Q_BLOCK, LANES), lambda b, g, qi: (b * nq + qi, g)),
            pl.BlockSpec(onehot.shape, lambda b, g, qi: (0, 0), pipeline_mode=once),
        ],
        out_specs=pl.BlockSpec((Q_BLOCK, NSA_HPG * HEAD_DIM), lambda b, g, qi: (b * nq + qi, g)),
        out_shape=jax.ShapeDtypeStruct((m, NSA_HEADS * HEAD_DIM), BF16),
        scratch_shapes=[
            pltpu.VMEM((halves, 2 * HEAD_DIM, NSA_HPG * Q_BLOCK), BF16),
            pltpu.VMEM((ncp, Q_BLOCK), F32),
            pltpu.VMEM((tk, NSA_HPG * Q_BLOCK), F32),
            pltpu.VMEM((tk, NSA_HPG * Q_BLOCK), F32),
            pltpu.VMEM((tk, NSA_HPG * Q_BLOCK), F32),
        ],
        compiler_params=_params(("parallel", "parallel", "arbitrary")),
        name="nsa_attention",
    )(rotated, rotated, flipped, rotated, flipped, cmp, cmp, gates, onehot)


def _sel_onehot(seq):
    keys = np.arange(min(seq, SEL_BIAS_KEYS))
    onehot = (keys[:, None] // SEL_BLOCK == np.arange(SEL_BIAS_BLOCKS)[None, :]).astype(np.float32)
    return jnp.asarray(onehot, BF16)


def _gate_weight(w_gl):
    d = w_gl.shape[0]
    w = w_gl.reshape(d, 3, NSA_KV_GROUPS, NSA_HPG).transpose(0, 2, 1, 3).reshape(d, NSA_KV_GROUPS, 3 * NSA_HPG)
    w = jnp.pad(w, ((0, 0), (0, 0), (0, LANES - 3 * NSA_HPG)))
    return w.reshape(d, NSA_KV_GROUPS * LANES)


def _nsa_layer(x, h, w_in, kc_pe, kc_w1, kc_w2, vc_pe, vc_w1, vc_w2, w_out, batch, seq, *, tk=1024):
    assert SEL_BLOCK == 4 * CMP_STRIDE and CMP_LEN == 2 * CMP_STRIDE and seq % max(tk, CMP_STRIDE * CMP_CHUNK) == 0
    n_main = NSA_HEADS * HEAD_DIM + 6 * NSA_KV_GROUPS * HEAD_DIM
    rotated, kr, flipped = _nsa_projections(h, w_in, _rope_tables(jnp.arange(seq)), seq, tm=1024)
    gates = _mm_sigmoid(h, _gate_weight(w_in[:, n_main:]).astype(BF16), tm=1024)
    rows = seq // CMP_STRIDE
    cmp_tables = _rope_tables(jnp.arange(rows) * CMP_STRIDE + CMP_LEN - 1)
    half = CMP_LEN // 2
    pe = jnp.stack([kc_pe, vc_pe]).reshape(2, 2, 1, half * HEAD_DIM)
    w1 = jnp.stack([kc_w1, vc_w1]).astype(BF16)
    w1 = w1.reshape(2, 2, half * HEAD_DIM, w1.shape[-1])
    w2 = jnp.stack([kc_w2, vc_w2]).astype(BF16)
    cmp = _compress(kr, pe, w1, w2, cmp_tables, batch, seq)
    att = _nsa_attention(rotated, flipped, cmp, gates, _sel_onehot(seq), batch, seq, tk=tk)
    return _mm_residual(att, w_out.astype(BF16), x, tm=1024, tn=512)


def kernel(x, p, norm_mix, norm_ffn, norm_ple, ffn_up, ffn_down, ple_proj, ple_gate, gm_in, gm_ln_g, gm_ln_b, gm_ws, gm_bs, gm_out, nsa_in, nsa_kc_pe, nsa_kc_w1, nsa_kc_w2, nsa_vc_pe, nsa_vc_w1, nsa_vc_w2, nsa_out, final_norm):
    batch, seq, d = x.shape
    m = batch * seq
    depth = p.shape[0]
    xf = x.reshape(m, d)
    row = lambda v: v.reshape(1, -1)
    for i in range(depth):
        j = i // 2
        if i % 2 == 0:
            z = _norm_mm(xf, row(norm_mix[i]), gm_in[j].astype(BF16), act="gelu", out_dtype=BF16,
                         tm=1024, tn=512)
            xf = _gmlp_gate_out(z, xf, row(gm_ln_g[j]), row(gm_ln_b[j]), gm_ws[j], gm_bs[j].T,
                                gm_out[j].astype(BF16), tm=512)
        else:
            xf = _nsa_layer(xf, h_mix, nsa_in[j], nsa_kc_pe[j], nsa_kc_w1[j], nsa_kc_w2[j],
                            nsa_vc_pe[j], nsa_vc_w1[j], nsa_vc_w2[j], nsa_out[j], batch, seq)
        xf = _ffn(xf, row(norm_ffn[i]), ffn_up[i].astype(BF16), ffn_down[i].astype(BF16), tm=1024, tf=512)
        if i == depth - 1:
            post, post_g = "final", final_norm
        elif (i + 1) % 2 == 1:
            post, post_g = "next", norm_mix[i + 1]
        else:
            post, post_g = "none", final_norm
        out = _ple(xf, p[i].reshape(m, -1), row(norm_ple[i]), ple_gate[i].astype(BF16),
                   ple_proj[i].astype(BF16), row(post_g), post=post, tm=256)
        xf, h_mix = out if post == "next" else (out, None)
    return xf.reshape(batch, seq, d)
```

```python
import functools

import numpy as np
import jax
import jax.numpy as jnp
from jax import lax
from jax.experimental import pallas as pl
from jax.experimental.pallas import tpu as pltpu

F32 = jnp.float32
BF16 = jnp.bfloat16

EPS = 1e-6
HEAD_DIM = 128
NSA_HEADS = 16
NSA_KV_GROUPS = 4
NSA_HPG = NSA_HEADS // NSA_KV_GROUPS
ROT_DIM = HEAD_DIM // 4
ROPE_THETA = 500000.0
CMP_LEN = 32
CMP_STRIDE = 16
SEL_BLOCK = 64
SEL_TOPK = 16
WINDOW = 512
Q_BLOCK = 128
GM_CHUNK = 128
NEG_INF = -1e30
LOG2_E = 1.4426950408889634

LANES = 128
SEL_BIAS_BLOCKS = LANES
SEL_BIAS_KEYS = SEL_BIAS_BLOCKS * SEL_BLOCK
VMEM_LIMIT = 56 * 1024 * 1024


def _params(sem):
    return pltpu.CompilerParams(dimension_semantics=sem, vmem_limit_bytes=VMEM_LIMIT)


def _rmsnorm(x, g):
    return x * lax.rsqrt(jnp.mean(x * x, axis=-1, keepdims=True) + EPS) * g


def _dot(a, b):
    return jnp.dot(a, b, preferred_element_type=F32)


def _rope(x, c, sa, sb):
    return x * c + pltpu.roll(x, LANES - ROT_DIM // 2, 1) * sa + pltpu.roll(x, ROT_DIM // 2, 1) * sb


def _rope_tables(pos):
    half = ROT_DIM // 2
    inv = jnp.power(jnp.float32(ROPE_THETA), -jnp.arange(half, dtype=F32) * 2.0 / ROT_DIM)
    ang = pos.astype(F32)[:, None] * inv[None, :]
    cos, sin = jnp.cos(ang), jnp.sin(ang)
    n = pos.shape[0]
    rest = HEAD_DIM - ROT_DIM
    c = jnp.concatenate([cos, cos, jnp.ones((n, rest), F32)], axis=1)
    sa = jnp.concatenate([-sin, jnp.zeros((n, half + rest), F32)], axis=1)
    sb = jnp.concatenate([jnp.zeros((n, half), F32), sin, jnp.zeros((n, rest), F32)], axis=1)
    return c, sa, sb


def _norm_mm_body(x_ref, g_ref, w_ref, o_ref, h_ref, *, act):
    @pl.when(pl.program_id(1) == 0)
    def _():
        h_ref[...] = _rmsnorm(x_ref[...], g_ref[...]).astype(BF16)

    acc = _dot(h_ref[...], w_ref[...])
    if act == "gelu":
        acc = jax.nn.gelu(acc)
    elif act == "sigmoid":
        acc = jax.nn.sigmoid(acc)
    o_ref[...] = acc.astype(o_ref.dtype)


def _norm_mm(x, g, w, *, act, out_dtype, tm, tn):
    m, d = x.shape
    n = w.shape[1]
    return pl.pallas_call(
        functools.partial(_norm_mm_body, act=act),
        grid=(m // tm, n // tn),
        in_specs=[
            pl.BlockSpec((tm, d), lambda i, j: (i, 0)),
            pl.BlockSpec((1, d), lambda i, j: (0, 0)),
            pl.BlockSpec((d, tn), lambda i, j: (0, j)),
        ],
        out_specs=pl.BlockSpec((tm, tn), lambda i, j: (i, j)),
        out_shape=jax.ShapeDtypeStruct((m, n), out_dtype),
        scratch_shapes=[pltpu.VMEM((tm, d), BF16)],
        compiler_params=_params(("parallel", "arbitrary")),
        name="norm_mm_" + str(act),
    )(x, g, w)


def _mm_sigmoid_body(h_ref, w_ref, o_ref):
    o_ref[...] = jax.nn.sigmoid(_dot(h_ref[...], w_ref[...]))


def _mm_sigmoid(h, w, *, tm):
    m, d = h.shape
    n = w.shape[1]
    return pl.pallas_call(
        _mm_sigmoid_body,
        grid=(m // tm,),
        in_specs=[pl.BlockSpec((tm, d), lambda i: (i, 0)), pl.BlockSpec((d, n), lambda i: (0, 0))],
        out_specs=pl.BlockSpec((tm, n), lambda i: (i, 0)),
        out_shape=jax.ShapeDtypeStruct((m, n), F32),
        compiler_params=_params(("parallel",)),
        name="mm_sigmoid",
    )(h, w)


def _gmlp_body(z_ref, x_ref, lg_ref, lb_ref, ws_ref, bs_ref, wo_ref, o_ref, y_ref, *, tm, width):
    groups = ws_ref.shape[0]
    gd = width // groups
    u = z_ref[:, :width]
    v = z_ref[:, width:].astype(F32)
    mu = jnp.mean(v, axis=-1, keepdims=True)
    var = jnp.mean(jnp.square(v - mu), axis=-1, keepdims=True)
    vn = ((v - mu) * lax.rsqrt(var + EPS) * lg_ref[...] + lb_ref[...]).astype(BF16)
    r = lax.broadcasted_iota(jnp.int32, (GM_CHUNK, GM_CHUNK), 0)
    c = lax.broadcasted_iota(jnp.int32, (GM_CHUNK, GM_CHUNK), 1)
    causal = c <= r
    for g in range(groups):
        wg = jnp.where(causal, ws_ref[g], 0.0).astype(BF16)
        bg = bs_ref[:, g:g + 1]
        for ch in range(tm // GM_CHUNK):
            rows = slice(ch * GM_CHUNK, (ch + 1) * GM_CHUNK)
            cols = slice(g * gd, (g + 1) * gd)
            sv = _dot(wg, vn[rows, cols]) + bg
            y_ref[rows, cols] = (u[rows, cols].astype(F32) * sv).astype(BF16)
    o_ref[...] = x_ref[...] + _dot(y_ref[...], wo_ref[...])


def _gmlp_gate_out(z, x, ln_g, ln_b, ws, bs, w_out, *, tm):
    m, d = x.shape
    width = z.shape[1] // 2
    groups = ws.shape[0]
    return pl.pallas_call(
        functools.partial(_gmlp_body, tm=tm, width=width),
        grid=(m // tm,),
        in_specs=[
            pl.BlockSpec((tm, 2 * width), lambda i: (i, 0)),
            pl.BlockSpec((tm, d), lambda i: (i, 0)),
            pl.BlockSpec((1, width), lambda i: (0, 0)),
            pl.BlockSpec((1, width), lambda i: (0, 0)),
            pl.BlockSpec((groups, GM_CHUNK, GM_CHUNK), lambda i: (0, 0, 0)),
            pl.BlockSpec((GM_CHUNK, groups), lambda i: (0, 0)),
            pl.BlockSpec((width, d), lambda i: (0, 0)),
        ],
        out_specs=pl.BlockSpec((tm, d), lambda i: (i, 0)),
        out_shape=jax.ShapeDtypeStruct((m, d), F32),
        scratch_shapes=[pltpu.VMEM((tm, width), BF16)],
        compiler_params=_params(("parallel",)),
        name="gmlp_gate_out",
    )(z, x, ln_g, ln_b, ws, bs, w_out)


def _ffn_body(x_ref, g_ref, wu_ref, wd_ref, o_ref, h_ref):
    f = pl.program_id(1)

    @pl.when(f == 0)
    def _():
        x = x_ref[...]
        h_ref[...] = _rmsnorm(x, g_ref[...]).astype(BF16)
        o_ref[...] = x

    a = jnp.square(jnp.maximum(_dot(h_ref[...], wu_ref[...]), 0.0)).astype(BF16)
    o_ref[...] += _dot(a, wd_ref[...])


def _ffn(x, g, w_up, w_down, *, tm, tf):
    m, d = x.shape
    ff = w_up.shape[1]
    return pl.pallas_call(
        _ffn_body,
        grid=(m // tm, ff // tf),
        in_specs=[
            pl.BlockSpec((tm, d), lambda i, f: (i, 0)),
            pl.BlockSpec((1, d), lambda i, f: (0, 0)),
            pl.BlockSpec((d, tf), lambda i, f: (0, f)),
            pl.BlockSpec((tf, d), lambda i, f: (f, 0)),
        ],
        out_specs=pl.BlockSpec((tm, d), lambda i, f: (i, 0)),
        out_shape=jax.ShapeDtypeStruct((m, d), F32),
        scratch_shapes=[pltpu.VMEM((tm, d), BF16)],
        compiler_params=_params(("parallel", "arbitrary")),
        name="ffn",
    )(x, g, w_up, w_down)


def _ple_body(x_ref, p_ref, g_ref, wg_ref, wp_ref, pg_ref, *o_refs, post):
    x = x_ref[...]
    h = _rmsnorm(x, g_ref[...]).astype(BF16)
    gate = jax.nn.sigmoid(_dot(h, wg_ref[...]))
    y = x + gate * _dot(p_ref[...].astype(BF16), wp_ref[...])
    if post == "final":
        o_refs[0][...] = _rmsnorm(y, pg_ref[...])
    else:
        o_refs[0][...] = y
    if post == "next":
        o_refs[1][...] = _rmsnorm(y, pg_ref[...]).astype(BF16)


def _ple(x, p, g, w_gate, w_proj, post_g, *, post, tm):
    m, d = x.shape
    pd = p.shape[1]
    row_spec = pl.BlockSpec((tm, d), lambda i: (i, 0))
    out_specs, out_shape = row_spec, jax.ShapeDtypeStruct((m, d), F32)
    if post == "next":
        out_specs, out_shape = [row_spec, row_spec], [out_shape, jax.ShapeDtypeStruct((m, d), BF16)]
    return pl.pallas_call(
        functools.partial(_ple_body, post=post),
        grid=(m // tm,),
        in_specs=[
            row_spec,
            pl.BlockSpec((tm, pd), lambda i: (i, 0)),
            pl.BlockSpec((1, d), lambda i: (0, 0)),
            pl.BlockSpec((d, d), lambda i: (0, 0)),
            pl.BlockSpec((pd, d), lambda i: (0, 0)),
            pl.BlockSpec((1, d), lambda i: (0, 0)),
        ],
        out_specs=out_specs,
        out_shape=out_shape,
        compiler_params=_params(("parallel",)),
        name="ple_" + post,
    )(x, p, g, w_gate, w_proj, post_g)


def _mm_res_body(a_ref, w_ref, x_ref, o_ref):
    o_ref[...] = x_ref[...] + _dot(a_ref[...], w_ref[...])


def _mm_residual(a, w, x, *, tm, tn):
    m, k = a.shape
    n = w.shape[1]
    return pl.pallas_call(
        _mm_res_body,
        grid=(m // tm, n // tn),
        in_specs=[
            pl.BlockSpec((tm, k), lambda i, j: (i, 0)),
            pl.BlockSpec((k, tn), lambda i, j: (0, j)),
            pl.BlockSpec((tm, tn), lambda i, j: (i, j)),
        ],
        out_specs=pl.BlockSpec((tm, tn), lambda i, j: (i, j)),
        out_shape=jax.ShapeDtypeStruct((m, n), F32),
        compiler_params=_params(("parallel", "arbitrary")),
        name="mm_residual",
    )(a, w, x)


ROT_Q, ROT_KS, ROT_KW = 0, NSA_HEADS, NSA_HEADS + NSA_KV_GROUPS


def _proj_rotated_body(h_ref, w_ref, c_ref, sa_ref, sb_ref, o_ref):
    acc = _dot(h_ref[...], w_ref[...])
    c, sa, sb = c_ref[...], sa_ref[...], sb_ref[...]
    for hh in range(o_ref.shape[0]):
        seg = _rope(acc[:, hh * HEAD_DIM:(hh + 1) * HEAD_DIM], c, sa, sb)
        if hh < ROT_KS:
            seg = seg * (HEAD_DIM ** -0.5 * LOG2_E)
        o_ref[hh] = seg.astype(BF16)


def _proj_grouped_body(h_ref, w_ref, o_ref, rows_ref):
    acc = _dot(h_ref[...], w_ref[...])
    groups = acc.shape[0] // CMP_STRIDE
    for hh in range(o_ref.shape[0]):
        rows_ref[hh] = acc[:, hh * HEAD_DIM:(hh + 1) * HEAD_DIM]
        o_ref[hh] = jnp.concatenate([rows_ref[hh, pl.ds(l, groups, stride=CMP_STRIDE), :]
                                     for l in range(CMP_STRIDE)], axis=1).astype(BF16)


def _proj_flipped_body(h_ref, w_ref, o_ref):
    acc = _dot(h_ref[...], w_ref[...])
    tile = o_ref.shape[-1]
    for hh in range(o_ref.shape[0]):
        for t in range(acc.shape[0] // tile):
            o_ref[hh, t] = acc[t * tile:(t + 1) * tile, hh * HEAD_DIM:(hh + 1) * HEAD_DIM].T.astype(BF16)


def _proj(body, h, w, extra, extra_specs, out_block, out_dims, scratch, name, *, tm):
    m, d = h.shape
    n = w.shape[1]
    heads = n // HEAD_DIM
    return pl.pallas_call(
        body,
        grid=(m // tm,),
        in_specs=[pl.BlockSpec((tm, d), lambda i: (i, 0)), pl.BlockSpec((d, n), lambda i: (0, 0))] + extra_specs,
        out_specs=pl.BlockSpec((heads,) + out_block, lambda i: (0, i) + (0,) * (len(out_block) - 1)),
        out_shape=jax.ShapeDtypeStruct((heads,) + out_dims, BF16),
        scratch_shapes=scratch,
        compiler_params=_params(("parallel",)),
        name=name,
    )(h, w, *extra)


def _nsa_projections(h, w_in, tables, seq, *, tm, sel_tile):
    m = h.shape[0]
    hd = HEAD_DIM
    q_end = NSA_HEADS * hd
    kv = NSA_KV_GROUPS * hd
    kc, vc, ks, vs, kw, vw = (slice(q_end + i * kv, q_end + (i + 1) * kv) for i in range(6))
    cast = lambda cols: jnp.concatenate([w_in[:, c] for c in cols], axis=1).astype(BF16)
    tpb = seq // tm
    tab_spec = pl.BlockSpec((tm, hd), lambda i: (i % tpb, 0))
    rotated = _proj(_proj_rotated_body, h, cast([slice(0, q_end), ks, kw]), tables, [tab_spec] * 3,
                    (tm, hd), (m, hd), [], "nsa_proj_rotated", tm=tm)
    grouped = _proj(_proj_grouped_body, h, cast([kc, vc]), (), [],
                    (tm // CMP_STRIDE, CMP_STRIDE * hd), (m // CMP_STRIDE, CMP_STRIDE * hd),
                    [pltpu.VMEM((2 * NSA_KV_GROUPS, tm, hd), F32)], "nsa_proj_grouped", tm=tm)
    flipped = [_proj(_proj_flipped_body, h, cast([cols]), (), [], (tile_rows // tile, hd, tile),
                     (m // tile, hd, tile), [], "nsa_proj_flipped", tm=tile_rows)
               for cols, tile, tile_rows in ((vs, sel_tile, max(tm, sel_tile)), (vw, Q_BLOCK, tm))]
    return rotated, grouped, flipped


CMP_CHUNK = LANES


def _compress_body(kr_ref, pe_ref, w1_ref, w2_ref, c_ref, sa_ref, sb_ref, o_ref):
    kr = kr_ref[0].astype(F32)
    rows = kr.shape[0]
    a = _dot((kr + pe_ref[0, 0]).astype(BF16), w1_ref[0, 0])
    b = _dot((kr + pe_ref[0, 1]).astype(BF16), w1_ref[0, 1])
    hid = jax.nn.gelu(a + pltpu.roll(b, rows - 1, 0))
    out = _dot(hid.astype(BF16), w2_ref[0])
    is_k = pl.program_id(0) == 0

    @pl.when(is_k)
    def _():
        roped = _rope(out, c_ref[...], sa_ref[...], sb_ref[...]).astype(BF16)
        for j in range(rows // CMP_CHUNK):
            o_ref[0, 0, 0, j] = roped[j * CMP_CHUNK:(j + 1) * CMP_CHUNK]

    @pl.when(jnp.logical_not(is_k))
    def _():
        for j in range(rows // CMP_CHUNK):
            o_ref[0, 0, 0, j] = out[j * CMP_CHUNK:(j + 1) * CMP_CHUNK].T.astype(BF16)


def _compress(kr, pe, w1, w2, tables, batch, seq):
    rows = seq // CMP_STRIDE
    kdim = CMP_STRIDE * HEAD_DIM
    hid = w1.shape[-1]
    tab_spec = pl.BlockSpec((rows, HEAD_DIM), lambda w, b, g: (0, 0))
    chunks = rows // CMP_CHUNK
    return pl.pallas_call(
        _compress_body,
        grid=(2, batch, NSA_KV_GROUPS),
        in_specs=[
            pl.BlockSpec((1, rows, kdim), lambda w, b, g: (NSA_KV_GROUPS * w + g, b, 0)),
            pl.BlockSpec((1, 2, 1, kdim), lambda w, b, g: (w, 0, 0, 0)),
            pl.BlockSpec((1, 2, kdim, hid), lambda w, b, g: (w, 0, 0, 0)),
            pl.BlockSpec((1, hid, HEAD_DIM), lambda w, b, g: (w, 0, 0)),
            tab_spec, tab_spec, tab_spec,
        ],
        out_specs=pl.BlockSpec((1, 1, 1, chunks, CMP_CHUNK, HEAD_DIM), lambda w, b, g: (w, b, g, 0, 0, 0)),
        out_shape=jax.ShapeDtypeStruct((2, batch, NSA_KV_GROUPS, chunks, CMP_CHUNK, HEAD_DIM), BF16),
        compiler_params=_params(("parallel", "parallel", "parallel")),
        name="nsa_compress",
    )(kr, pe, w1, w2, *tables)


def _attn_body(q_ref, ks_ref, vs_ref, kw_ref, vw_ref, kc_ref, vc_ref, gate_ref, e_ref,
               o_ref, qaug_ref, ps_ref, sa_ref, sb_ref, sd_ref, *, ns, tk):
    qi = pl.program_id(2)
    t0 = qi * Q_BLOCK
    cols = NSA_HPG * Q_BLOCK
    halves = qaug_ref.shape[0]
    for n in range(NSA_HPG):
        qt = q_ref[n].astype(F32).T.astype(BF16)
        for h in range(halves):
            qaug_ref[h, 0:HEAD_DIM, n * Q_BLOCK:(n + 1) * Q_BLOCK] = qt
    q_t = qaug_ref[0, 0:HEAD_DIM, :]
    tq = t0 + (lax.broadcasted_iota(jnp.int32, (1, cols), 1) & (Q_BLOCK - 1))

    ncp = ps_ref.shape[0]
    step = min(2 * CMP_CHUNK, ncp)

    def compressed(rows):
        chunks = range(rows // CMP_CHUNK)
        s = _dot(jnp.concatenate([kc_ref[0, 0, 0, i] for i in chunks], axis=0), q_t)
        cend = lax.broadcasted_iota(jnp.int32, (rows, cols), 0) * CMP_STRIDE + (CMP_LEN - 1)
        s = jnp.where(cend <= tq, s, NEG_INF)
        p = jnp.exp2(s - jnp.max(s, axis=0, keepdims=True))
        acc = _dot(jnp.concatenate([vc_ref[0, 0, 0, i] for i in chunks], axis=1), p.astype(BF16))
        inv = jnp.where(tq >= CMP_LEN - 1, 1.0 / jnp.sum(p, axis=0, keepdims=True), 0.0)
        p = p * inv
        ps = p[:, 0:Q_BLOCK]
        for n in range(1, NSA_HPG):
            ps = ps + p[:, n * Q_BLOCK:(n + 1) * Q_BLOCK]
        ps_ref[0:rows, :] = ps
        if rows < ncp:
            ps_ref[rows:ncp, :] = jnp.zeros((ncp - rows, Q_BLOCK), F32)
        return acc * inv

    variants = [functools.partial(compressed, rows) for rows in range(step, ncp + 1, step)]
    reach = jnp.minimum((t0 + Q_BLOCK - CMP_LEN) // (CMP_STRIDE * step), len(variants) - 1)
    o_c = lax.switch(reach, variants) if len(variants) > 1 else variants[0]()

    ratio = SEL_BLOCK // CMP_STRIDE
    r = [ps_ref[pl.ds(j, ns, stride=ratio), :] for j in range(ratio)]
    prev = pltpu.roll(r[ratio - 1], 1, 0)
    prev = jnp.where(lax.broadcasted_iota(jnp.int32, (ns, Q_BLOCK), 0) == 0, 0.0, prev)
    imp = 2.0 * (r[0] + r[1] + r[2]) + r[3] + prev
    nsp = halves * SEL_BIAS_BLOCKS
    if nsp > ns:
        imp = jnp.concatenate([imp, jnp.zeros((nsp - ns, Q_BLOCK), F32)], axis=0)

    blk = lax.broadcasted_iota(jnp.int32, (nsp, Q_BLOCK), 0)
    blk_f = blk.astype(F32)
    cur = (t0 + lax.broadcasted_iota(jnp.int32, (nsp, Q_BLOCK), 1)) // SEL_BLOCK
    valid = blk <= cur
    forced = (blk == 0) | (blk == cur) | (blk == cur - 1)
    n_forced = 3
    rounds = min(SEL_TOPK, ns) - n_forced
    free = valid & jnp.logical_not(forced)
    score0 = jnp.where(free, imp, NEG_INF)

    score = score0
    for _ in range(rounds):
        score = jnp.where(score == jnp.max(score, axis=0, keepdims=True), -jnp.inf, score)
    removed = score == -jnp.inf
    n_removed = jnp.sum(jnp.where(removed & free, 1.0, 0.0), axis=0, keepdims=True)
    cur_row = (t0 + lax.broadcasted_iota(jnp.int32, (1, Q_BLOCK), 1)) // SEL_BLOCK
    n_free = jnp.clip(cur_row + 1 - n_forced, 0, rounds).astype(F32)
    n_tied = jnp.sum(jnp.where(n_removed == n_free, 0.0, 1.0))

    def tie_breaking_rounds():
        score, taken = score0, jnp.zeros((nsp, Q_BLOCK), F32)
        for _ in range(rounds):
            top = jnp.max(score, axis=0, keepdims=True)
            first = jnp.min(jnp.where(score == top, blk_f, float(nsp)), axis=0, keepdims=True)
            hit = blk_f == first
            taken = jnp.where(hit, 1.0, taken)
            score = jnp.where(hit, -jnp.inf, score)
        return taken

    taken = lax.cond(n_tied == 0.0, lambda: jnp.where(removed, 1.0, 0.0), tie_breaking_rounds)
    bias = jnp.where((forced | (taken > 0.0)) & valid, 0.0, NEG_INF).astype(BF16)

    for h in range(halves):
        bh = bias[h * SEL_BIAS_BLOCKS:(h + 1) * SEL_BIAS_BLOCKS]
        for n in range(NSA_HPG):
            qaug_ref[h, HEAD_DIM:HEAD_DIM + SEL_BIAS_BLOCKS, n * Q_BLOCK:(n + 1) * Q_BLOCK] = bh

    init = (jnp.full((1, cols), NEG_INF, F32), jnp.zeros((1, cols), F32),
            jnp.zeros((HEAD_DIM, cols), F32))

    n_win = WINDOW // Q_BLOCK + 1
    win_keys = n_win * Q_BLOCK
    w0 = pl.multiple_of(jnp.maximum(t0 - WINDOW, 0), Q_BLOCK)
    newest = tq - w0
    krow = lax.broadcasted_iota(jnp.int32, (win_keys, cols), 0)
    s_w = jnp.where(krow <= newest, _dot(kw_ref[0, pl.ds(w0, win_keys), :], q_t), NEG_INF)
    oldest = lax.broadcasted_iota(jnp.int32, (Q_BLOCK, cols), 0) > newest - WINDOW
    s_w = jnp.concatenate([jnp.where(oldest, s_w[:Q_BLOCK], NEG_INF), s_w[Q_BLOCK:]], axis=0)
    p_w = jnp.exp2(s_w - jnp.max(s_w, axis=0, keepdims=True))
    vw_t = jnp.concatenate([vw_ref[0, w0 // Q_BLOCK + w] for w in range(n_win)], axis=1)
    o_w = _dot(vw_t, p_w.astype(BF16)) * (1.0 / jnp.sum(p_w, axis=0, keepdims=True))

    def sel_scores(j, s_ref, diagonal=False):
        k0 = pl.multiple_of(j * tk, tk)
        e0 = pl.multiple_of(k0 % SEL_BIAS_KEYS, tk)
        kaug = jnp.concatenate([ks_ref[0, pl.ds(k0, tk), :], e_ref[pl.ds(e0, tk), :]], axis=1)
        s = _dot(kaug, qaug_ref[k0 // SEL_BIAS_KEYS])
        if diagonal:
            lk = lax.broadcasted_iota(jnp.int32, (tk, cols), 0)
            lq = lax.broadcasted_iota(jnp.int32, (tk, cols), 1) & (Q_BLOCK - 1)
            s = jnp.where(lk - lq <= t0 - k0, s, NEG_INF)
        s_ref[...] = s
        return jnp.max(s, axis=0, keepdims=True)

    def sel_update(s_ref, s_max, j, carry):
        m, l, acc = carry
        m_new = jnp.maximum(m, s_max)
        alpha = jnp.exp2(m - m_new)
        p = jnp.exp2(s_ref[...] - m_new)
        l = alpha * l + jnp.sum(p, axis=0, keepdims=True)
        acc = alpha * acc + _dot(vs_ref[0, j], p.astype(BF16))
        return m_new, l, acc

    n_full = t0 // tk
    last_full = jnp.maximum(n_full - 1, 0)
    max_d = sel_scores(n_full, sd_ref, diagonal=True)
    max_a = sel_scores(0, sa_ref)

    def sel_pair(j, carry):
        max_a, state = carry
        max_b = sel_scores(j + 1, sb_ref)
        state = sel_update(sa_ref, max_a, j, state)
        max_a = sel_scores(jnp.minimum(j + 2, last_full), sa_ref)
        return max_a, sel_update(sb_ref, max_b, j + 1, state)

    unroll = 4
    carry = lax.fori_loop(0, n_full // unroll,
                          lambda jj, c: sel_pair(unroll * jj + 2, sel_pair(unroll * jj, c)), (max_a, init))
    rest = n_full % unroll
    carry = lax.cond(rest >= 2, lambda c: sel_pair(n_full - rest, c), lambda c: c, carry)
    max_a, state = carry
    state = lax.cond(rest % 2 == 1, lambda c: sel_update(sa_ref, max_a, n_full - 1, c), lambda c: c, state)
    _, l_s, acc_s = sel_update(sd_ref, max_d, n_full, state)
    o_s = acc_s * (1.0 / l_s)

    g_t = gate_ref[...].T
    for n in range(NSA_HPG):
        c = slice(n * Q_BLOCK, (n + 1) * Q_BLOCK)
        o = (g_t[n:n + 1] * o_c[:, c] + g_t[NSA_HPG + n:NSA_HPG + n + 1] * o_s[:, c]
             + g_t[2 * NSA_HPG + n:2 * NSA_HPG + n + 1] * o_w[:, c])
        o_ref[:, n * HEAD_DIM:(n + 1) * HEAD_DIM] = o.T.astype(BF16)


def _nsa_attention(rotated, vs_t, vw_t, cmp, gates, onehot, batch, seq, *, tk):
    nq = seq // Q_BLOCK
    ns = seq // SEL_BLOCK
    ncp = seq // CMP_STRIDE
    halves = -(-ns // SEL_BIAS_BLOCKS)
    m = batch * seq
    once = pl.Buffered(1)

    def slab(first):
        return pl.BlockSpec((1, seq, HEAD_DIM), lambda b, g, qi: (first + g, b, 0), pipeline_mode=once)

    def slab_t(tile):
        return pl.BlockSpec((1, seq // tile, HEAD_DIM, tile), lambda b, g, qi: (g, b, 0, 0), pipeline_mode=once)

    def cmp_spec(which):
        return pl.BlockSpec((1, 1, 1, ncp // CMP_CHUNK, CMP_CHUNK, HEAD_DIM),
                            lambda b, g, qi: (which, b, g, 0, 0, 0))

    return pl.pallas_call(
        functools.partial(_attn_body, ns=ns, tk=tk),
        grid=(batch, NSA_KV_GROUPS, nq),
        in_specs=[
            pl.BlockSpec((NSA_HPG, Q_BLOCK, HEAD_DIM), lambda b, g, qi: (g, b * nq + qi, 0)),
            slab(ROT_KS), slab_t(tk), slab(ROT_KW), slab_t(Q_BLOCK),
            cmp_spec(0), cmp_spec(1),
            pl.BlockSpec((Q_BLOCK, LANES), lambda b, g, qi: (b * nq + qi, g)),
            pl.BlockSpec(onehot.shape, lambda b, g, qi: (0, 0), pipeline_mode=once),
        ],
        out_specs=pl.BlockSpec((Q_BLOCK, NSA_HPG * HEAD_DIM), lambda b, g, qi: (b * nq + qi, g)),
        out_shape=jax.ShapeDtypeStruct((m, NSA_HEADS * HEAD_DIM), BF16),
        scratch_shapes=[
            pltpu.VMEM((halves, 2 * HEAD_DIM, NSA_HPG * Q_BLOCK), BF16),
            pltpu.VMEM((ncp, Q_BLOCK), F32),
            pltpu.VMEM((tk, NSA_HPG * Q_BLOCK), F32),
            pltpu.VMEM((tk, NSA_HPG * Q_BLOCK), F32),
            pltpu.VMEM((tk, NSA_HPG * Q_BLOCK), F32),
        ],
        compiler_params=_params(("parallel", "parallel", "arbitrary")),
        name="nsa_attention",
    )(rotated, rotated, vs_t, rotated, vw_t, cmp, cmp, gates, onehot)


def _sel_onehot(seq):
    keys = np.arange(min(seq, SEL_BIAS_KEYS))
    onehot = (keys[:, None] // SEL_BLOCK == np.arange(SEL_BIAS_BLOCKS)[None, :]).astype(np.float32)
    return jnp.asarray(onehot, BF16)


def _gate_weight(w_gl):
    d = w_gl.shape[0]
    w = w_gl.reshape(d, 3, NSA_KV_GROUPS, NSA_HPG).transpose(0, 2, 1, 3).reshape(d, NSA_KV_GROUPS, 3 * NSA_HPG)
    w = jnp.pad(w, ((0, 0), (0, 0), (0, LANES - 3 * NSA_HPG)))
    return w.reshape(d, NSA_KV_GROUPS * LANES)


def _nsa_layer(x, h, w_in, kc_pe, kc_w1, kc_w2, vc_pe, vc_w1, vc_w2, w_out, batch, seq, *, tk=1024):
    assert SEL_BLOCK == 4 * CMP_STRIDE and CMP_LEN == 2 * CMP_STRIDE and seq % max(tk, CMP_STRIDE * CMP_CHUNK) == 0
    n_main = NSA_HEADS * HEAD_DIM + 6 * NSA_KV_GROUPS * HEAD_DIM
    rotated, kr, (vs_t, vw_t) = _nsa_projections(h, w_in, _rope_tables(jnp.arange(seq)), seq, tm=512, sel_tile=tk)
    gates = _mm_sigmoid(h, _gate_weight(w_in[:, n_main:]).astype(BF16), tm=1024)
    rows = seq // CMP_STRIDE
    cmp_tables = _rope_tables(jnp.arange(rows) * CMP_STRIDE + CMP_LEN - 1)
    half = CMP_LEN // 2
    pe = jnp.stack([kc_pe, vc_pe]).reshape(2, 2, 1, half * HEAD_DIM)
    w1 = jnp.stack([kc_w1, vc_w1]).astype(BF16)
    w1 = w1.reshape(2, 2, half * HEAD_DIM, w1.shape[-1])
    w2 = jnp.stack([kc_w2, vc_w2]).astype(BF16)
    cmp = _compress(kr, pe, w1, w2, cmp_tables, batch, seq)
    att = _nsa_attention(rotated, vs_t, vw_t, cmp, gates, _sel_onehot(seq), batch, seq, tk=tk)
    return _mm_residual(att, w_out.astype(BF16), x, tm=512, tn=w_out.shape[1])


def kernel(x, p, norm_mix, norm_ffn, norm_ple, ffn_up, ffn_down, ple_proj, ple_gate, gm_in, gm_ln_g, gm_ln_b, gm_ws, gm_bs, gm_out, nsa_in, nsa_kc_pe, nsa_kc_w1, nsa_kc_w2, nsa_vc_pe, nsa_vc_w1, nsa_vc_w2, nsa_out, final_norm):
    batch, seq, d = x.shape
    m = batch * seq
    depth = p.shape[0]
    xf = x.reshape(m, d)
    row = lambda v: v.reshape(1, -1)
    for i in range(depth):
        j = i // 2
        if i % 2 == 0:
            z = _norm_mm(xf, row(norm_mix[i]), gm_in[j].astype(BF16), act="gelu", out_dtype=BF16,
                         tm=512, tn=gm_in.shape[-1])
            xf = _gmlp_gate_out(z, xf, row(gm_ln_g[j]), row(gm_ln_b[j]), gm_ws[j], gm_bs[j].T,
                                gm_out[j].astype(BF16), tm=512)
        else:
            xf = _nsa_layer(xf, h_mix, nsa_in[j], nsa_kc_pe[j], nsa_kc_w1[j], nsa_kc_w2[j],
                            nsa_vc_pe[j], nsa_vc_w1[j], nsa_vc_w2[j], nsa_out[j], batch, seq)
        xf = _ffn(xf, row(norm_ffn[i]), ffn_up[i].astype(BF16), ffn_down[i].astype(BF16), tm=1024, tf=512)
        if i == depth - 1:
            post, post_g = "final", final_norm
        elif (i + 1) % 2 == 1:
            post, post_g = "next", norm_mix[i + 1]
        else:
            post, post_g = "none", final_norm
        out = _ple(xf, p[i].reshape(m, -1), row(norm_ple[i]), ple_gate[i].astype(BF16),
                   ple_proj[i].astype(BF16), row(post_g), post=post, tm=256)
        xf, h_mix = out if post == "next" else (out, None)
    return xf.reshape(batch, seq, d)
```

```python
import functools

import numpy as np
import jax
import jax.numpy as jnp
from jax import lax
from jax.experimental import pallas as pl
from jax.experimental.pallas import tpu as pltpu

F32 = jnp.float32
BF16 = jnp.bfloat16

EPS = 1e-6
HEAD_DIM = 128
NSA_HEADS = 16
NSA_KV_GROUPS = 4
NSA_HPG = NSA_HEADS // NSA_KV_GROUPS
ROT_DIM = HEAD_DIM // 4
ROPE_THETA = 500000.0
CMP_LEN = 32
CMP_STRIDE = 16
SEL_BLOCK = 64
SEL_TOPK = 16
WINDOW = 512
Q_BLOCK = 128
GM_CHUNK = 128
NEG_INF = -1e30
LOG2_E = 1.4426950408889634

LANES = 128
SEL_BIAS_BLOCKS = LANES
SEL_BIAS_KEYS = SEL_BIAS_BLOCKS * SEL_BLOCK
VMEM_LIMIT = 56 * 1024 * 1024


def _params(sem):
    return pltpu.CompilerParams(dimension_semantics=sem, vmem_limit_bytes=VMEM_LIMIT)


def _rmsnorm(x, g):
    return x * lax.rsqrt(jnp.mean(x * x, axis=-1, keepdims=True) + EPS) * g


def _dot(a, b):
    return jnp.dot(a, b, preferred_element_type=F32)


def _rope(x, c, sa, sb):
    return x * c + pltpu.roll(x, LANES - ROT_DIM // 2, 1) * sa + pltpu.roll(x, ROT_DIM // 2, 1) * sb


def _rope_tables(pos):
    half = ROT_DIM // 2
    inv = jnp.power(jnp.float32(ROPE_THETA), -jnp.arange(half, dtype=F32) * 2.0 / ROT_DIM)
    ang = pos.astype(F32)[:, None] * inv[None, :]
    cos, sin = jnp.cos(ang), jnp.sin(ang)
    n = pos.shape[0]
    rest = HEAD_DIM - ROT_DIM
    c = jnp.concatenate([cos, cos, jnp.ones((n, rest), F32)], axis=1)
    sa = jnp.concatenate([-sin, jnp.zeros((n, half + rest), F32)], axis=1)
    sb = jnp.concatenate([jnp.zeros((n, half), F32), sin, jnp.zeros((n, rest), F32)], axis=1)
    return c, sa, sb


def _norm_mm_body(x_ref, g_ref, w_ref, o_ref, h_ref, *, act):
    @pl.when(pl.program_id(1) == 0)
    def _():
        h_ref[...] = _rmsnorm(x_ref[...], g_ref[...]).astype(BF16)

    acc = _dot(h_ref[...], w_ref[...])
    if act == "gelu":
        acc = jax.nn.gelu(acc)
    elif act == "sigmoid":
        acc = jax.nn.sigmoid(acc)
    o_ref[...] = acc.astype(o_ref.dtype)


def _norm_mm(x, g, w, *, act, out_dtype, tm, tn):
    m, d = x.shape
    n = w.shape[1]
    return pl.pallas_call(
        functools.partial(_norm_mm_body, act=act),
        grid=(m // tm, n // tn),
        in_specs=[
            pl.BlockSpec((tm, d), lambda i, j: (i, 0)),
            pl.BlockSpec((1, d), lambda i, j: (0, 0)),
            pl.BlockSpec((d, tn), lambda i, j: (0, j)),
        ],
        out_specs=pl.BlockSpec((tm, tn), lambda i, j: (i, j)),
        out_shape=jax.ShapeDtypeStruct((m, n), out_dtype),
        scratch_shapes=[pltpu.VMEM((tm, d), BF16)],
        compiler_params=_params(("parallel", "arbitrary")),
        name="norm_mm_" + str(act),
    )(x, g, w)


def _mm_sigmoid_body(h_ref, w_ref, o_ref):
    o_ref[...] = jax.nn.sigmoid(_dot(h_ref[...], w_ref[...]))


def _mm_sigmoid(h, w, *, tm):
    m, d = h.shape
    n = w.shape[1]
    return pl.pallas_call(
        _mm_sigmoid_body,
        grid=(m // tm,),
        in_specs=[pl.BlockSpec((tm, d), lambda i: (i, 0)), pl.BlockSpec((d, n), lambda i: (0, 0))],
        out_specs=pl.BlockSpec((tm, n), lambda i: (i, 0)),
        out_shape=jax.ShapeDtypeStruct((m, n), F32),
        compiler_params=_params(("parallel",)),
        name="mm_sigmoid",
    )(h, w)


def _gmlp_body(z_ref, x_ref, lg_ref, lb_ref, ws_ref, bs_ref, wo_ref, o_ref, y_ref, *, tm, width):
    groups = ws_ref.shape[0]
    gd = width // groups
    u = z_ref[:, :width]
    v = z_ref[:, width:].astype(F32)
    mu = jnp.mean(v, axis=-1, keepdims=True)
    var = jnp.mean(jnp.square(v - mu), axis=-1, keepdims=True)
    vn = ((v - mu) * lax.rsqrt(var + EPS) * lg_ref[...] + lb_ref[...]).astype(BF16)
    r = lax.broadcasted_iota(jnp.int32, (GM_CHUNK, GM_CHUNK), 0)
    c = lax.broadcasted_iota(jnp.int32, (GM_CHUNK, GM_CHUNK), 1)
    causal = c <= r
    for g in range(groups):
        wg = jnp.where(causal, ws_ref[g], 0.0).astype(BF16)
        bg = bs_ref[:, g:g + 1]
        for ch in range(tm // GM_CHUNK):
            rows = slice(ch * GM_CHUNK, (ch + 1) * GM_CHUNK)
            cols = slice(g * gd, (g + 1) * gd)
            sv = _dot(wg, vn[rows, cols]) + bg
            y_ref[rows, cols] = (u[rows, cols].astype(F32) * sv).astype(BF16)
    o_ref[...] = x_ref[...] + _dot(y_ref[...], wo_ref[...])


def _gmlp_gate_out(z, x, ln_g, ln_b, ws, bs, w_out, *, tm):
    m, d = x.shape
    width = z.shape[1] // 2
    groups = ws.shape[0]
    return pl.pallas_call(
        functools.partial(_gmlp_body, tm=tm, width=width),
        grid=(m // tm,),
        in_specs=[
            pl.BlockSpec((tm, 2 * width), lambda i: (i, 0)),
            pl.BlockSpec((tm, d), lambda i: (i, 0)),
            pl.BlockSpec((1, width), lambda i: (0, 0)),
            pl.BlockSpec((1, width), lambda i: (0, 0)),
            pl.BlockSpec((groups, GM_CHUNK, GM_CHUNK), lambda i: (0, 0, 0)),
            pl.BlockSpec((GM_CHUNK, groups), lambda i: (0, 0)),
            pl.BlockSpec((width, d), lambda i: (0, 0)),
        ],
        out_specs=pl.BlockSpec((tm, d), lambda i: (i, 0)),
        out_shape=jax.ShapeDtypeStruct((m, d), F32),
        scratch_shapes=[pltpu.VMEM((tm, width), BF16)],
        compiler_params=_params(("parallel",)),
        name="gmlp_gate_out",
    )(z, x, ln_g, ln_b, ws, bs, w_out)


def _ffn_body(x_ref, g_ref, wu_ref, wd_ref, o_ref, h_ref):
    f = pl.program_id(1)

    @pl.when(f == 0)
    def _():
        x = x_ref[...]
        h_ref[...] = _rmsnorm(x, g_ref[...]).astype(BF16)
        o_ref[...] = x

    a = jnp.square(jnp.maximum(_dot(h_ref[...], wu_ref[...]), 0.0)).astype(BF16)
    o_ref[...] += _dot(a, wd_ref[...])


def _ffn(x, g, w_up, w_down, *, tm, tf):
    m, d = x.shape
    ff = w_up.shape[1]
    return pl.pallas_call(
        _ffn_body,
        grid=(m // tm, ff // tf),
        in_specs=[
            pl.BlockSpec((tm, d), lambda i, f: (i, 0)),
            pl.BlockSpec((1, d), lambda i, f: (0, 0)),
            pl.BlockSpec((d, tf), lambda i, f: (0, f)),
            pl.BlockSpec((tf, d), lambda i, f: (f, 0)),
        ],
        out_specs=pl.BlockSpec((tm, d), lambda i, f: (i, 0)),
        out_shape=jax.ShapeDtypeStruct((m, d), F32),
        scratch_shapes=[pltpu.VMEM((tm, d), BF16)],
        compiler_params=_params(("parallel", "arbitrary")),
        name="ffn",
    )(x, g, w_up, w_down)


def _ple_body(x_ref, p_ref, g_ref, wg_ref, wp_ref, pg_ref, *o_refs, post):
    x = x_ref[...]
    h = _rmsnorm(x, g_ref[...]).astype(BF16)
    gate = jax.nn.sigmoid(_dot(h, wg_ref[...]))
    y = x + gate * _dot(p_ref[...].astype(BF16), wp_ref[...])
    if post == "final":
        o_refs[0][...] = _rmsnorm(y, pg_ref[...])
    else:
        o_refs[0][...] = y
    if post == "next":
        o_refs[1][...] = _rmsnorm(y, pg_ref[...]).astype(BF16)


def _ple(x, p, g, w_gate, w_proj, post_g, *, post, tm):
    m, d = x.shape
    pd = p.shape[1]
    row_spec = pl.BlockSpec((tm, d), lambda i: (i, 0))
    out_specs, out_shape = row_spec, jax.ShapeDtypeStruct((m, d), F32)
    if post == "next":
        out_specs, out_shape = [row_spec, row_spec], [out_shape, jax.ShapeDtypeStruct((m, d), BF16)]
    return pl.pallas_call(
        functools.partial(_ple_body, post=post),
        grid=(m // tm,),
        in_specs=[
            row_spec,
            pl.BlockSpec((tm, pd), lambda i: (i, 0)),
            pl.BlockSpec((1, d), lambda i: (0, 0)),
            pl.BlockSpec((d, d), lambda i: (0, 0)),
            pl.BlockSpec((pd, d), lambda i: (0, 0)),
            pl.BlockSpec((1, d), lambda i: (0, 0)),
        ],
        out_specs=out_specs,
        out_shape=out_shape,
        compiler_params=_params(("parallel",)),
        name="ple_" + post,
    )(x, p, g, w_gate, w_proj, post_g)


def _mm_res_body(a_ref, w_ref, x_ref, o_ref):
    o_ref[...] = x_ref[...] + _dot(a_ref[...], w_ref[...])


def _mm_residual(a, w, x, *, tm, tn):
    m, k = a.shape
    n = w.shape[1]
    return pl.pallas_call(
        _mm_res_body,
        grid=(m // tm, n // tn),
        in_specs=[
            pl.BlockSpec((tm, k), lambda i, j: (i, 0)),
            pl.BlockSpec((k, tn), lambda i, j: (0, j)),
            pl.BlockSpec((tm, tn), lambda i, j: (i, j)),
        ],
        out_specs=pl.BlockSpec((tm, tn), lambda i, j: (i, j)),
        out_shape=jax.ShapeDtypeStruct((m, n), F32),
        compiler_params=_params(("parallel", "arbitrary")),
        name="mm_residual",
    )(a, w, x)


ROT_Q, ROT_KS, ROT_KW = 0, NSA_HEADS, NSA_HEADS + NSA_KV_GROUPS


def _proj_rotated_body(h_ref, w_ref, c_ref, sa_ref, sb_ref, o_ref):
    acc = _dot(h_ref[...], w_ref[...])
    c, sa, sb = c_ref[...], sa_ref[...], sb_ref[...]
    for hh in range(o_ref.shape[0]):
        seg = _rope(acc[:, hh * HEAD_DIM:(hh + 1) * HEAD_DIM], c, sa, sb)
        if hh < ROT_KS:
            seg = seg * (HEAD_DIM ** -0.5 * LOG2_E)
        o_ref[hh] = seg.astype(BF16)


def _proj_grouped_body(h_ref, w_ref, o_ref, rows_ref):
    acc = _dot(h_ref[...], w_ref[...])
    groups = acc.shape[0] // CMP_STRIDE
    for hh in range(o_ref.shape[0]):
        rows_ref[hh] = acc[:, hh * HEAD_DIM:(hh + 1) * HEAD_DIM]
        o_ref[hh] = jnp.concatenate([rows_ref[hh, pl.ds(l, groups, stride=CMP_STRIDE), :]
                                     for l in range(CMP_STRIDE)], axis=1).astype(BF16)


def _proj_flipped_body(h_ref, w_ref, o_ref):
    acc = _dot(h_ref[...], w_ref[...])
    tile = o_ref.shape[-1]
    for hh in range(o_ref.shape[0]):
        for t in range(acc.shape[0] // tile):
            o_ref[hh, t] = acc[t * tile:(t + 1) * tile, hh * HEAD_DIM:(hh + 1) * HEAD_DIM].T.astype(BF16)


def _proj(body, h, w, extra, extra_specs, out_block, out_dims, scratch, name, *, tm):
    m, d = h.shape
    n = w.shape[1]
    heads = n // HEAD_DIM
    return pl.pallas_call(
        body,
        grid=(m // tm,),
        in_specs=[pl.BlockSpec((tm, d), lambda i: (i, 0)), pl.BlockSpec((d, n), lambda i: (0, 0))] + extra_specs,
        out_specs=pl.BlockSpec((heads,) + out_block, lambda i: (0, i) + (0,) * (len(out_block) - 1)),
        out_shape=jax.ShapeDtypeStruct((heads,) + out_dims, BF16),
        scratch_shapes=scratch,
        compiler_params=_params(("parallel",)),
        name=name,
    )(h, w, *extra)


def _nsa_projections(h, w_in, tables, seq, *, tm, sel_tile):
    m = h.shape[0]
    hd = HEAD_DIM
    q_end = NSA_HEADS * hd
    kv = NSA_KV_GROUPS * hd
    kc, vc, ks, vs, kw, vw = (slice(q_end + i * kv, q_end + (i + 1) * kv) for i in range(6))
    cast = lambda cols: jnp.concatenate([w_in[:, c] for c in cols], axis=1).astype(BF16)
    tpb = seq // tm
    tab_spec = pl.BlockSpec((tm, hd), lambda i: (i % tpb, 0))
    rotated = _proj(_proj_rotated_body, h, cast([slice(0, q_end), ks, kw]), tables, [tab_spec] * 3,
                    (tm, hd), (m, hd), [], "nsa_proj_rotated", tm=tm)
    grouped = _proj(_proj_grouped_body, h, cast([kc, vc]), (), [],
                    (tm // CMP_STRIDE, CMP_STRIDE * hd), (m // CMP_STRIDE, CMP_STRIDE * hd),
                    [pltpu.VMEM((2 * NSA_KV_GROUPS, tm, hd), F32)], "nsa_proj_grouped", tm=tm)
    flipped = [_proj(_proj_flipped_body, h, cast([cols]), (), [], (tile_rows // tile, hd, tile),
                     (m // tile, hd, tile), [], "nsa_proj_flipped", tm=tile_rows)
               for cols, tile, tile_rows in ((vs, sel_tile, max(tm, sel_tile)), (vw, Q_BLOCK, tm))]
    return rotated, grouped, flipped


CMP_CHUNK = LANES


def _compress_body(kr_ref, pe_ref, w1_ref, w2_ref, c_ref, sa_ref, sb_ref, o_ref):
    kr = kr_ref[0].astype(F32)
    rows = kr.shape[0]
    a = _dot((kr + pe_ref[0, 0]).astype(BF16), w1_ref[0, 0])
    b = _dot((kr + pe_ref[0, 1]).astype(BF16), w1_ref[0, 1])
    hid = jax.nn.gelu(a + pltpu.roll(b, rows - 1, 0))
    out = _dot(hid.astype(BF16), w2_ref[0])
    is_k = pl.program_id(0) == 0

    @pl.when(is_k)
    def _():
        roped = _rope(out, c_ref[...], sa_ref[...], sb_ref[...]).astype(BF16)
        for j in range(rows // CMP_CHUNK):
            o_ref[0, 0, 0, j] = roped[j * CMP_CHUNK:(j + 1) * CMP_CHUNK]

    @pl.when(jnp.logical_not(is_k))
    def _():
        for j in range(rows // CMP_CHUNK):
            o_ref[0, 0, 0, j] = out[j * CMP_CHUNK:(j + 1) * CMP_CHUNK].T.astype(BF16)


def _compress(kr, pe, w1, w2, tables, batch, seq):
    rows = seq // CMP_STRIDE
    kdim = CMP_STRIDE * HEAD_DIM
    hid = w1.shape[-1]
    tab_spec = pl.BlockSpec((rows, HEAD_DIM), lambda w, b, g: (0, 0))
    chunks = rows // CMP_CHUNK
    return pl.pallas_call(
        _compress_body,
        grid=(2, batch, NSA_KV_GROUPS),
        in_specs=[
            pl.BlockSpec((1, rows, kdim), lambda w, b, g: (NSA_KV_GROUPS * w + g, b, 0)),
            pl.BlockSpec((1, 2, 1, kdim), lambda w, b, g: (w, 0, 0, 0)),
            pl.BlockSpec((1, 2, kdim, hid), lambda w, b, g: (w, 0, 0, 0)),
            pl.BlockSpec((1, hid, HEAD_DIM), lambda w, b, g: (w, 0, 0)),
            tab_spec, tab_spec, tab_spec,
        ],
        out_specs=pl.BlockSpec((1, 1, 1, chunks, CMP_CHUNK, HEAD_DIM), lambda w, b, g: (w, b, g, 0, 0, 0)),
        out_shape=jax.ShapeDtypeStruct((2, batch, NSA_KV_GROUPS, chunks, CMP_CHUNK, HEAD_DIM), BF16),
        compiler_params=_params(("parallel", "parallel", "parallel")),
        name="nsa_compress",
    )(kr, pe, w1, w2, *tables)


def _attn_body(q_ref, ks_ref, vs_ref, kw_ref, vw_ref, kc_ref, vc_ref, gate_ref, e_ref,
               o_ref, qaug_ref, ps_ref, sa_ref, sb_ref, sd_ref, *, ns, tk):
    qi = pl.program_id(2)
    t0 = qi * Q_BLOCK
    cols = NSA_HPG * Q_BLOCK
    halves = qaug_ref.shape[0]
    for n in range(NSA_HPG):
        qt = q_ref[n].astype(F32).T.astype(BF16)
        for h in range(halves):
            qaug_ref[h, 0:HEAD_DIM, n * Q_BLOCK:(n + 1) * Q_BLOCK] = qt
    q_t = qaug_ref[0, 0:HEAD_DIM, :]
    tq = t0 + (lax.broadcasted_iota(jnp.int32, (1, cols), 1) & (Q_BLOCK - 1))

    ncp = ps_ref.shape[0]
    step = min(2 * CMP_CHUNK, ncp)
    last_block = (tq - (CMP_LEN - 1)) // CMP_STRIDE

    def compressed(rows):
        chunks = range(rows // CMP_CHUNK)
        s = _dot(jnp.concatenate([kc_ref[0, 0, 0, i] for i in chunks], axis=0), q_t)
        s = jnp.where(lax.broadcasted_iota(jnp.int32, (rows, cols), 0) <= last_block, s, NEG_INF)
        p = jnp.exp2(s - jnp.max(s, axis=0, keepdims=True))
        acc = _dot(jnp.concatenate([vc_ref[0, 0, 0, i] for i in chunks], axis=1), p.astype(BF16))
        inv = jnp.where(tq >= CMP_LEN - 1, 1.0 / jnp.sum(p, axis=0, keepdims=True), 0.0)
        p = p * inv
        ps = p[:, 0:Q_BLOCK]
        for n in range(1, NSA_HPG):
            ps = ps + p[:, n * Q_BLOCK:(n + 1) * Q_BLOCK]
        ps_ref[0:rows, :] = ps
        if rows < ncp:
            ps_ref[rows:ncp, :] = jnp.zeros((ncp - rows, Q_BLOCK), F32)
        return acc * inv

    variants = [functools.partial(compressed, rows) for rows in range(step, ncp + 1, step)]
    reach = jnp.minimum((t0 + Q_BLOCK - CMP_LEN) // (CMP_STRIDE * step), len(variants) - 1)
    o_c = lax.switch(reach, variants) if len(variants) > 1 else variants[0]()

    ratio = SEL_BLOCK // CMP_STRIDE
    r = [ps_ref[pl.ds(j, ns, stride=ratio), :] for j in range(ratio)]
    prev = pltpu.roll(r[ratio - 1], 1, 0)
    prev = jnp.where(lax.broadcasted_iota(jnp.int32, (ns, Q_BLOCK), 0) == 0, 0.0, prev)
    imp = 2.0 * (r[0] + r[1] + r[2]) + r[3] + prev
    nsp = halves * SEL_BIAS_BLOCKS
    if nsp > ns:
        imp = jnp.concatenate([imp, jnp.zeros((nsp - ns, Q_BLOCK), F32)], axis=0)

    blk = lax.broadcasted_iota(jnp.int32, (nsp, Q_BLOCK), 0)
    blk_f = blk.astype(F32)
    cur = (t0 + lax.broadcasted_iota(jnp.int32, (nsp, Q_BLOCK), 1)) // SEL_BLOCK
    valid = blk <= cur
    forced = (blk == 0) | (blk == cur) | (blk == cur - 1)
    n_forced = 3
    rounds = min(SEL_TOPK, ns) - n_forced
    free = valid & jnp.logical_not(forced)
    score0 = jnp.where(free, imp, NEG_INF)

    score = score0
    for _ in range(rounds):
        score = jnp.where(score == jnp.max(score, axis=0, keepdims=True), -jnp.inf, score)
    removed = score == -jnp.inf
    n_removed = jnp.sum(jnp.where(removed & free, 1.0, 0.0), axis=0, keepdims=True)
    cur_row = (t0 + lax.broadcasted_iota(jnp.int32, (1, Q_BLOCK), 1)) // SEL_BLOCK
    n_free = jnp.clip(cur_row + 1 - n_forced, 0, rounds).astype(F32)
    n_tied = jnp.sum(jnp.where(n_removed == n_free, 0.0, 1.0))

    def tie_breaking_rounds():
        score, taken = score0, jnp.zeros((nsp, Q_BLOCK), F32)
        for _ in range(rounds):
            top = jnp.max(score, axis=0, keepdims=True)
            first = jnp.min(jnp.where(score == top, blk_f, float(nsp)), axis=0, keepdims=True)
            hit = blk_f == first
            taken = jnp.where(hit, 1.0, taken)
            score = jnp.where(hit, -jnp.inf, score)
        return taken

    taken = lax.cond(n_tied == 0.0, lambda: jnp.where(removed, 1.0, 0.0), tie_breaking_rounds)
    bias = jnp.where((forced | (taken > 0.0)) & valid, 0.0, NEG_INF).astype(BF16)

    for h in range(halves):
        bh = bias[h * SEL_BIAS_BLOCKS:(h + 1) * SEL_BIAS_BLOCKS]
        for n in range(NSA_HPG):
            qaug_ref[h, HEAD_DIM:HEAD_DIM + SEL_BIAS_BLOCKS, n * Q_BLOCK:(n + 1) * Q_BLOCK] = bh

    init = (jnp.full((1, cols), NEG_INF, F32), jnp.zeros((1, cols), F32),
            jnp.zeros((HEAD_DIM, cols), F32))

    n_win = WINDOW // Q_BLOCK + 1
    win_keys = n_win * Q_BLOCK
    w0 = pl.multiple_of(jnp.maximum(t0 - WINDOW, 0), Q_BLOCK)
    newest = tq - w0
    krow = lax.broadcasted_iota(jnp.int32, (win_keys, cols), 0)
    s_w = jnp.where(krow <= newest, _dot(kw_ref[0, pl.ds(w0, win_keys), :], q_t), NEG_INF)
    oldest = lax.broadcasted_iota(jnp.int32, (Q_BLOCK, cols), 0) > newest - WINDOW
    s_w = jnp.concatenate([jnp.where(oldest, s_w[:Q_BLOCK], NEG_INF), s_w[Q_BLOCK:]], axis=0)
    p_w = jnp.exp2(s_w - jnp.max(s_w, axis=0, keepdims=True))
    vw_t = jnp.concatenate([vw_ref[0, w0 // Q_BLOCK + w] for w in range(n_win)], axis=1)
    o_w = _dot(vw_t, p_w.astype(BF16)) * (1.0 / jnp.sum(p_w, axis=0, keepdims=True))

    def sel_scores(j, s_ref, rows=tk, diagonal=False):
        k0 = pl.multiple_of(j * tk, tk)
        e0 = pl.multiple_of(k0 % SEL_BIAS_KEYS, tk)
        kaug = jnp.concatenate([ks_ref[0, pl.ds(k0, rows), :], e_ref[pl.ds(e0, rows), :]], axis=1)
        s = _dot(kaug, qaug_ref[k0 // SEL_BIAS_KEYS])
        if diagonal:
            lk = lax.broadcasted_iota(jnp.int32, (rows, cols), 0)
            lq = lax.broadcasted_iota(jnp.int32, (rows, cols), 1) & (Q_BLOCK - 1)
            s = jnp.where(lk - lq <= t0 - k0, s, NEG_INF)
        s_ref[0:rows, :] = s
        return jnp.max(s, axis=0, keepdims=True)

    def sel_update(s_ref, s_max, j, carry, rows=tk):
        m, l, acc = carry
        m_new = jnp.maximum(m, s_max)
        alpha = jnp.exp2(m - m_new)
        p = jnp.exp2(s_ref[0:rows, :] - m_new)
        l = alpha * l + jnp.sum(p, axis=0, keepdims=True)
        acc = alpha * acc + _dot(vs_ref[0, j, :, 0:rows], p.astype(BF16))
        return m_new, l, acc

    n_full = t0 // tk
    last_full = jnp.maximum(n_full - 1, 0)
    diag_step = min(tk, 2 * Q_BLOCK)
    diag_rows = list(range(diag_step, tk + 1, diag_step))
    diag_kind = (t0 - n_full * tk + Q_BLOCK - 1) // diag_step
    max_d = lax.switch(diag_kind, [functools.partial(sel_scores, n_full, sd_ref, rows, True) for rows in diag_rows])
    max_a = sel_scores(0, sa_ref)

    def sel_pair(j, carry):
        max_a, state = carry
        max_b = sel_scores(j + 1, sb_ref)
        state = sel_update(sa_ref, max_a, j, state)
        max_a = sel_scores(jnp.minimum(j + 2, last_full), sa_ref)
        return max_a, sel_update(sb_ref, max_b, j + 1, state)

    unroll = 4
    carry = lax.fori_loop(0, n_full // unroll,
                          lambda jj, c: sel_pair(unroll * jj + 2, sel_pair(unroll * jj, c)), (max_a, init))
    rest = n_full % unroll
    carry = lax.cond(rest >= 2, lambda c: sel_pair(n_full - rest, c), lambda c: c, carry)
    max_a, state = carry
    state = lax.cond(rest % 2 == 1, lambda c: sel_update(sa_ref, max_a, n_full - 1, c), lambda c: c, state)
    _, l_s, acc_s = lax.switch(diag_kind, [functools.partial(sel_update, sd_ref, max_d, n_full, rows=rows)
                                           for rows in diag_rows], state)
    o_s = acc_s * (1.0 / l_s)

    g_t = gate_ref[...].T
    for n in range(NSA_HPG):
        c = slice(n * Q_BLOCK, (n + 1) * Q_BLOCK)
        o = (g_t[n:n + 1] * o_c[:, c] + g_t[NSA_HPG + n:NSA_HPG + n + 1] * o_s[:, c]
             + g_t[2 * NSA_HPG + n:2 * NSA_HPG + n + 1] * o_w[:, c])
        o_ref[:, n * HEAD_DIM:(n + 1) * HEAD_DIM] = o.T.astype(BF16)


def _nsa_attention(rotated, vs_t, vw_t, cmp, gates, onehot, batch, seq, *, tk):
    nq = seq // Q_BLOCK
    ns = seq // SEL_BLOCK
    ncp = seq // CMP_STRIDE
    halves = -(-ns // SEL_BIAS_BLOCKS)
    m = batch * seq
    once = pl.Buffered(1)

    def slab(first):
        return pl.BlockSpec((1, seq, HEAD_DIM), lambda b, g, qi: (first + g, b, 0), pipeline_mode=once)

    def slab_t(tile):
        return pl.BlockSpec((1, seq // tile, HEAD_DIM, tile), lambda b, g, qi: (g, b, 0, 0), pipeline_mode=once)

    def cmp_spec(which):
        return pl.BlockSpec((1, 1, 1, ncp // CMP_CHUNK, CMP_CHUNK, HEAD_DIM),
                            lambda b, g, qi: (which, b, g, 0, 0, 0))

    return pl.pallas_call(
        functools.partial(_attn_body, ns=ns, tk=tk),
        grid=(batch, NSA_KV_GROUPS, nq),
        in_specs=[
            pl.BlockSpec((NSA_HPG, Q_BLOCK, HEAD_DIM), lambda b, g, qi: (g, b * nq + qi, 0)),
            slab(ROT_KS), slab_t(tk), slab(ROT_KW), slab_t(Q_BLOCK),
            cmp_spec(0), cmp_spec(1),
            pl.BlockSpec((Q_BLOCK, LANES), lambda b, g, qi: (b * nq + qi, g)),
            pl.BlockSpec(onehot.shape, lambda b, g, qi: (0, 0), pipeline_mode=once),
        ],
        out_specs=pl.BlockSpec((Q_BLOCK, NSA_HPG * HEAD_DIM), lambda b, g, qi: (b * nq + qi, g)),
        out_shape=jax.ShapeDtypeStruct((m, NSA_HEADS * HEAD_DIM), BF16),
        scratch_shapes=[
            pltpu.VMEM((halves, 2 * HEAD_DIM, NSA_HPG * Q_BLOCK), BF16),
            pltpu.VMEM((ncp, Q_BLOCK), F32),
            pltpu.VMEM((tk, NSA_HPG * Q_BLOCK), F32),
            pltpu.VMEM((tk, NSA_HPG * Q_BLOCK), F32),
            pltpu.VMEM((tk, NSA_HPG * Q_BLOCK), F32),
        ],
        compiler_params=_params(("parallel", "parallel", "arbitrary")),
        name="nsa_attention",
    )(rotated, rotated, vs_t, rotated, vw_t, cmp, cmp, gates, onehot)


def _sel_onehot(seq):
    keys = np.arange(min(seq, SEL_BIAS_KEYS))
    onehot = (keys[:, None] // SEL_BLOCK == np.arange(SEL_BIAS_BLOCKS)[None, :]).astype(np.float32)
    return jnp.asarray(onehot, BF16)


def _gate_weight(w_gl):
    d = w_gl.shape[0]
    w = w_gl.reshape(d, 3, NSA_KV_GROUPS, NSA_HPG).transpose(0, 2, 1, 3).reshape(d, NSA_KV_GROUPS, 3 * NSA_HPG)
    w = jnp.pad(w, ((0, 0), (0, 0), (0, LANES - 3 * NSA_HPG)))
    return w.reshape(d, NSA_KV_GROUPS * LANES)


def _nsa_layer(x, h, w_in, kc_pe, kc_w1, kc_w2, vc_pe, vc_w1, vc_w2, w_out, batch, seq, *, tk=1024):
    assert SEL_BLOCK == 4 * CMP_STRIDE and CMP_LEN == 2 * CMP_STRIDE and seq % max(tk, CMP_STRIDE * CMP_CHUNK) == 0
    n_main = NSA_HEADS * HEAD_DIM + 6 * NSA_KV_GROUPS * HEAD_DIM
    rotated, kr, (vs_t, vw_t) = _nsa_projections(h, w_in, _rope_tables(jnp.arange(seq)), seq, tm=512, sel_tile=tk)
    gates = _mm_sigmoid(h, _gate_weight(w_in[:, n_main:]).astype(BF16), tm=1024)
    rows = seq // CMP_STRIDE
    cmp_tables = _rope_tables(jnp.arange(rows) * CMP_STRIDE + CMP_LEN - 1)
    half = CMP_LEN // 2
    pe = jnp.stack([kc_pe, vc_pe]).reshape(2, 2, 1, half * HEAD_DIM)
    w1 = jnp.stack([kc_w1, vc_w1]).astype(BF16)
    w1 = w1.reshape(2, 2, half * HEAD_DIM, w1.shape[-1])
    w2 = jnp.stack([kc_w2, vc_w2]).astype(BF16)
    cmp = _compress(kr, pe, w1, w2, cmp_tables, batch, seq)
    att = _nsa_attention(rotated, vs_t, vw_t, cmp, gates, _sel_onehot(seq), batch, seq, tk=tk)
    return _mm_residual(att, w_out.astype(BF16), x, tm=512, tn=w_out.shape[1])


def kernel(x, p, norm_mix, norm_ffn, norm_ple, ffn_up, ffn_down, ple_proj, ple_gate, gm_in, gm_ln_g, gm_ln_b, gm_ws, gm_bs, gm_out, nsa_in, nsa_kc_pe, nsa_kc_w1, nsa_kc_w2, nsa_vc_pe, nsa_vc_w1, nsa_vc_w2, nsa_out, final_norm):
    batch, seq, d = x.shape
    m = batch * seq
    depth = p.shape[0]
    xf = x.reshape(m, d)
    row = lambda v: v.reshape(1, -1)
    for i in range(depth):
        j = i // 2
        if i % 2 == 0:
            z = _norm_mm(xf, row(norm_mix[i]), gm_in[j].astype(BF16), act="gelu", out_dtype=BF16,
                         tm=512, tn=gm_in.shape[-1])
            xf = _gmlp_gate_out(z, xf, row(gm_ln_g[j]), row(gm_ln_b[j]), gm_ws[j], gm_bs[j].T,
                                gm_out[j].astype(BF16), tm=512)
        else:
            xf = _nsa_layer(xf, h_mix, nsa_in[j], nsa_kc_pe[j], nsa_kc_w1[j], nsa_kc_w2[j],
                            nsa_vc_pe[j], nsa_vc_w1[j], nsa_vc_w2[j], nsa_out[j], batch, seq)
        xf = _ffn(xf, row(norm_ffn[i]), ffn_up[i].astype(BF16), ffn_down[i].astype(BF16), tm=1024, tf=512)
        if i == depth - 1:
            post, post_g = "final", final_norm
        elif (i + 1) % 2 == 1:
            post, post_g = "next", norm_mix[i + 1]
        else:
            post, post_g = "none", final_norm
        out = _ple(xf, p[i].reshape(m, -1), row(norm_ple[i]), ple_gate[i].astype(BF16),
                   ple_proj[i].astype(BF16), row(post_g), post=post, tm=256)
        xf, h_mix = out if post == "next" else (out, None)
    return xf.reshape(batch, seq, d)
```

```python
import functools

import numpy as np
import jax
import jax.numpy as jnp
from jax import lax
from jax.experimental import pallas as pl
from jax.experimental.pallas import tpu as pltpu

F32 = jnp.float32
BF16 = jnp.bfloat16

EPS = 1e-6
HEAD_DIM = 128
NSA_HEADS = 16
NSA_KV_GROUPS = 4
NSA_HPG = NSA_HEADS // NSA_KV_GROUPS
ROT_DIM = HEAD_DIM // 4
ROPE_THETA = 500000.0
CMP_LEN = 32
CMP_STRIDE = 16
SEL_BLOCK = 64
SEL_TOPK = 16
WINDOW = 512
Q_BLOCK = 128
GM_CHUNK = 128
NEG_INF = -1e30
LOG2_E = 1.4426950408889634

LANES = 128
SEL_BIAS_BLOCKS = LANES
SEL_BIAS_KEYS = SEL_BIAS_BLOCKS * SEL_BLOCK
VMEM_LIMIT = 56 * 1024 * 1024


def _params(sem):
    return pltpu.CompilerParams(dimension_semantics=sem, vmem_limit_bytes=VMEM_LIMIT)


def _rmsnorm(x, g):
    return x * lax.rsqrt(jnp.mean(x * x, axis=-1, keepdims=True) + EPS) * g


def _dot(a, b):
    return jnp.dot(a, b, preferred_element_type=F32)


def _rope(x, c, sa, sb):
    return x * c + pltpu.roll(x, LANES - ROT_DIM // 2, 1) * sa + pltpu.roll(x, ROT_DIM // 2, 1) * sb


def _rope_tables(pos):
    half = ROT_DIM // 2
    inv = jnp.power(jnp.float32(ROPE_THETA), -jnp.arange(half, dtype=F32) * 2.0 / ROT_DIM)
    ang = pos.astype(F32)[:, None] * inv[None, :]
    cos, sin = jnp.cos(ang), jnp.sin(ang)
    n = pos.shape[0]
    rest = HEAD_DIM - ROT_DIM
    c = jnp.concatenate([cos, cos, jnp.ones((n, rest), F32)], axis=1)
    sa = jnp.concatenate([-sin, jnp.zeros((n, half + rest), F32)], axis=1)
    sb = jnp.concatenate([jnp.zeros((n, half), F32), sin, jnp.zeros((n, rest), F32)], axis=1)
    return c, sa, sb


def _norm_mm_body(x_ref, g_ref, w_ref, o_ref, h_ref, *, act):
    @pl.when(pl.program_id(1) == 0)
    def _():
        h_ref[...] = _rmsnorm(x_ref[...], g_ref[...]).astype(BF16)

    acc = _dot(h_ref[...], w_ref[...])
    if act == "gelu":
        acc = jax.nn.gelu(acc)
    elif act == "sigmoid":
        acc = jax.nn.sigmoid(acc)
    o_ref[...] = acc.astype(o_ref.dtype)


def _norm_mm(x, g, w, *, act, out_dtype, tm, tn):
    m, d = x.shape
    n = w.shape[1]
    return pl.pallas_call(
        functools.partial(_norm_mm_body, act=act),
        grid=(m // tm, n // tn),
        in_specs=[
            pl.BlockSpec((tm, d), lambda i, j: (i, 0)),
            pl.BlockSpec((1, d), lambda i, j: (0, 0)),
            pl.BlockSpec((d, tn), lambda i, j: (0, j)),
        ],
        out_specs=pl.BlockSpec((tm, tn), lambda i, j: (i, j)),
        out_shape=jax.ShapeDtypeStruct((m, n), out_dtype),
        scratch_shapes=[pltpu.VMEM((tm, d), BF16)],
        compiler_params=_params(("parallel", "arbitrary")),
        name="norm_mm_" + str(act),
    )(x, g, w)


def _mm_sigmoid_body(h_ref, w_ref, o_ref):
    o_ref[...] = jax.nn.sigmoid(_dot(h_ref[...], w_ref[...]))


def _mm_sigmoid(h, w, *, tm):
    m, d = h.shape
    n = w.shape[1]
    return pl.pallas_call(
        _mm_sigmoid_body,
        grid=(m // tm,),
        in_specs=[pl.BlockSpec((tm, d), lambda i: (i, 0)), pl.BlockSpec((d, n), lambda i: (0, 0))],
        out_specs=pl.BlockSpec((tm, n), lambda i: (i, 0)),
        out_shape=jax.ShapeDtypeStruct((m, n), F32),
        compiler_params=_params(("parallel",)),
        name="mm_sigmoid",
    )(h, w)


def _gmlp_body(z_ref, x_ref, lg_ref, lb_ref, ws_ref, bs_ref, wo_ref, o_ref, y_ref, *, tm, width):
    groups = ws_ref.shape[0]
    gd = width // groups
    u = z_ref[:, :width]
    v = z_ref[:, width:].astype(F32)
    mu = jnp.mean(v, axis=-1, keepdims=True)
    var = jnp.mean(jnp.square(v - mu), axis=-1, keepdims=True)
    vn = ((v - mu) * lax.rsqrt(var + EPS) * lg_ref[...] + lb_ref[...]).astype(BF16)
    r = lax.broadcasted_iota(jnp.int32, (GM_CHUNK, GM_CHUNK), 0)
    c = lax.broadcasted_iota(jnp.int32, (GM_CHUNK, GM_CHUNK), 1)
    causal = c <= r
    for g in range(groups):
        wg = jnp.where(causal, ws_ref[g], 0.0).astype(BF16)
        bg = bs_ref[:, g:g + 1]
        for ch in range(tm // GM_CHUNK):
            rows = slice(ch * GM_CHUNK, (ch + 1) * GM_CHUNK)
            cols = slice(g * gd, (g + 1) * gd)
            sv = _dot(wg, vn[rows, cols]) + bg
            y_ref[rows, cols] = (u[rows, cols].astype(F32) * sv).astype(BF16)
    o_ref[...] = x_ref[...] + _dot(y_ref[...], wo_ref[...])


def _gmlp_gate_out(z, x, ln_g, ln_b, ws, bs, w_out, *, tm):
    m, d = x.shape
    width = z.shape[1] // 2
    groups = ws.shape[0]
    return pl.pallas_call(
        functools.partial(_gmlp_body, tm=tm, width=width),
        grid=(m // tm,),
        in_specs=[
            pl.BlockSpec((tm, 2 * width), lambda i: (i, 0)),
            pl.BlockSpec((tm, d), lambda i: (i, 0)),
            pl.BlockSpec((1, width), lambda i: (0, 0)),
            pl.BlockSpec((1, width), lambda i: (0, 0)),
            pl.BlockSpec((groups, GM_CHUNK, GM_CHUNK), lambda i: (0, 0, 0)),
            pl.BlockSpec((GM_CHUNK, groups), lambda i: (0, 0)),
            pl.BlockSpec((width, d), lambda i: (0, 0)),
        ],
        out_specs=pl.BlockSpec((tm, d), lambda i: (i, 0)),
        out_shape=jax.ShapeDtypeStruct((m, d), F32),
        scratch_shapes=[pltpu.VMEM((tm, width), BF16)],
        compiler_params=_params(("parallel",)),
        name="gmlp_gate_out",
    )(z, x, ln_g, ln_b, ws, bs, w_out)


def _ffn_body(x_ref, g_ref, wu_ref, wd_ref, o_ref, h_ref):
    f = pl.program_id(1)

    @pl.when(f == 0)
    def _():
        x = x_ref[...]
        h_ref[...] = _rmsnorm(x, g_ref[...]).astype(BF16)
        o_ref[...] = x

    a = jnp.square(jnp.maximum(_dot(h_ref[...], wu_ref[...]), 0.0)).astype(BF16)
    o_ref[...] += _dot(a, wd_ref[...])


def _ffn(x, g, w_up, w_down, *, tm, tf):
    m, d = x.shape
    ff = w_up.shape[1]
    return pl.pallas_call(
        _ffn_body,
        grid=(m // tm, ff // tf),
        in_specs=[
            pl.BlockSpec((tm, d), lambda i, f: (i, 0)),
            pl.BlockSpec((1, d), lambda i, f: (0, 0)),
            pl.BlockSpec((d, tf), lambda i, f: (0, f)),
            pl.BlockSpec((tf, d), lambda i, f: (f, 0)),
        ],
        out_specs=pl.BlockSpec((tm, d), lambda i, f: (i, 0)),
        out_shape=jax.ShapeDtypeStruct((m, d), F32),
        scratch_shapes=[pltpu.VMEM((tm, d), BF16)],
        compiler_params=_params(("parallel", "arbitrary")),
        name="ffn",
    )(x, g, w_up, w_down)


def _ple_body(x_ref, p_ref, g_ref, wg_ref, wp_ref, pg_ref, *o_refs, post):
    x = x_ref[...]
    h = _rmsnorm(x, g_ref[...]).astype(BF16)
    gate = jax.nn.sigmoid(_dot(h, wg_ref[...]))
    y = x + gate * _dot(p_ref[...].astype(BF16), wp_ref[...])
    if post == "final":
        o_refs[0][...] = _rmsnorm(y, pg_ref[...])
    else:
        o_refs[0][...] = y
    if post == "next":
        o_refs[1][...] = _rmsnorm(y, pg_ref[...]).astype(BF16)


def _ple(x, p, g, w_gate, w_proj, post_g, *, post, tm):
    m, d = x.shape
    pd = p.shape[1]
    row_spec = pl.BlockSpec((tm, d), lambda i: (i, 0))
    out_specs, out_shape = row_spec, jax.ShapeDtypeStruct((m, d), F32)
    if post == "next":
        out_specs, out_shape = [row_spec, row_spec], [out_shape, jax.ShapeDtypeStruct((m, d), BF16)]
    return pl.pallas_call(
        functools.partial(_ple_body, post=post),
        grid=(m // tm,),
        in_specs=[
            row_spec,
            pl.BlockSpec((tm, pd), lambda i: (i, 0)),
            pl.BlockSpec((1, d), lambda i: (0, 0)),
            pl.BlockSpec((d, d), lambda i: (0, 0)),
            pl.BlockSpec((pd, d), lambda i: (0, 0)),
            pl.BlockSpec((1, d), lambda i: (0, 0)),
        ],
        out_specs=out_specs,
        out_shape=out_shape,
        compiler_params=_params(("parallel",)),
        name="ple_" + post,
    )(x, p, g, w_gate, w_proj, post_g)


def _mm_res_body(a_ref, w_ref, x_ref, o_ref):
    o_ref[...] = x_ref[...] + _dot(a_ref[...], w_ref[...])


def _mm_residual(a, w, x, *, tm, tn):
    m, k = a.shape
    n = w.shape[1]
    return pl.pallas_call(
        _mm_res_body,
        grid=(m // tm, n // tn),
        in_specs=[
            pl.BlockSpec((tm, k), lambda i, j: (i, 0)),
            pl.BlockSpec((k, tn), lambda i, j: (0, j)),
            pl.BlockSpec((tm, tn), lambda i, j: (i, j)),
        ],
        out_specs=pl.BlockSpec((tm, tn), lambda i, j: (i, j)),
        out_shape=jax.ShapeDtypeStruct((m, n), F32),
        compiler_params=_params(("parallel", "arbitrary")),
        name="mm_residual",
    )(a, w, x)


ROT_KS, ROT_KW = 0, NSA_KV_GROUPS


def _proj_rotated_body(h_ref, w_ref, c_ref, sa_ref, sb_ref, o_ref):
    acc = _dot(h_ref[...], w_ref[...])
    c, sa, sb = c_ref[...], sa_ref[...], sb_ref[...]
    for hh in range(o_ref.shape[0]):
        o_ref[hh] = _rope(acc[:, hh * HEAD_DIM:(hh + 1) * HEAD_DIM], c, sa, sb).astype(BF16)


def _proj_query_body(h_ref, w_ref, c_ref, sa_ref, sb_ref, o_ref):
    acc = _dot(h_ref[...], w_ref[...])
    c, sa, sb = c_ref[...], sa_ref[...], sb_ref[...]
    tile = o_ref.shape[-1]
    for hh in range(o_ref.shape[0]):
        seg = _rope(acc[:, hh * HEAD_DIM:(hh + 1) * HEAD_DIM], c, sa, sb) * (HEAD_DIM ** -0.5 * LOG2_E)
        for t in range(acc.shape[0] // tile):
            o_ref[hh, t] = seg[t * tile:(t + 1) * tile].T.astype(BF16)


def _proj_grouped_body(h_ref, w_ref, o_ref, rows_ref):
    acc = _dot(h_ref[...], w_ref[...])
    groups = acc.shape[0] // CMP_STRIDE
    for hh in range(o_ref.shape[0]):
        rows_ref[hh] = acc[:, hh * HEAD_DIM:(hh + 1) * HEAD_DIM]
        o_ref[hh] = jnp.concatenate([rows_ref[hh, pl.ds(l, groups, stride=CMP_STRIDE), :]
                                     for l in range(CMP_STRIDE)], axis=1).astype(BF16)


def _proj_flipped_body(h_ref, w_ref, o_ref):
    acc = _dot(h_ref[...], w_ref[...])
    tile = o_ref.shape[-1]
    for hh in range(o_ref.shape[0]):
        for t in range(acc.shape[0] // tile):
            o_ref[hh, t] = acc[t * tile:(t + 1) * tile, hh * HEAD_DIM:(hh + 1) * HEAD_DIM].T.astype(BF16)


def _proj(body, h, w, extra, extra_specs, out_block, out_dims, scratch, name, *, tm):
    m, d = h.shape
    n = w.shape[1]
    heads = n // HEAD_DIM
    return pl.pallas_call(
        body,
        grid=(m // tm,),
        in_specs=[pl.BlockSpec((tm, d), lambda i: (i, 0)), pl.BlockSpec((d, n), lambda i: (0, 0))] + extra_specs,
        out_specs=pl.BlockSpec((heads,) + out_block, lambda i: (0, i) + (0,) * (len(out_block) - 1)),
        out_shape=jax.ShapeDtypeStruct((heads,) + out_dims, BF16),
        scratch_shapes=scratch,
        compiler_params=_params(("parallel",)),
        name=name,
    )(h, w, *extra)


def _nsa_projections(h, w_in, tables, seq, *, tm, sel_tile):
    m = h.shape[0]
    hd = HEAD_DIM
    q_end = NSA_HEADS * hd
    kv = NSA_KV_GROUPS * hd
    kc, vc, ks, vs, kw, vw = (slice(q_end + i * kv, q_end + (i + 1) * kv) for i in range(6))
    cast = lambda cols: jnp.concatenate([w_in[:, c] for c in cols], axis=1).astype(BF16)
    tpb = seq // tm
    tab_spec = pl.BlockSpec((tm, hd), lambda i: (i % tpb, 0))
    queries = _proj(_proj_query_body, h, cast([slice(0, q_end)]), tables, [tab_spec] * 3,
                    (tm // Q_BLOCK, hd, Q_BLOCK), (m // Q_BLOCK, hd, Q_BLOCK), [], "nsa_proj_query", tm=tm)
    rotated = _proj(_proj_rotated_body, h, cast([ks, kw]), tables, [tab_spec] * 3,
                    (tm, hd), (m, hd), [], "nsa_proj_rotated", tm=tm)
    grouped = _proj(_proj_grouped_body, h, cast([kc, vc]), (), [],
                    (tm // CMP_STRIDE, CMP_STRIDE * hd), (m // CMP_STRIDE, CMP_STRIDE * hd),
                    [pltpu.VMEM((2 * NSA_KV_GROUPS, tm, hd), F32)], "nsa_proj_grouped", tm=tm)
    flipped = [_proj(_proj_flipped_body, h, cast([cols]), (), [], (tile_rows // tile, hd, tile),
                     (m // tile, hd, tile), [], "nsa_proj_flipped", tm=tile_rows)
               for cols, tile, tile_rows in ((vs, sel_tile, max(tm, sel_tile)), (vw, Q_BLOCK, tm))]
    return queries, rotated, grouped, flipped


CMP_CHUNK = LANES


def _compress_body(kr_ref, pe_ref, w1_ref, w2_ref, c_ref, sa_ref, sb_ref, o_ref):
    kr = kr_ref[0].astype(F32)
    rows = kr.shape[0]
    a = _dot((kr + pe_ref[0, 0]).astype(BF16), w1_ref[0, 0])
    b = _dot((kr + pe_ref[0, 1]).astype(BF16), w1_ref[0, 1])
    hid = jax.nn.gelu(a + pltpu.roll(b, rows - 1, 0))
    out = _dot(hid.astype(BF16), w2_ref[0])
    is_k = pl.program_id(0) == 0

    @pl.when(is_k)
    def _():
        roped = _rope(out, c_ref[...], sa_ref[...], sb_ref[...]).astype(BF16)
        for j in range(rows // CMP_CHUNK):
            o_ref[0, 0, 0, j] = roped[j * CMP_CHUNK:(j + 1) * CMP_CHUNK]

    @pl.when(jnp.logical_not(is_k))
    def _():
        for j in range(rows // CMP_CHUNK):
            o_ref[0, 0, 0, j] = out[j * CMP_CHUNK:(j + 1) * CMP_CHUNK].T.astype(BF16)


def _compress(kr, pe, w1, w2, tables, batch, seq):
    rows = seq // CMP_STRIDE
    kdim = CMP_STRIDE * HEAD_DIM
    hid = w1.shape[-1]
    tab_spec = pl.BlockSpec((rows, HEAD_DIM), lambda w, b, g: (0, 0))
    chunks = rows // CMP_CHUNK
    return pl.pallas_call(
        _compress_body,
        grid=(2, batch, NSA_KV_GROUPS),
        in_specs=[
            pl.BlockSpec((1, rows, kdim), lambda w, b, g: (NSA_KV_GROUPS * w + g, b, 0)),
            pl.BlockSpec((1, 2, 1, kdim), lambda w, b, g: (w, 0, 0, 0)),
            pl.BlockSpec((1, 2, kdim, hid), lambda w, b, g: (w, 0, 0, 0)),
            pl.BlockSpec((1, hid, HEAD_DIM), lambda w, b, g: (w, 0, 0)),
            tab_spec, tab_spec, tab_spec,
        ],
        out_specs=pl.BlockSpec((1, 1, 1, chunks, CMP_CHUNK, HEAD_DIM), lambda w, b, g: (w, b, g, 0, 0, 0)),
        out_shape=jax.ShapeDtypeStruct((2, batch, NSA_KV_GROUPS, chunks, CMP_CHUNK, HEAD_DIM), BF16),
        compiler_params=_params(("parallel", "parallel", "parallel")),
        name="nsa_compress",
    )(kr, pe, w1, w2, *tables)


def _attn_body(q_ref, ks_ref, vs_ref, kw_ref, vw_ref, kc_ref, vc_ref, gate_ref, e_ref,
               o_ref, qaug_ref, ps_ref, sa_ref, sb_ref, sd_ref, *, ns, tk):
    qi = pl.program_id(2)
    t0 = qi * Q_BLOCK
    cols = NSA_HPG * Q_BLOCK
    halves = qaug_ref.shape[0]
    for n in range(NSA_HPG):
        for h in range(halves):
            qaug_ref[h, 0:HEAD_DIM, n * Q_BLOCK:(n + 1) * Q_BLOCK] = q_ref[n, 0]
    q_t = qaug_ref[0, 0:HEAD_DIM, :]
    tq = t0 + (lax.broadcasted_iota(jnp.int32, (1, cols), 1) & (Q_BLOCK - 1))

    ncp = ps_ref.shape[0]
    step = min(2 * CMP_CHUNK, ncp)

    def compressed(rows):
        chunks = range(rows // CMP_CHUNK)
        s = _dot(jnp.concatenate([kc_ref[0, 0, 0, i] for i in chunks], axis=0), q_t)
        cend = lax.broadcasted_iota(jnp.int32, (rows, cols), 0) * CMP_STRIDE + (CMP_LEN - 1)
        s = jnp.where(cend <= tq, s, NEG_INF)
        p = jnp.exp2(s - jnp.max(s, axis=0, keepdims=True))
        acc = _dot(jnp.concatenate([vc_ref[0, 0, 0, i] for i in chunks], axis=1), p.astype(BF16))
        inv = jnp.where(tq >= CMP_LEN - 1, 1.0 / jnp.sum(p, axis=0, keepdims=True), 0.0)
        p = p * inv
        ps = p[:, 0:Q_BLOCK]
        for n in range(1, NSA_HPG):
            ps = ps + p[:, n * Q_BLOCK:(n + 1) * Q_BLOCK]
        ps_ref[0:rows, :] = ps
        if rows < ncp:
            ps_ref[rows:ncp, :] = jnp.zeros((ncp - rows, Q_BLOCK), F32)
        return acc * inv

    variants = [functools.partial(compressed, rows) for rows in range(step, ncp + 1, step)]
    reach = jnp.minimum((t0 + Q_BLOCK - CMP_LEN) // (CMP_STRIDE * step), len(variants) - 1)
    o_c = lax.switch(reach, variants) if len(variants) > 1 else variants[0]()

    ratio = SEL_BLOCK // CMP_STRIDE
    r = [ps_ref[pl.ds(j, ns, stride=ratio), :] for j in range(ratio)]
    prev = pltpu.roll(r[ratio - 1], 1, 0)
    prev = jnp.where(lax.broadcasted_iota(jnp.int32, (ns, Q_BLOCK), 0) == 0, 0.0, prev)
    imp = 2.0 * (r[0] + r[1] + r[2]) + r[3] + prev
    nsp = halves * SEL_BIAS_BLOCKS
    if nsp > ns:
        imp = jnp.concatenate([imp, jnp.zeros((nsp - ns, Q_BLOCK), F32)], axis=0)

    blk = lax.broadcasted_iota(jnp.int32, (nsp, Q_BLOCK), 0)
    blk_f = blk.astype(F32)
    cur = (t0 + lax.broadcasted_iota(jnp.int32, (nsp, Q_BLOCK), 1)) // SEL_BLOCK
    valid = blk <= cur
    forced = (blk == 0) | (blk == cur) | (blk == cur - 1)
    n_forced = 3
    rounds = min(SEL_TOPK, ns) - n_forced
    free = valid & jnp.logical_not(forced)
    score0 = jnp.where(free, imp, NEG_INF)

    score = score0
    for _ in range(rounds):
        score = jnp.where(score == jnp.max(score, axis=0, keepdims=True), -jnp.inf, score)
    removed = score == -jnp.inf
    n_removed = jnp.sum(jnp.where(removed & free, 1.0, 0.0), axis=0, keepdims=True)
    cur_row = (t0 + lax.broadcasted_iota(jnp.int32, (1, Q_BLOCK), 1)) // SEL_BLOCK
    n_free = jnp.clip(cur_row + 1 - n_forced, 0, rounds).astype(F32)
    n_tied = jnp.sum(jnp.where(n_removed == n_free, 0.0, 1.0))

    def tie_breaking_rounds():
        score, taken = score0, jnp.zeros((nsp, Q_BLOCK), F32)
        for _ in range(rounds):
            top = jnp.max(score, axis=0, keepdims=True)
            first = jnp.min(jnp.where(score == top, blk_f, float(nsp)), axis=0, keepdims=True)
            hit = blk_f == first
            taken = jnp.where(hit, 1.0, taken)
            score = jnp.where(hit, -jnp.inf, score)
        return taken

    taken = lax.cond(n_tied == 0.0, lambda: jnp.where(removed, 1.0, 0.0), tie_breaking_rounds)
    bias = jnp.where((forced | (taken > 0.0)) & valid, 0.0, NEG_INF).astype(BF16)

    for h in range(halves):
        bh = bias[h * SEL_BIAS_BLOCKS:(h + 1) * SEL_BIAS_BLOCKS]
        for n in range(NSA_HPG):
            qaug_ref[h, HEAD_DIM:HEAD_DIM + SEL_BIAS_BLOCKS, n * Q_BLOCK:(n + 1) * Q_BLOCK] = bh

    init = (jnp.full((1, cols), NEG_INF, F32), jnp.zeros((1, cols), F32),
            jnp.zeros((HEAD_DIM, cols), F32))

    n_win = WINDOW // Q_BLOCK + 1
    win_keys = n_win * Q_BLOCK
    w0 = pl.multiple_of(jnp.maximum(t0 - WINDOW, 0), Q_BLOCK)
    newest = tq - w0
    krow = lax.broadcasted_iota(jnp.int32, (win_keys, cols), 0)
    s_w = jnp.where(krow <= newest, _dot(kw_ref[0, pl.ds(w0, win_keys), :], q_t), NEG_INF)
    oldest = lax.broadcasted_iota(jnp.int32, (Q_BLOCK, cols), 0) > newest - WINDOW
    s_w = jnp.concatenate([jnp.where(oldest, s_w[:Q_BLOCK], NEG_INF), s_w[Q_BLOCK:]], axis=0)
    p_w = jnp.exp2(s_w - jnp.max(s_w, axis=0, keepdims=True))
    vw_t = jnp.concatenate([vw_ref[0, w0 // Q_BLOCK + w] for w in range(n_win)], axis=1)
    o_w = _dot(vw_t, p_w.astype(BF16)) * (1.0 / jnp.sum(p_w, axis=0, keepdims=True))

    def sel_scores(j, s_ref, diagonal=False):
        k0 = pl.multiple_of(j * tk, tk)
        e0 = pl.multiple_of(k0 % SEL_BIAS_KEYS, tk)
        kaug = jnp.concatenate([ks_ref[0, pl.ds(k0, tk), :], e_ref[pl.ds(e0, tk), :]], axis=1)
        s = _dot(kaug, qaug_ref[k0 // SEL_BIAS_KEYS])
        if diagonal:
            lk = lax.broadcasted_iota(jnp.int32, (tk, cols), 0)
            lq = lax.broadcasted_iota(jnp.int32, (tk, cols), 1) & (Q_BLOCK - 1)
            s = jnp.where(lk - lq <= t0 - k0, s, NEG_INF)
        s_ref[...] = s
        return jnp.max(s, axis=0, keepdims=True)

    def sel_update(s_ref, s_max, j, carry):
        m, l, acc = carry
        m_new = jnp.maximum(m, s_max)
        alpha = jnp.exp2(m - m_new)
        p = jnp.exp2(s_ref[...] - m_new)
        l = alpha * l + jnp.sum(p, axis=0, keepdims=True)
        acc = alpha * acc + _dot(vs_ref[0, j], p.astype(BF16))
        return m_new, l, acc

    n_full = t0 // tk
    last_full = jnp.maximum(n_full - 1, 0)
    max_d = sel_scores(n_full, sd_ref, diagonal=True)
    max_a = sel_scores(0, sa_ref)

    def sel_pair(j, carry):
        max_a, state = carry
        max_b = sel_scores(j + 1, sb_ref)
        state = sel_update(sa_ref, max_a, j, state)
        max_a = sel_scores(jnp.minimum(j + 2, last_full), sa_ref)
        return max_a, sel_update(sb_ref, max_b, j + 1, state)

    unroll = 4
    carry = lax.fori_loop(0, n_full // unroll,
                          lambda jj, c: sel_pair(unroll * jj + 2, sel_pair(unroll * jj, c)), (max_a, init))

    def finish(rest, carry):
        max_cur, state = carry
        first = n_full - rest
        bufs = (sa_ref, sb_ref)
        for i in range(rest):
            if i + 1 < rest:
                max_next = sel_scores(first + i + 1, bufs[(i + 1) % 2])
            state = sel_update(bufs[i % 2], max_cur, first + i, state)
            if i + 1 < rest:
                max_cur = max_next
        _, l_s, acc_s = sel_update(sd_ref, max_d, n_full, state)
        o_s = acc_s * (1.0 / l_s)
        g_t = gate_ref[...].T
        for n in range(NSA_HPG):
            c = slice(n * Q_BLOCK, (n + 1) * Q_BLOCK)
            o = (g_t[n:n + 1] * o_c[:, c] + g_t[NSA_HPG + n:NSA_HPG + n + 1] * o_s[:, c]
                 + g_t[2 * NSA_HPG + n:2 * NSA_HPG + n + 1] * o_w[:, c])
            o_ref[:, n * HEAD_DIM:(n + 1) * HEAD_DIM] = o.T.astype(BF16)
        return jnp.int32(0)

    lax.switch(n_full % unroll, [functools.partial(finish, rest) for rest in range(unroll)], carry)


def _nsa_attention(queries, rotated, vs_t, vw_t, cmp, gates, onehot, batch, seq, *, tk):
    nq = seq // Q_BLOCK
    ns = seq // SEL_BLOCK
    ncp = seq // CMP_STRIDE
    halves = -(-ns // SEL_BIAS_BLOCKS)
    m = batch * seq
    once = pl.Buffered(1)

    def slab(first):
        return pl.BlockSpec((1, seq, HEAD_DIM), lambda b, g, qi: (first + g, b, 0), pipeline_mode=once)

    def slab_t(tile):
        return pl.BlockSpec((1, seq // tile, HEAD_DIM, tile), lambda b, g, qi: (g, b, 0, 0), pipeline_mode=once)

    def cmp_spec(which):
        return pl.BlockSpec((1, 1, 1, ncp // CMP_CHUNK, CMP_CHUNK, HEAD_DIM),
                            lambda b, g, qi: (which, b, g, 0, 0, 0))

    return pl.pallas_call(
        functools.partial(_attn_body, ns=ns, tk=tk),
        grid=(batch, NSA_KV_GROUPS, nq),
        in_specs=[
            pl.BlockSpec((NSA_HPG, 1, HEAD_DIM, Q_BLOCK), lambda b, g, qi: (g, b * nq + qi, 0, 0)),
            slab(ROT_KS), slab_t(tk), slab(ROT_KW), slab_t(Q_BLOCK),
            cmp_spec(0), cmp_spec(1),
            pl.BlockSpec((Q_BLOCK, LANES), lambda b, g, qi: (b * nq + qi, g)),
            pl.BlockSpec(onehot.shape, lambda b, g, qi: (0, 0), pipeline_mode=once),
        ],
        out_specs=pl.BlockSpec((Q_BLOCK, NSA_HPG * HEAD_DIM), lambda b, g, qi: (b * nq + qi, g)),
        out_shape=jax.ShapeDtypeStruct((m, NSA_HEADS * HEAD_DIM), BF16),
        scratch_shapes=[
            pltpu.VMEM((halves, 2 * HEAD_DIM, NSA_HPG * Q_BLOCK), BF16),
            pltpu.VMEM((ncp, Q_BLOCK), F32),
            pltpu.VMEM((tk, NSA_HPG * Q_BLOCK), F32),
            pltpu.VMEM((tk, NSA_HPG * Q_BLOCK), F32),
            pltpu.VMEM((tk, NSA_HPG * Q_BLOCK), F32),
        ],
        compiler_params=_params(("parallel", "parallel", "arbitrary")),
        name="nsa_attention",
    )(queries, rotated, vs_t, rotated, vw_t, cmp, cmp, gates, onehot)


def _sel_onehot(seq):
    keys = np.arange(min(seq, SEL_BIAS_KEYS))
    onehot = (keys[:, None] // SEL_BLOCK == np.arange(SEL_BIAS_BLOCKS)[None, :]).astype(np.float32)
    return jnp.asarray(onehot, BF16)


def _gate_weight(w_gl):
    d = w_gl.shape[0]
    w = w_gl.reshape(d, 3, NSA_KV_GROUPS, NSA_HPG).transpose(0, 2, 1, 3).reshape(d, NSA_KV_GROUPS, 3 * NSA_HPG)
    w = jnp.pad(w, ((0, 0), (0, 0), (0, LANES - 3 * NSA_HPG)))
    return w.reshape(d, NSA_KV_GROUPS * LANES)


def _nsa_layer(x, h, w_in, kc_pe, kc_w1, kc_w2, vc_pe, vc_w1, vc_w2, w_out, batch, seq, *, tk=1024):
    assert SEL_BLOCK == 4 * CMP_STRIDE and CMP_LEN == 2 * CMP_STRIDE and seq % max(tk, CMP_STRIDE * CMP_CHUNK) == 0
    n_main = NSA_HEADS * HEAD_DIM + 6 * NSA_KV_GROUPS * HEAD_DIM
    queries, rotated, kr, (vs_t, vw_t) = _nsa_projections(h, w_in, _rope_tables(jnp.arange(seq)), seq, tm=512, sel_tile=tk)
    gates = _mm_sigmoid(h, _gate_weight(w_in[:, n_main:]).astype(BF16), tm=1024)
    rows = seq // CMP_STRIDE
    cmp_tables = _rope_tables(jnp.arange(rows) * CMP_STRIDE + CMP_LEN - 1)
    half = CMP_LEN // 2
    pe = jnp.stack([kc_pe, vc_pe]).reshape(2, 2, 1, half * HEAD_DIM)
    w1 = jnp.stack([kc_w1, vc_w1]).astype(BF16)
    w1 = w1.reshape(2, 2, half * HEAD_DIM, w1.shape[-1])
    w2 = jnp.stack([kc_w2, vc_w2]).astype(BF16)
    cmp = _compress(kr, pe, w1, w2, cmp_tables, batch, seq)
    att = _nsa_attention(queries, rotated, vs_t, vw_t, cmp, gates, _sel_onehot(seq), batch, seq, tk=tk)
    return _mm_residual(att, w_out.astype(BF16), x, tm=512, tn=w_out.shape[1])


def kernel(x, p, norm_mix, norm_ffn, norm_ple, ffn_up, ffn_down, ple_proj, ple_gate, gm_in, gm_ln_g, gm_ln_b, gm_ws, gm_bs, gm_out, nsa_in, nsa_kc_pe, nsa_kc_w1, nsa_kc_w2, nsa_vc_pe, nsa_vc_w1, nsa_vc_w2, nsa_out, final_norm):
    batch, seq, d = x.shape
    m = batch * seq
    depth = p.shape[0]
    xf = x.reshape(m, d)
    row = lambda v: v.reshape(1, -1)
    for i in range(depth):
        j = i // 2
        if i % 2 == 0:
            z = _norm_mm(xf, row(norm_mix[i]), gm_in[j].astype(BF16), act="gelu", out_dtype=BF16,
                         tm=512, tn=gm_in.shape[-1])
            xf = _gmlp_gate_out(z, xf, row(gm_ln_g[j]), row(gm_ln_b[j]), gm_ws[j], gm_bs[j].T,
                                gm_out[j].astype(BF16), tm=512)
        else:
            xf = _nsa_layer(xf, h_mix, nsa_in[j], nsa_kc_pe[j], nsa_kc_w1[j], nsa_kc_w2[j],
                            nsa_vc_pe[j], nsa_vc_w1[j], nsa_vc_w2[j], nsa_out[j], batch, seq)
        xf = _ffn(xf, row(norm_ffn[i]), ffn_up[i].astype(BF16), ffn_down[i].astype(BF16), tm=1024, tf=512)
        if i == depth - 1:
            post, post_g = "final", final_norm
        elif (i + 1) % 2 == 1:
            post, post_g = "next", norm_mix[i + 1]
        else:
            post, post_g = "none", final_norm
        out = _ple(xf, p[i].reshape(m, -1), row(norm_ple[i]), ple_gate[i].astype(BF16),
                   ple_proj[i].astype(BF16), row(post_g), post=post, tm=256)
        xf, h_mix = out if post == "next" else (out, None)
    return xf.reshape(batch, seq, d)
```

```python
import functools

import numpy as np
import jax
import jax.numpy as jnp
from jax import lax
from jax.experimental import pallas as pl
from jax.experimental.pallas import tpu as pltpu

F32 = jnp.float32
BF16 = jnp.bfloat16

EPS = 1e-6
HEAD_DIM = 128
NSA_HEADS = 16
NSA_KV_GROUPS = 4
NSA_HPG = NSA_HEADS // NSA_KV_GROUPS
ROT_DIM = HEAD_DIM // 4
ROPE_THETA = 500000.0
CMP_LEN = 32
CMP_STRIDE = 16
SEL_BLOCK = 64
SEL_TOPK = 16
WINDOW = 512
Q_BLOCK = 128
GM_CHUNK = 128
NEG_INF = -1e30
LOG2_E = 1.4426950408889634

LANES = 128
SEL_BIAS_BLOCKS = LANES
SEL_BIAS_KEYS = SEL_BIAS_BLOCKS * SEL_BLOCK
VMEM_LIMIT = 56 * 1024 * 1024


def _params(sem):
    return pltpu.CompilerParams(dimension_semantics=sem, vmem_limit_bytes=VMEM_LIMIT)


def _rmsnorm(x, g):
    return x * lax.rsqrt(jnp.mean(x * x, axis=-1, keepdims=True) + EPS) * g


def _dot(a, b):
    return jnp.dot(a, b, preferred_element_type=F32)


def _rope(x, c, sa, sb):
    return x * c + pltpu.roll(x, LANES - ROT_DIM // 2, 1) * sa + pltpu.roll(x, ROT_DIM // 2, 1) * sb


def _rope_tables(pos):
    half = ROT_DIM // 2
    inv = jnp.power(jnp.float32(ROPE_THETA), -jnp.arange(half, dtype=F32) * 2.0 / ROT_DIM)
    ang = pos.astype(F32)[:, None] * inv[None, :]
    cos, sin = jnp.cos(ang), jnp.sin(ang)
    n = pos.shape[0]
    rest = HEAD_DIM - ROT_DIM
    c = jnp.concatenate([cos, cos, jnp.ones((n, rest), F32)], axis=1)
    sa = jnp.concatenate([-sin, jnp.zeros((n, half + rest), F32)], axis=1)
    sb = jnp.concatenate([jnp.zeros((n, half), F32), sin, jnp.zeros((n, rest), F32)], axis=1)
    return c, sa, sb


def _norm_mm_body(x_ref, g_ref, w_ref, o_ref, h_ref, *, act):
    @pl.when(pl.program_id(1) == 0)
    def _():
        h_ref[...] = _rmsnorm(x_ref[...], g_ref[...]).astype(BF16)

    acc = _dot(h_ref[...], w_ref[...])
    if act == "gelu":
        acc = jax.nn.gelu(acc)
    elif act == "sigmoid":
        acc = jax.nn.sigmoid(acc)
    o_ref[...] = acc.astype(o_ref.dtype)


def _norm_mm(x, g, w, *, act, out_dtype, tm, tn):
    m, d = x.shape
    n = w.shape[1]
    return pl.pallas_call(
        functools.partial(_norm_mm_body, act=act),
        grid=(m // tm, n // tn),
        in_specs=[
            pl.BlockSpec((tm, d), lambda i, j: (i, 0)),
            pl.BlockSpec((1, d), lambda i, j: (0, 0)),
            pl.BlockSpec((d, tn), lambda i, j: (0, j)),
        ],
        out_specs=pl.BlockSpec((tm, tn), lambda i, j: (i, j)),
        out_shape=jax.ShapeDtypeStruct((m, n), out_dtype),
        scratch_shapes=[pltpu.VMEM((tm, d), BF16)],
        compiler_params=_params(("parallel", "arbitrary")),
        name="norm_mm_" + str(act),
    )(x, g, w)


def _mm_sigmoid_body(h_ref, w_ref, o_ref):
    o_ref[...] = jax.nn.sigmoid(_dot(h_ref[...], w_ref[...]))


def _mm_sigmoid(h, w, *, tm):
    m, d = h.shape
    n = w.shape[1]
    return pl.pallas_call(
        _mm_sigmoid_body,
        grid=(m // tm,),
        in_specs=[pl.BlockSpec((tm, d), lambda i: (i, 0)), pl.BlockSpec((d, n), lambda i: (0, 0))],
        out_specs=pl.BlockSpec((tm, n), lambda i: (i, 0)),
        out_shape=jax.ShapeDtypeStruct((m, n), F32),
        compiler_params=_params(("parallel",)),
        name="mm_sigmoid",
    )(h, w)


def _gmlp_body(z_ref, x_ref, lg_ref, lb_ref, ws_ref, bs_ref, wo_ref, o_ref, y_ref, *, tm, width):
    groups = ws_ref.shape[0]
    gd = width // groups
    u = z_ref[:, :width]
    v = z_ref[:, width:].astype(F32)
    mu = jnp.mean(v, axis=-1, keepdims=True)
    var = jnp.mean(jnp.square(v - mu), axis=-1, keepdims=True)
    vn = ((v - mu) * lax.rsqrt(var + EPS) * lg_ref[...] + lb_ref[...]).astype(BF16)
    r = lax.broadcasted_iota(jnp.int32, (GM_CHUNK, GM_CHUNK), 0)
    c = lax.broadcasted_iota(jnp.int32, (GM_CHUNK, GM_CHUNK), 1)
    causal = c <= r
    for g in range(groups):
        wg = jnp.where(causal, ws_ref[g], 0.0).astype(BF16)
        bg = bs_ref[:, g:g + 1]
        for ch in range(tm // GM_CHUNK):
            rows = slice(ch * GM_CHUNK, (ch + 1) * GM_CHUNK)
            cols = slice(g * gd, (g + 1) * gd)
            sv = _dot(wg, vn[rows, cols]) + bg
            y_ref[rows, cols] = (u[rows, cols].astype(F32) * sv).astype(BF16)
    o_ref[...] = x_ref[...] + _dot(y_ref[...], wo_ref[...])


def _gmlp_gate_out(z, x, ln_g, ln_b, ws, bs, w_out, *, tm):
    m, d = x.shape
    width = z.shape[1] // 2
    groups = ws.shape[0]
    return pl.pallas_call(
        functools.partial(_gmlp_body, tm=tm, width=width),
        grid=(m // tm,),
        in_specs=[
            pl.BlockSpec((tm, 2 * width), lambda i: (i, 0)),
            pl.BlockSpec((tm, d), lambda i: (i, 0)),
            pl.BlockSpec((1, width), lambda i: (0, 0)),
            pl.BlockSpec((1, width), lambda i: (0, 0)),
            pl.BlockSpec((groups, GM_CHUNK, GM_CHUNK), lambda i: (0, 0, 0)),
            pl.BlockSpec((GM_CHUNK, groups), lambda i: (0, 0)),
            pl.BlockSpec((width, d), lambda i: (0, 0)),
        ],
        out_specs=pl.BlockSpec((tm, d), lambda i: (i, 0)),
        out_shape=jax.ShapeDtypeStruct((m, d), F32),
        scratch_shapes=[pltpu.VMEM((tm, width), BF16)],
        compiler_params=_params(("parallel",)),
        name="gmlp_gate_out",
    )(z, x, ln_g, ln_b, ws, bs, w_out)


def _ffn_body(x_ref, g_ref, wu_ref, wd_ref, o_ref, h_ref):
    f = pl.program_id(1)

    @pl.when(f == 0)
    def _():
        x = x_ref[...]
        h_ref[...] = _rmsnorm(x, g_ref[...]).astype(BF16)
        o_ref[...] = x

    a = jnp.square(jnp.maximum(_dot(h_ref[...], wu_ref[...]), 0.0)).astype(BF16)
    o_ref[...] += _dot(a, wd_ref[...])


def _ffn(x, g, w_up, w_down, *, tm, tf):
    m, d = x.shape
    ff = w_up.shape[1]
    return pl.pallas_call(
        _ffn_body,
        grid=(m // tm, ff // tf),
        in_specs=[
            pl.BlockSpec((tm, d), lambda i, f: (i, 0)),
            pl.BlockSpec((1, d), lambda i, f: (0, 0)),
            pl.BlockSpec((d, tf), lambda i, f: (0, f)),
            pl.BlockSpec((tf, d), lambda i, f: (f, 0)),
        ],
        out_specs=pl.BlockSpec((tm, d), lambda i, f: (i, 0)),
        out_shape=jax.ShapeDtypeStruct((m, d), F32),
        scratch_shapes=[pltpu.VMEM((tm, d), BF16)],
        compiler_params=_params(("parallel", "arbitrary")),
        name="ffn",
    )(x, g, w_up, w_down)


def _ple_body(x_ref, p_ref, g_ref, wg_ref, wp_ref, pg_ref, *o_refs, post):
    x = x_ref[...]
    h = _rmsnorm(x, g_ref[...]).astype(BF16)
    gate = jax.nn.sigmoid(_dot(h, wg_ref[...]))
    y = x + gate * _dot(p_ref[...].astype(BF16), wp_ref[...])
    if post == "final":
        o_refs[0][...] = _rmsnorm(y, pg_ref[...])
    else:
        o_refs[0][...] = y
    if post == "next":
        o_refs[1][...] = _rmsnorm(y, pg_ref[...]).astype(BF16)


def _ple(x, p, g, w_gate, w_proj, post_g, *, post, tm):
    m, d = x.shape
    pd = p.shape[1]
    row_spec = pl.BlockSpec((tm, d), lambda i: (i, 0))
    out_specs, out_shape = row_spec, jax.ShapeDtypeStruct((m, d), F32)
    if post == "next":
        out_specs, out_shape = [row_spec, row_spec], [out_shape, jax.ShapeDtypeStruct((m, d), BF16)]
    return pl.pallas_call(
        functools.partial(_ple_body, post=post),
        grid=(m // tm,),
        in_specs=[
            row_spec,
            pl.BlockSpec((tm, pd), lambda i: (i, 0)),
            pl.BlockSpec((1, d), lambda i: (0, 0)),
            pl.BlockSpec((d, d), lambda i: (0, 0)),
            pl.BlockSpec((pd, d), lambda i: (0, 0)),
            pl.BlockSpec((1, d), lambda i: (0, 0)),
        ],
        out_specs=out_specs,
        out_shape=out_shape,
        compiler_params=_params(("parallel",)),
        name="ple_" + post,
    )(x, p, g, w_gate, w_proj, post_g)


def _mm_res_body(a_ref, w_ref, x_ref, o_ref):
    o_ref[...] = x_ref[...] + _dot(a_ref[...], w_ref[...])


def _mm_residual(a, w, x, *, tm, tn):
    m, k = a.shape
    n = w.shape[1]
    return pl.pallas_call(
        _mm_res_body,
        grid=(m // tm, n // tn),
        in_specs=[
            pl.BlockSpec((tm, k), lambda i, j: (i, 0)),
            pl.BlockSpec((k, tn), lambda i, j: (0, j)),
            pl.BlockSpec((tm, tn), lambda i, j: (i, j)),
        ],
        out_specs=pl.BlockSpec((tm, tn), lambda i, j: (i, j)),
        out_shape=jax.ShapeDtypeStruct((m, n), F32),
        compiler_params=_params(("parallel", "arbitrary")),
        name="mm_residual",
    )(a, w, x)


ROT_KS, ROT_KW = 0, NSA_KV_GROUPS


def _proj_rotated_body(h_ref, w_ref, c_ref, sa_ref, sb_ref, o_ref):
    acc = _dot(h_ref[...], w_ref[...])
    c, sa, sb = c_ref[...], sa_ref[...], sb_ref[...]
    for hh in range(o_ref.shape[0]):
        o_ref[hh] = _rope(acc[:, hh * HEAD_DIM:(hh + 1) * HEAD_DIM], c, sa, sb).astype(BF16)


def _proj_query_body(h_ref, w_ref, c_ref, sa_ref, sb_ref, o_ref):
    acc = _dot(h_ref[...], w_ref[...])
    c, sa, sb = c_ref[...], sa_ref[...], sb_ref[...]
    tile = o_ref.shape[-1]
    for hh in range(o_ref.shape[0]):
        seg = _rope(acc[:, hh * HEAD_DIM:(hh + 1) * HEAD_DIM], c, sa, sb) * (HEAD_DIM ** -0.5 * LOG2_E)
        for t in range(acc.shape[0] // tile):
            o_ref[hh, t] = seg[t * tile:(t + 1) * tile].T.astype(BF16)


def _proj_grouped_body(h_ref, w_ref, o_ref, rows_ref):
    acc = _dot(h_ref[...], w_ref[...])
    groups = acc.shape[0] // CMP_STRIDE
    for hh in range(o_ref.shape[0]):
        rows_ref[hh] = acc[:, hh * HEAD_DIM:(hh + 1) * HEAD_DIM]
        o_ref[hh] = jnp.concatenate([rows_ref[hh, pl.ds(l, groups, stride=CMP_STRIDE), :]
                                     for l in range(CMP_STRIDE)], axis=1).astype(BF16)


def _proj_flipped_body(h_ref, w_ref, o_ref):
    acc = _dot(h_ref[...], w_ref[...])
    tile = o_ref.shape[-1]
    for hh in range(o_ref.shape[0]):
        for t in range(acc.shape[0] // tile):
            o_ref[hh, t] = acc[t * tile:(t + 1) * tile, hh * HEAD_DIM:(hh + 1) * HEAD_DIM].T.astype(BF16)


def _proj(body, h, w, extra, extra_specs, out_block, out_dims, scratch, name, *, tm):
    m, d = h.shape
    n = w.shape[1]
    heads = n // HEAD_DIM
    return pl.pallas_call(
        body,
        grid=(m // tm,),
        in_specs=[pl.BlockSpec((tm, d), lambda i: (i, 0)), pl.BlockSpec((d, n), lambda i: (0, 0))] + extra_specs,
        out_specs=pl.BlockSpec((heads,) + out_block, lambda i: (0, i) + (0,) * (len(out_block) - 1)),
        out_shape=jax.ShapeDtypeStruct((heads,) + out_dims, BF16),
        scratch_shapes=scratch,
        compiler_params=_params(("parallel",)),
        name=name,
    )(h, w, *extra)


def _nsa_projections(h, w_in, tables, seq, *, tm, sel_tile):
    m = h.shape[0]
    hd = HEAD_DIM
    q_end = NSA_HEADS * hd
    kv = NSA_KV_GROUPS * hd
    kc, vc, ks, vs, kw, vw = (slice(q_end + i * kv, q_end + (i + 1) * kv) for i in range(6))
    w_in = w_in.astype(BF16)
    cast = lambda cols: jnp.concatenate([w_in[:, c] for c in cols], axis=1)
    tpb = seq // tm
    tab_spec = pl.BlockSpec((tm, hd), lambda i: (i % tpb, 0))
    queries = _proj(_proj_query_body, h, cast([slice(0, q_end)]), tables, [tab_spec] * 3,
                    (tm // Q_BLOCK, hd, Q_BLOCK), (m // Q_BLOCK, hd, Q_BLOCK), [], "nsa_proj_query", tm=tm)
    rotated = _proj(_proj_rotated_body, h, cast([ks, kw]), tables, [tab_spec] * 3,
                    (tm, hd), (m, hd), [], "nsa_proj_rotated", tm=tm)
    grouped = _proj(_proj_grouped_body, h, cast([kc, vc]), (), [],
                    (tm // CMP_STRIDE, CMP_STRIDE * hd), (m // CMP_STRIDE, CMP_STRIDE * hd),
                    [pltpu.VMEM((2 * NSA_KV_GROUPS, tm, hd), F32)], "nsa_proj_grouped", tm=tm)
    flipped = [_proj(_proj_flipped_body, h, cast([cols]), (), [], (tile_rows // tile, hd, tile),
                     (m // tile, hd, tile), [], "nsa_proj_flipped", tm=tile_rows)
               for cols, tile, tile_rows in ((vs, sel_tile, max(tm, sel_tile)), (vw, Q_BLOCK, tm))]
    return queries, rotated, grouped, flipped


CMP_CHUNK = LANES


def _compress_body(kr_ref, pe_ref, w1_ref, w2_ref, c_ref, sa_ref, sb_ref, o_ref):
    kr = kr_ref[0].astype(F32)
    rows = kr.shape[0]
    a = _dot((kr + pe_ref[0, 0]).astype(BF16), w1_ref[0, 0])
    b = _dot((kr + pe_ref[0, 1]).astype(BF16), w1_ref[0, 1])
    hid = jax.nn.gelu(a + pltpu.roll(b, rows - 1, 0))
    out = _dot(hid.astype(BF16), w2_ref[0])
    is_k = pl.program_id(0) == 0

    @pl.when(is_k)
    def _():
        roped = _rope(out, c_ref[...], sa_ref[...], sb_ref[...]).astype(BF16)
        for j in range(rows // CMP_CHUNK):
            o_ref[0, 0, 0, j] = roped[j * CMP_CHUNK:(j + 1) * CMP_CHUNK]

    @pl.when(jnp.logical_not(is_k))
    def _():
        for j in range(rows // CMP_CHUNK):
            o_ref[0, 0, 0, j] = out[j * CMP_CHUNK:(j + 1) * CMP_CHUNK].T.astype(BF16)


def _compress(kr, pe, w1, w2, tables, batch, seq):
    rows = seq // CMP_STRIDE
    kdim = CMP_STRIDE * HEAD_DIM
    hid = w1.shape[-1]
    tab_spec = pl.BlockSpec((rows, HEAD_DIM), lambda w, b, g: (0, 0))
    chunks = rows // CMP_CHUNK
    return pl.pallas_call(
        _compress_body,
        grid=(2, batch, NSA_KV_GROUPS),
        in_specs=[
            pl.BlockSpec((1, rows, kdim), lambda w, b, g: (NSA_KV_GROUPS * w + g, b, 0)),
            pl.BlockSpec((1, 2, 1, kdim), lambda w, b, g: (w, 0, 0, 0)),
            pl.BlockSpec((1, 2, kdim, hid), lambda w, b, g: (w, 0, 0, 0)),
            pl.BlockSpec((1, hid, HEAD_DIM), lambda w, b, g: (w, 0, 0)),
            tab_spec, tab_spec, tab_spec,
        ],
        out_specs=pl.BlockSpec((1, 1, 1, chunks, CMP_CHUNK, HEAD_DIM), lambda w, b, g: (w, b, g, 0, 0, 0)),
        out_shape=jax.ShapeDtypeStruct((2, batch, NSA_KV_GROUPS, chunks, CMP_CHUNK, HEAD_DIM), BF16),
        compiler_params=_params(("parallel", "parallel", "parallel")),
        name="nsa_compress",
    )(kr, pe, w1, w2, *tables)


def _attn_body(q_ref, ks_ref, vs_ref, kw_ref, vw_ref, kc_ref, vc_ref, gate_ref, e_ref,
               o_ref, qaug_ref, ps_ref, sa_ref, sb_ref, sd_ref, *, ns, tk):
    qi = pl.program_id(2)
    t0 = qi * Q_BLOCK
    cols = NSA_HPG * Q_BLOCK
    halves = qaug_ref.shape[0]
    for n in range(NSA_HPG):
        for h in range(halves):
            qaug_ref[h, 0:HEAD_DIM, n * Q_BLOCK:(n + 1) * Q_BLOCK] = q_ref[n, 0]
    q_t = qaug_ref[0, 0:HEAD_DIM, :]
    tq = t0 + (lax.broadcasted_iota(jnp.int32, (1, cols), 1) & (Q_BLOCK - 1))

    ncp = ps_ref.shape[0]
    step = min(2 * CMP_CHUNK, ncp)

    def compressed(rows):
        chunks = range(rows // CMP_CHUNK)
        s = _dot(jnp.concatenate([kc_ref[0, 0, 0, i] for i in chunks], axis=0), q_t)
        cend = lax.broadcasted_iota(jnp.int32, (rows, cols), 0) * CMP_STRIDE + (CMP_LEN - 1)
        s = jnp.where(cend <= tq, s, NEG_INF)
        p = jnp.exp2(s - jnp.max(s, axis=0, keepdims=True))
        acc = _dot(jnp.concatenate([vc_ref[0, 0, 0, i] for i in chunks], axis=1), p.astype(BF16))
        inv = jnp.where(tq >= CMP_LEN - 1, 1.0 / jnp.sum(p, axis=0, keepdims=True), 0.0)
        p = p * inv
        ps = p[:, 0:Q_BLOCK]
        for n in range(1, NSA_HPG):
            ps = ps + p[:, n * Q_BLOCK:(n + 1) * Q_BLOCK]
        ps_ref[0:rows, :] = ps
        if rows < ncp:
            ps_ref[rows:ncp, :] = jnp.zeros((ncp - rows, Q_BLOCK), F32)
        return (acc * inv,) + fast_rounds(rows // ratio)

    ratio = SEL_BLOCK // CMP_STRIDE
    nsp = halves * SEL_BIAS_BLOCKS
    n_forced = 3
    rounds = min(SEL_TOPK, ns) - n_forced

    def block_masks(nb):
        blk = lax.broadcasted_iota(jnp.int32, (nb, Q_BLOCK), 0)
        cur = (t0 + lax.broadcasted_iota(jnp.int32, (nb, Q_BLOCK), 1)) // SEL_BLOCK
        return blk, blk <= cur, (blk == 0) | (blk == cur) | (blk == cur - 1)

    def free_scores(nb):
        r = [ps_ref[pl.ds(j, nb, stride=ratio), :] for j in range(ratio)]
        blk, valid, forced = block_masks(nb)
        prev = jnp.where(blk == 0, 0.0, pltpu.roll(r[ratio - 1], 1, 0))
        free = valid & jnp.logical_not(forced)
        return blk, free, jnp.where(free, 2.0 * (r[0] + r[1] + r[2]) + r[3] + prev, NEG_INF)

    def all_blocks(taken):
        nb = taken.shape[0]
        return taken if nb == nsp else jnp.concatenate([taken, jnp.zeros((nsp - nb, Q_BLOCK), F32)], axis=0)

    def fast_rounds(nb):
        _, free, score = free_scores(nb)
        for _ in range(rounds):
            score = jnp.where(score == jnp.max(score, axis=0, keepdims=True), -jnp.inf, score)
        removed = score == -jnp.inf
        n_removed = jnp.sum(jnp.where(removed & free, 1.0, 0.0), axis=0, keepdims=True)
        cur_row = (t0 + lax.broadcasted_iota(jnp.int32, (1, Q_BLOCK), 1)) // SEL_BLOCK
        n_free = jnp.clip(cur_row + 1 - n_forced, 0, rounds).astype(F32)
        return all_blocks(jnp.where(removed, 1.0, 0.0)), jnp.sum(jnp.where(n_removed == n_free, 0.0, 1.0))

    def tie_breaking_rounds():
        blk, _, score = free_scores(ns)
        blk_f, taken = blk.astype(F32), jnp.zeros((ns, Q_BLOCK), F32)
        for _ in range(rounds):
            top = jnp.max(score, axis=0, keepdims=True)
            first = jnp.min(jnp.where(score == top, blk_f, float(ns)), axis=0, keepdims=True)
            hit = blk_f == first
            taken = jnp.where(hit, 1.0, taken)
            score = jnp.where(hit, -jnp.inf, score)
        return all_blocks(taken)

    variants = [functools.partial(compressed, rows) for rows in range(step, ncp + 1, step)]
    reach = jnp.minimum((t0 + Q_BLOCK - CMP_LEN) // (CMP_STRIDE * step), len(variants) - 1)
    o_c, taken, n_tied = lax.switch(reach, variants) if len(variants) > 1 else variants[0]()
    taken = lax.cond(n_tied == 0.0, lambda: taken, tie_breaking_rounds)
    _, valid, forced = block_masks(nsp)
    bias = jnp.where((forced | (taken > 0.0)) & valid, 0.0, NEG_INF).astype(BF16)

    for h in range(halves):
        bh = bias[h * SEL_BIAS_BLOCKS:(h + 1) * SEL_BIAS_BLOCKS]
        for n in range(NSA_HPG):
            qaug_ref[h, HEAD_DIM:HEAD_DIM + SEL_BIAS_BLOCKS, n * Q_BLOCK:(n + 1) * Q_BLOCK] = bh

    init = (jnp.full((1, cols), NEG_INF, F32), jnp.zeros((1, cols), F32),
            jnp.zeros((HEAD_DIM, cols), F32))

    n_win = WINDOW // Q_BLOCK + 1
    win_keys = n_win * Q_BLOCK
    w0 = pl.multiple_of(jnp.maximum(t0 - WINDOW, 0), Q_BLOCK)
    newest = tq - w0
    krow = lax.broadcasted_iota(jnp.int32, (win_keys, cols), 0)
    s_w = jnp.where(krow <= newest, _dot(kw_ref[0, pl.ds(w0, win_keys), :], q_t), NEG_INF)
    oldest = lax.broadcasted_iota(jnp.int32, (Q_BLOCK, cols), 0) > newest - WINDOW
    s_w = jnp.concatenate([jnp.where(oldest, s_w[:Q_BLOCK], NEG_INF), s_w[Q_BLOCK:]], axis=0)
    p_w = jnp.exp2(s_w - jnp.max(s_w, axis=0, keepdims=True))
    vw_t = jnp.concatenate([vw_ref[0, w0 // Q_BLOCK + w] for w in range(n_win)], axis=1)
    o_w = _dot(vw_t, p_w.astype(BF16)) * (1.0 / jnp.sum(p_w, axis=0, keepdims=True))

    def sel_scores(j, s_ref, diagonal=False):
        k0 = pl.multiple_of(j * tk, tk)
        e0 = pl.multiple_of(k0 % SEL_BIAS_KEYS, tk)
        kaug = jnp.concatenate([ks_ref[0, pl.ds(k0, tk), :], e_ref[pl.ds(e0, tk), :]], axis=1)
        s = _dot(kaug, qaug_ref[k0 // SEL_BIAS_KEYS])
        if diagonal:
            lk = lax.broadcasted_iota(jnp.int32, (tk, cols), 0)
            lq = lax.broadcasted_iota(jnp.int32, (tk, cols), 1) & (Q_BLOCK - 1)
            s = jnp.where(lk - lq <= t0 - k0, s, NEG_INF)
        s_ref[...] = s
        return jnp.max(s, axis=0, keepdims=True)

    def sel_update(s_ref, s_max, j, carry):
        m, l, acc = carry
        m_new = jnp.maximum(m, s_max)
        alpha = jnp.exp2(m - m_new)
        p = jnp.exp2(s_ref[...] - m_new)
        l = alpha * l + jnp.sum(p, axis=0, keepdims=True)
        acc = alpha * acc + _dot(vs_ref[0, j], p.astype(BF16))
        return m_new, l, acc

    n_full = t0 // tk
    last_full = jnp.maximum(n_full - 1, 0)
    max_d = sel_scores(n_full, sd_ref, diagonal=True)
    max_a = sel_scores(0, sa_ref)

    def sel_pair(j, carry):
        max_a, state = carry
        max_b = sel_scores(j + 1, sb_ref)
        state = sel_update(sa_ref, max_a, j, state)
        max_a = sel_scores(jnp.minimum(j + 2, last_full), sa_ref)
        return max_a, sel_update(sb_ref, max_b, j + 1, state)

    unroll = 4
    carry = lax.fori_loop(0, n_full // unroll,
                          lambda jj, c: sel_pair(unroll * jj + 2, sel_pair(unroll * jj, c)), (max_a, init))

    def finish(rest, carry):
        max_cur, state = carry
        first = n_full - rest
        bufs = (sa_ref, sb_ref)
        for i in range(rest):
            if i + 1 < rest:
                max_next = sel_scores(first + i + 1, bufs[(i + 1) % 2])
            state = sel_update(bufs[i % 2], max_cur, first + i, state)
            if i + 1 < rest:
                max_cur = max_next
        _, l_s, acc_s = sel_update(sd_ref, max_d, n_full, state)
        o_s = acc_s * (1.0 / l_s)
        g_t = gate_ref[...].T
        for n in range(NSA_HPG):
            c = slice(n * Q_BLOCK, (n + 1) * Q_BLOCK)
            o = (g_t[n:n + 1] * o_c[:, c] + g_t[NSA_HPG + n:NSA_HPG + n + 1] * o_s[:, c]
                 + g_t[2 * NSA_HPG + n:2 * NSA_HPG + n + 1] * o_w[:, c])
            o_ref[:, n * HEAD_DIM:(n + 1) * HEAD_DIM] = o.T.astype(BF16)
        return jnp.int32(0)

    lax.switch(n_full % unroll, [functools.partial(finish, rest) for rest in range(unroll)], carry)


def _nsa_attention(queries, rotated, vs_t, vw_t, cmp, gates, onehot, batch, seq, *, tk):
    nq = seq // Q_BLOCK
    ns = seq // SEL_BLOCK
    ncp = seq // CMP_STRIDE
    halves = -(-ns // SEL_BIAS_BLOCKS)
    m = batch * seq
    once = pl.Buffered(1)

    def slab(first):
        return pl.BlockSpec((1, seq, HEAD_DIM), lambda b, g, qi: (first + g, b, 0), pipeline_mode=once)

    def slab_t(tile):
        return pl.BlockSpec((1, seq // tile, HEAD_DIM, tile), lambda b, g, qi: (g, b, 0, 0), pipeline_mode=once)

    def cmp_spec(which):
        return pl.BlockSpec((1, 1, 1, ncp // CMP_CHUNK, CMP_CHUNK, HEAD_DIM),
                            lambda b, g, qi: (which, b, g, 0, 0, 0))

    return pl.pallas_call(
        functools.partial(_attn_body, ns=ns, tk=tk),
        grid=(batch, NSA_KV_GROUPS, nq),
        in_specs=[
            pl.BlockSpec((NSA_HPG, 1, HEAD_DIM, Q_BLOCK), lambda b, g, qi: (g, b * nq + qi, 0, 0)),
            slab(ROT_KS), slab_t(tk), slab(ROT_KW), slab_t(Q_BLOCK),
            cmp_spec(0), cmp_spec(1),
            pl.BlockSpec((Q_BLOCK, LANES), lambda b, g, qi: (b * nq + qi, g)),
            pl.BlockSpec(onehot.shape, lambda b, g, qi: (0, 0), pipeline_mode=once),
        ],
        out_specs=pl.BlockSpec((Q_BLOCK, NSA_HPG * HEAD_DIM), lambda b, g, qi: (b * nq + qi, g)),
        out_shape=jax.ShapeDtypeStruct((m, NSA_HEADS * HEAD_DIM), BF16),
        scratch_shapes=[
            pltpu.VMEM((halves, 2 * HEAD_DIM, NSA_HPG * Q_BLOCK), BF16),
            pltpu.VMEM((ncp, Q_BLOCK), F32),
            pltpu.VMEM((tk, NSA_HPG * Q_BLOCK), F32),
            pltpu.VMEM((tk, NSA_HPG * Q_BLOCK), F32),
            pltpu.VMEM((tk, NSA_HPG * Q_BLOCK), F32),
        ],
        compiler_params=_params(("parallel", "parallel", "arbitrary")),
        name="nsa_attention",
    )(queries, rotated, vs_t, rotated, vw_t, cmp, cmp, gates, onehot)


def _sel_onehot(seq):
    keys = np.arange(min(seq, SEL_BIAS_KEYS))
    onehot = (keys[:, None] // SEL_BLOCK == np.arange(SEL_BIAS_BLOCKS)[None, :]).astype(np.float32)
    return jnp.asarray(onehot, BF16)


def _gate_weight(w_gl):
    d = w_gl.shape[0]
    w = w_gl.reshape(d, 3, NSA_KV_GROUPS, NSA_HPG).transpose(0, 2, 1, 3).reshape(d, NSA_KV_GROUPS, 3 * NSA_HPG)
    w = jnp.pad(w, ((0, 0), (0, 0), (0, LANES - 3 * NSA_HPG)))
    return w.reshape(d, NSA_KV_GROUPS * LANES)


def _nsa_layer(x, h, w_in, kc_pe, kc_w1, kc_w2, vc_pe, vc_w1, vc_w2, w_out, batch, seq, *, tk=1024):
    assert SEL_BLOCK == 4 * CMP_STRIDE and CMP_LEN == 2 * CMP_STRIDE and seq % max(tk, CMP_STRIDE * CMP_CHUNK) == 0
    n_main = NSA_HEADS * HEAD_DIM + 6 * NSA_KV_GROUPS * HEAD_DIM
    queries, rotated, kr, (vs_t, vw_t) = _nsa_projections(h, w_in, _rope_tables(jnp.arange(seq)), seq, tm=512, sel_tile=tk)
    gates = _mm_sigmoid(h, _gate_weight(w_in[:, n_main:]).astype(BF16), tm=1024)
    rows = seq // CMP_STRIDE
    cmp_tables = _rope_tables(jnp.arange(rows) * CMP_STRIDE + CMP_LEN - 1)
    half = CMP_LEN // 2
    pe = jnp.stack([kc_pe, vc_pe]).reshape(2, 2, 1, half * HEAD_DIM)
    w1 = jnp.stack([kc_w1, vc_w1]).astype(BF16)
    w1 = w1.reshape(2, 2, half * HEAD_DIM, w1.shape[-1])
    w2 = jnp.stack([kc_w2, vc_w2]).astype(BF16)
    cmp = _compress(kr, pe, w1, w2, cmp_tables, batch, seq)
    att = _nsa_attention(queries, rotated, vs_t, vw_t, cmp, gates, _sel_onehot(seq), batch, seq, tk=tk)
    return _mm_residual(att, w_out.astype(BF16), x, tm=512, tn=w_out.shape[1])


def kernel(x, p, norm_mix, norm_ffn, norm_ple, ffn_up, ffn_down, ple_proj, ple_gate, gm_in, gm_ln_g, gm_ln_b, gm_ws, gm_bs, gm_out, nsa_in, nsa_kc_pe, nsa_kc_w1, nsa_kc_w2, nsa_vc_pe, nsa_vc_w1, nsa_vc_w2, nsa_out, final_norm):
    batch, seq, d = x.shape
    m = batch * seq
    depth = p.shape[0]
    xf = x.reshape(m, d)
    row = lambda v: v.reshape(1, -1)
    for i in range(depth):
        j = i // 2
        if i % 2 == 0:
            z = _norm_mm(xf, row(norm_mix[i]), gm_in[j].astype(BF16), act="gelu", out_dtype=BF16,
                         tm=512, tn=gm_in.shape[-1])
            xf = _gmlp_gate_out(z, xf, row(gm_ln_g[j]), row(gm_ln_b[j]), gm_ws[j], gm_bs[j].T,
                                gm_out[j].astype(BF16), tm=512)
        else:
            xf = _nsa_layer(xf, h_mix, nsa_in[j], nsa_kc_pe[j], nsa_kc_w1[j], nsa_kc_w2[j],
                            nsa_vc_pe[j], nsa_vc_w1[j], nsa_vc_w2[j], nsa_out[j], batch, seq)
        xf = _ffn(xf, row(norm_ffn[i]), ffn_up[i].astype(BF16), ffn_down[i].astype(BF16), tm=1024, tf=512)
        if i == depth - 1:
            post, post_g = "final", final_norm
        elif (i + 1) % 2 == 1:
            post, post_g = "next", norm_mix[i + 1]
        else:
            post, post_g = "none", final_norm
        out = _ple(xf, p[i].reshape(m, -1), row(norm_ple[i]), ple_gate[i].astype(BF16),
                   ple_proj[i].astype(BF16), row(post_g), post=post, tm=512)
        xf, h_mix = out if post == "next" else (out, None)
    return xf.reshape(batch, seq, d)
```

```python
import functools

import numpy as np
import jax
import jax.numpy as jnp
from jax import lax
from jax.experimental import pallas as pl
from jax.experimental.pallas import tpu as pltpu

F32 = jnp.float32
BF16 = jnp.bfloat16

EPS = 1e-6
HEAD_DIM = 128
NSA_HEADS = 16
NSA_KV_GROUPS = 4
NSA_HPG = NSA_HEADS // NSA_KV_GROUPS
ROT_DIM = HEAD_DIM // 4
ROPE_THETA = 500000.0
CMP_LEN = 32
CMP_STRIDE = 16
SEL_BLOCK = 64
SEL_TOPK = 16
WINDOW = 512
Q_BLOCK = 128
GM_CHUNK = 128
NEG_INF = -1e30
LOG2_E = 1.4426950408889634

LANES = 128
SEL_BIAS_BLOCKS = LANES
SEL_BIAS_KEYS = SEL_BIAS_BLOCKS * SEL_BLOCK
VMEM_LIMIT = 56 * 1024 * 1024


def _params(sem):
    return pltpu.CompilerParams(dimension_semantics=sem, vmem_limit_bytes=VMEM_LIMIT)


def _rmsnorm(x, g):
    return x * lax.rsqrt(jnp.mean(x * x, axis=-1, keepdims=True) + EPS) * g


def _dot(a, b):
    return jnp.dot(a, b, preferred_element_type=F32)


def _rope(x, c, sa, sb):
    return x * c + pltpu.roll(x, LANES - ROT_DIM // 2, 1) * sa + pltpu.roll(x, ROT_DIM // 2, 1) * sb


def _rope_tables(pos):
    half = ROT_DIM // 2
    inv = jnp.power(jnp.float32(ROPE_THETA), -jnp.arange(half, dtype=F32) * 2.0 / ROT_DIM)
    ang = pos.astype(F32)[:, None] * inv[None, :]
    cos, sin = jnp.cos(ang), jnp.sin(ang)
    n = pos.shape[0]
    rest = HEAD_DIM - ROT_DIM
    c = jnp.concatenate([cos, cos, jnp.ones((n, rest), F32)], axis=1)
    sa = jnp.concatenate([-sin, jnp.zeros((n, half + rest), F32)], axis=1)
    sb = jnp.concatenate([jnp.zeros((n, half), F32), sin, jnp.zeros((n, rest), F32)], axis=1)
    return c, sa, sb


def _norm_mm_body(x_ref, g_ref, w_ref, o_ref, h_ref, *, act):
    @pl.when(pl.program_id(1) == 0)
    def _():
        h_ref[...] = _rmsnorm(x_ref[...], g_ref[...]).astype(BF16)

    acc = _dot(h_ref[...], w_ref[...])
    if act == "gelu":
        acc = jax.nn.gelu(acc)
    elif act == "sigmoid":
        acc = jax.nn.sigmoid(acc)
    o_ref[...] = acc.astype(o_ref.dtype)


def _norm_mm(x, g, w, *, act, out_dtype, tm, tn):
    m, d = x.shape
    n = w.shape[1]
    return pl.pallas_call(
        functools.partial(_norm_mm_body, act=act),
        grid=(m // tm, n // tn),
        in_specs=[
            pl.BlockSpec((tm, d), lambda i, j: (i, 0)),
            pl.BlockSpec((1, d), lambda i, j: (0, 0)),
            pl.BlockSpec((d, tn), lambda i, j: (0, j)),
        ],
        out_specs=pl.BlockSpec((tm, tn), lambda i, j: (i, j)),
        out_shape=jax.ShapeDtypeStruct((m, n), out_dtype),
        scratch_shapes=[pltpu.VMEM((tm, d), BF16)],
        compiler_params=_params(("parallel", "arbitrary")),
        name="norm_mm_" + str(act),
    )(x, g, w)


def _mm_sigmoid_body(h_ref, w_ref, o_ref):
    o_ref[...] = jax.nn.sigmoid(_dot(h_ref[...], w_ref[...]))


def _mm_sigmoid(h, w, *, tm):
    m, d = h.shape
    n = w.shape[1]
    return pl.pallas_call(
        _mm_sigmoid_body,
        grid=(m // tm,),
        in_specs=[pl.BlockSpec((tm, d), lambda i: (i, 0)), pl.BlockSpec((d, n), lambda i: (0, 0))],
        out_specs=pl.BlockSpec((tm, n), lambda i: (i, 0)),
        out_shape=jax.ShapeDtypeStruct((m, n), F32),
        compiler_params=_params(("parallel",)),
        name="mm_sigmoid",
    )(h, w)


def _gmlp_body(z_ref, x_ref, lg_ref, lb_ref, ws_ref, bs_ref, wo_ref, o_ref, y_ref, *, tm, width):
    groups = ws_ref.shape[0]
    gd = width // groups
    u = z_ref[:, :width]
    v = z_ref[:, width:].astype(F32)
    mu = jnp.mean(v, axis=-1, keepdims=True)
    var = jnp.mean(jnp.square(v - mu), axis=-1, keepdims=True)
    vn = ((v - mu) * lax.rsqrt(var + EPS) * lg_ref[...] + lb_ref[...]).astype(BF16)
    r = lax.broadcasted_iota(jnp.int32, (GM_CHUNK, GM_CHUNK), 0)
    c = lax.broadcasted_iota(jnp.int32, (GM_CHUNK, GM_CHUNK), 1)
    causal = c <= r
    for g in range(groups):
        wg = jnp.where(causal, ws_ref[g], 0.0).astype(BF16)
        bg = bs_ref[:, g:g + 1]
        for ch in range(tm // GM_CHUNK):
            rows = slice(ch * GM_CHUNK, (ch + 1) * GM_CHUNK)
            cols = slice(g * gd, (g + 1) * gd)
            sv = _dot(wg, vn[rows, cols]) + bg
            y_ref[rows, cols] = (u[rows, cols].astype(F32) * sv).astype(BF16)
    o_ref[...] = x_ref[...] + _dot(y_ref[...], wo_ref[...])


def _gmlp_gate_out(z, x, ln_g, ln_b, ws, bs, w_out, *, tm):
    m, d = x.shape
    width = z.shape[1] // 2
    groups = ws.shape[0]
    return pl.pallas_call(
        functools.partial(_gmlp_body, tm=tm, width=width),
        grid=(m // tm,),
        in_specs=[
            pl.BlockSpec((tm, 2 * width), lambda i: (i, 0)),
            pl.BlockSpec((tm, d), lambda i: (i, 0)),
            pl.BlockSpec((1, width), lambda i: (0, 0)),
            pl.BlockSpec((1, width), lambda i: (0, 0)),
            pl.BlockSpec((groups, GM_CHUNK, GM_CHUNK), lambda i: (0, 0, 0)),
            pl.BlockSpec((GM_CHUNK, groups), lambda i: (0, 0)),
            pl.BlockSpec((width, d), lambda i: (0, 0)),
        ],
        out_specs=pl.BlockSpec((tm, d), lambda i: (i, 0)),
        out_shape=jax.ShapeDtypeStruct((m, d), F32),
        scratch_shapes=[pltpu.VMEM((tm, width), BF16)],
        compiler_params=_params(("parallel",)),
        name="gmlp_gate_out",
    )(z, x, ln_g, ln_b, ws, bs, w_out)


def _ffn_body(x_ref, g_ref, wu_ref, wd_ref, o_ref, h_ref):
    f = pl.program_id(1)

    @pl.when(f == 0)
    def _():
        x = x_ref[...]
        h_ref[...] = _rmsnorm(x, g_ref[...]).astype(BF16)
        o_ref[...] = x

    a = jnp.square(jnp.maximum(_dot(h_ref[...], wu_ref[0]), 0.0)).astype(BF16)
    o_ref[...] += _dot(a, wd_ref[0])


def _ffn(x, g, w_up, w_down, layer, *, tm, tf):
    m, d = x.shape
    ff = w_up.shape[2]
    return pl.pallas_call(
        _ffn_body,
        grid=(m // tm, ff // tf),
        in_specs=[
            pl.BlockSpec((tm, d), lambda i, f: (i, 0)),
            pl.BlockSpec((1, d), lambda i, f: (0, 0)),
            pl.BlockSpec((1, d, tf), lambda i, f: (layer, 0, f)),
            pl.BlockSpec((1, tf, d), lambda i, f: (layer, f, 0)),
        ],
        out_specs=pl.BlockSpec((tm, d), lambda i, f: (i, 0)),
        out_shape=jax.ShapeDtypeStruct((m, d), F32),
        scratch_shapes=[pltpu.VMEM((tm, d), BF16)],
        compiler_params=_params(("parallel", "arbitrary")),
        name="ffn",
    )(x, g, w_up, w_down)


def _ple_body(x_ref, p_ref, g_ref, wg_ref, wp_ref, pg_ref, *o_refs, post):
    x = x_ref[...]
    h = _rmsnorm(x, g_ref[...]).astype(BF16)
    gate = jax.nn.sigmoid(_dot(h, wg_ref[0]))
    y = x + gate * _dot(p_ref[0].astype(BF16), wp_ref[0])
    if post == "final":
        o_refs[0][...] = _rmsnorm(y, pg_ref[...])
    else:
        o_refs[0][...] = y
    if post == "next":
        o_refs[1][...] = _rmsnorm(y, pg_ref[...]).astype(BF16)


def _ple(x, p, g, w_gate, w_proj, post_g, layer, *, post, tm):
    m, d = x.shape
    pd = p.shape[2]
    row_spec = pl.BlockSpec((tm, d), lambda i: (i, 0))
    out_specs, out_shape = row_spec, jax.ShapeDtypeStruct((m, d), F32)
    if post == "next":
        out_specs, out_shape = [row_spec, row_spec], [out_shape, jax.ShapeDtypeStruct((m, d), BF16)]
    return pl.pallas_call(
        functools.partial(_ple_body, post=post),
        grid=(m // tm,),
        in_specs=[
            row_spec,
            pl.BlockSpec((1, tm, pd), lambda i: (layer, i, 0)),
            pl.BlockSpec((1, d), lambda i: (0, 0)),
            pl.BlockSpec((1, d, d), lambda i: (layer, 0, 0)),
            pl.BlockSpec((1, pd, d), lambda i: (layer, 0, 0)),
            pl.BlockSpec((1, d), lambda i: (0, 0)),
        ],
        out_specs=out_specs,
        out_shape=out_shape,
        compiler_params=_params(("parallel",)),
        name="ple_" + post,
    )(x, p, g, w_gate, w_proj, post_g)


def _mm_res_body(a_ref, w_ref, x_ref, o_ref):
    o_ref[...] = x_ref[...] + _dot(a_ref[...], w_ref[...])


def _mm_residual(a, w, x, *, tm, tn):
    m, k = a.shape
    n = w.shape[1]
    return pl.pallas_call(
        _mm_res_body,
        grid=(m // tm, n // tn),
        in_specs=[
            pl.BlockSpec((tm, k), lambda i, j: (i, 0)),
            pl.BlockSpec((k, tn), lambda i, j: (0, j)),
            pl.BlockSpec((tm, tn), lambda i, j: (i, j)),
        ],
        out_specs=pl.BlockSpec((tm, tn), lambda i, j: (i, j)),
        out_shape=jax.ShapeDtypeStruct((m, n), F32),
        compiler_params=_params(("parallel", "arbitrary")),
        name="mm_residual",
    )(a, w, x)


ROT_KS, ROT_KW = 0, NSA_KV_GROUPS


def _proj_rotated_body(h_ref, w_ref, c_ref, sa_ref, sb_ref, o_ref):
    acc = _dot(h_ref[...], w_ref[...])
    c, sa, sb = c_ref[...], sa_ref[...], sb_ref[...]
    for hh in range(o_ref.shape[0]):
        o_ref[hh] = _rope(acc[:, hh * HEAD_DIM:(hh + 1) * HEAD_DIM], c, sa, sb).astype(BF16)


def _proj_query_body(h_ref, w_ref, c_ref, sa_ref, sb_ref, o_ref):
    acc = _dot(h_ref[...], w_ref[...])
    c, sa, sb = c_ref[...], sa_ref[...], sb_ref[...]
    tile = o_ref.shape[-1]
    for hh in range(o_ref.shape[0]):
        seg = _rope(acc[:, hh * HEAD_DIM:(hh + 1) * HEAD_DIM], c, sa, sb) * (HEAD_DIM ** -0.5 * LOG2_E)
        for t in range(acc.shape[0] // tile):
            o_ref[hh, t] = seg[t * tile:(t + 1) * tile].T.astype(BF16)


def _proj_grouped_body(h_ref, w_ref, o_ref, rows_ref):
    acc = _dot(h_ref[...], w_ref[...])
    groups = acc.shape[0] // CMP_STRIDE
    for hh in range(o_ref.shape[0]):
        rows_ref[hh] = acc[:, hh * HEAD_DIM:(hh + 1) * HEAD_DIM]
        o_ref[hh] = jnp.concatenate([rows_ref[hh, pl.ds(l, groups, stride=CMP_STRIDE), :]
                                     for l in range(CMP_STRIDE)], axis=1).astype(BF16)


def _proj_flipped_body(h_ref, w_ref, o_ref):
    acc = _dot(h_ref[...], w_ref[...])
    tile = o_ref.shape[-1]
    for hh in range(o_ref.shape[0]):
        for t in range(acc.shape[0] // tile):
            o_ref[hh, t] = acc[t * tile:(t + 1) * tile, hh * HEAD_DIM:(hh + 1) * HEAD_DIM].T.astype(BF16)


def _proj(body, h, w, extra, extra_specs, out_block, out_dims, scratch, name, *, tm):
    m, d = h.shape
    n = w.shape[1]
    heads = n // HEAD_DIM
    return pl.pallas_call(
        body,
        grid=(m // tm,),
        in_specs=[pl.BlockSpec((tm, d), lambda i: (i, 0)), pl.BlockSpec((d, n), lambda i: (0, 0))] + extra_specs,
        out_specs=pl.BlockSpec((heads,) + out_block, lambda i: (0, i) + (0,) * (len(out_block) - 1)),
        out_shape=jax.ShapeDtypeStruct((heads,) + out_dims, BF16),
        scratch_shapes=scratch,
        compiler_params=_params(("parallel",)),
        name=name,
    )(h, w, *extra)


def _nsa_projections(h, w_in, tables, seq, *, tm, sel_tile):
    m = h.shape[0]
    hd = HEAD_DIM
    q_end = NSA_HEADS * hd
    kv = NSA_KV_GROUPS * hd
    kc, vc, ks, vs, kw, vw = (slice(q_end + i * kv, q_end + (i + 1) * kv) for i in range(6))
    w_in = w_in.astype(BF16)
    cast = lambda cols: jnp.concatenate([w_in[:, c] for c in cols], axis=1)
    tpb = seq // tm
    tab_spec = pl.BlockSpec((tm, hd), lambda i: (i % tpb, 0))
    queries = _proj(_proj_query_body, h, cast([slice(0, q_end)]), tables, [tab_spec] * 3,
                    (tm // Q_BLOCK, hd, Q_BLOCK), (m // Q_BLOCK, hd, Q_BLOCK), [], "nsa_proj_query", tm=tm)
    rotated = _proj(_proj_rotated_body, h, cast([ks, kw]), tables, [tab_spec] * 3,
                    (tm, hd), (m, hd), [], "nsa_proj_rotated", tm=tm)
    grouped = _proj(_proj_grouped_body, h, cast([kc, vc]), (), [],
                    (tm // CMP_STRIDE, CMP_STRIDE * hd), (m // CMP_STRIDE, CMP_STRIDE * hd),
                    [pltpu.VMEM((2 * NSA_KV_GROUPS, tm, hd), F32)], "nsa_proj_grouped", tm=tm)
    flipped = [_proj(_proj_flipped_body, h, cast([cols]), (), [], (tile_rows // tile, hd, tile),
                     (m // tile, hd, tile), [], "nsa_proj_flipped", tm=tile_rows)
               for cols, tile, tile_rows in ((vs, sel_tile, max(tm, sel_tile)), (vw, Q_BLOCK, tm))]
    return queries, rotated, grouped, flipped


CMP_CHUNK = LANES


def _compress_body(kr_ref, pe_ref, w1_ref, w2_ref, c_ref, sa_ref, sb_ref, o_ref):
    kr = kr_ref[0].astype(F32)
    rows = kr.shape[0]
    a = _dot((kr + pe_ref[0, 0]).astype(BF16), w1_ref[0, 0])
    b = _dot((kr + pe_ref[0, 1]).astype(BF16), w1_ref[0, 1])
    hid = jax.nn.gelu(a + pltpu.roll(b, rows - 1, 0))
    out = _dot(hid.astype(BF16), w2_ref[0])
    is_k = pl.program_id(0) == 0

    @pl.when(is_k)
    def _():
        roped = _rope(out, c_ref[...], sa_ref[...], sb_ref[...]).astype(BF16)
        for j in range(rows // CMP_CHUNK):
            o_ref[0, 0, 0, j] = roped[j * CMP_CHUNK:(j + 1) * CMP_CHUNK]

    @pl.when(jnp.logical_not(is_k))
    def _():
        for j in range(rows // CMP_CHUNK):
            o_ref[0, 0, 0, j] = out[j * CMP_CHUNK:(j + 1) * CMP_CHUNK].T.astype(BF16)


def _compress(kr, pe, w1, w2, tables, batch, seq):
    rows = seq // CMP_STRIDE
    kdim = CMP_STRIDE * HEAD_DIM
    hid = w1.shape[-1]
    tab_spec = pl.BlockSpec((rows, HEAD_DIM), lambda w, b, g: (0, 0))
    chunks = rows // CMP_CHUNK
    return pl.pallas_call(
        _compress_body,
        grid=(2, batch, NSA_KV_GROUPS),
        in_specs=[
            pl.BlockSpec((1, rows, kdim), lambda w, b, g: (NSA_KV_GROUPS * w + g, b, 0)),
            pl.BlockSpec((1, 2, 1, kdim), lambda w, b, g: (w, 0, 0, 0)),
            pl.BlockSpec((1, 2, kdim, hid), lambda w, b, g: (w, 0, 0, 0)),
            pl.BlockSpec((1, hid, HEAD_DIM), lambda w, b, g: (w, 0, 0)),
            tab_spec, tab_spec, tab_spec,
        ],
        out_specs=pl.BlockSpec((1, 1, 1, chunks, CMP_CHUNK, HEAD_DIM), lambda w, b, g: (w, b, g, 0, 0, 0)),
        out_shape=jax.ShapeDtypeStruct((2, batch, NSA_KV_GROUPS, chunks, CMP_CHUNK, HEAD_DIM), BF16),
        compiler_params=_params(("parallel", "parallel", "parallel")),
        name="nsa_compress",
    )(kr, pe, w1, w2, *tables)


def _attn_body(q_ref, ks_ref, vs_ref, kw_ref, vw_ref, kc_ref, vc_ref, gate_ref, e_ref,
               o_ref, qaug_ref, ps_ref, sa_ref, sb_ref, sd_ref, *, ns, tk):
    qi = pl.program_id(2)
    t0 = qi * Q_BLOCK
    cols = NSA_HPG * Q_BLOCK
    halves = qaug_ref.shape[0]
    for n in range(NSA_HPG):
        for h in range(halves):
            qaug_ref[h, 0:HEAD_DIM, n * Q_BLOCK:(n + 1) * Q_BLOCK] = q_ref[n, 0]
    q_t = qaug_ref[0, 0:HEAD_DIM, :]
    tq = t0 + (lax.broadcasted_iota(jnp.int32, (1, cols), 1) & (Q_BLOCK - 1))

    ncp = ps_ref.shape[0]
    step = min(2 * CMP_CHUNK, ncp)

    def compressed(rows):
        chunks = range(rows // CMP_CHUNK)
        s = _dot(jnp.concatenate([kc_ref[0, 0, 0, i] for i in chunks], axis=0), q_t)
        cend = lax.broadcasted_iota(jnp.int32, (rows, cols), 0) * CMP_STRIDE + (CMP_LEN - 1)
        s = jnp.where(cend <= tq, s, NEG_INF)
        p = jnp.exp2(s - jnp.max(s, axis=0, keepdims=True))
        acc = _dot(jnp.concatenate([vc_ref[0, 0, 0, i] for i in chunks], axis=1), p.astype(BF16))
        inv = jnp.where(tq >= CMP_LEN - 1, 1.0 / jnp.sum(p, axis=0, keepdims=True), 0.0)
        p = p * inv
        ps = p[:, 0:Q_BLOCK]
        for n in range(1, NSA_HPG):
            ps = ps + p[:, n * Q_BLOCK:(n + 1) * Q_BLOCK]
        ps_ref[0:rows, :] = ps
        if rows < ncp:
            ps_ref[rows:ncp, :] = jnp.zeros((ncp - rows, Q_BLOCK), F32)
        return (acc * inv,) + fast_rounds(rows // ratio)

    ratio = SEL_BLOCK // CMP_STRIDE
    nsp = halves * SEL_BIAS_BLOCKS
    n_forced = 3
    rounds = min(SEL_TOPK, ns) - n_forced

    def block_masks(nb):
        blk = lax.broadcasted_iota(jnp.int32, (nb, Q_BLOCK), 0)
        cur = (t0 + lax.broadcasted_iota(jnp.int32, (nb, Q_BLOCK), 1)) // SEL_BLOCK
        return blk, blk <= cur, (blk == 0) | (blk == cur) | (blk == cur - 1)

    def free_scores(nb):
        r = [ps_ref[pl.ds(j, nb, stride=ratio), :] for j in range(ratio)]
        blk, valid, forced = block_masks(nb)
        prev = jnp.where(blk == 0, 0.0, pltpu.roll(r[ratio - 1], 1, 0))
        free = valid & jnp.logical_not(forced)
        return blk, free, jnp.where(free, 2.0 * (r[0] + r[1] + r[2]) + r[3] + prev, NEG_INF)

    def all_blocks(taken):
        nb = taken.shape[0]
        return taken if nb == nsp else jnp.concatenate([taken, jnp.zeros((nsp - nb, Q_BLOCK), F32)], axis=0)

    def fast_rounds(nb):
        _, free, score = free_scores(nb)
        for _ in range(rounds):
            score = jnp.where(score == jnp.max(score, axis=0, keepdims=True), -jnp.inf, score)
        removed = score == -jnp.inf
        n_removed = jnp.sum(jnp.where(removed & free, 1.0, 0.0), axis=0, keepdims=True)
        cur_row = (t0 + lax.broadcasted_iota(jnp.int32, (1, Q_BLOCK), 1)) // SEL_BLOCK
        n_free = jnp.clip(cur_row + 1 - n_forced, 0, rounds).astype(F32)
        return all_blocks(jnp.where(removed, 1.0, 0.0)), jnp.sum(jnp.where(n_removed == n_free, 0.0, 1.0))

    def tie_breaking_rounds():
        blk, _, score = free_scores(ns)
        blk_f, taken = blk.astype(F32), jnp.zeros((ns, Q_BLOCK), F32)
        for _ in range(rounds):
            top = jnp.max(score, axis=0, keepdims=True)
            first = jnp.min(jnp.where(score == top, blk_f, float(ns)), axis=0, keepdims=True)
            hit = blk_f == first
            taken = jnp.where(hit, 1.0, taken)
            score = jnp.where(hit, -jnp.inf, score)
        return all_blocks(taken)

    variants = [functools.partial(compressed, rows) for rows in range(step, ncp + 1, step)]
    reach = jnp.minimum((t0 + Q_BLOCK - CMP_LEN) // (CMP_STRIDE * step), len(variants) - 1)
    o_c, taken, n_tied = lax.switch(reach, variants) if len(variants) > 1 else variants[0]()
    taken = lax.cond(n_tied == 0.0, lambda: taken, tie_breaking_rounds)
    _, valid, forced = block_masks(nsp)
    bias = jnp.where((forced | (taken > 0.0)) & valid, 0.0, NEG_INF).astype(BF16)

    for h in range(halves):
        bh = bias[h * SEL_BIAS_BLOCKS:(h + 1) * SEL_BIAS_BLOCKS]
        for n in range(NSA_HPG):
            qaug_ref[h, HEAD_DIM:HEAD_DIM + SEL_BIAS_BLOCKS, n * Q_BLOCK:(n + 1) * Q_BLOCK] = bh

    init = (jnp.full((1, cols), NEG_INF, F32), jnp.zeros((1, cols), F32),
            jnp.zeros((HEAD_DIM, cols), F32))

    n_win = WINDOW // Q_BLOCK + 1
    win_keys = n_win * Q_BLOCK
    w0 = pl.multiple_of(jnp.maximum(t0 - WINDOW, 0), Q_BLOCK)
    newest = tq - w0
    krow = lax.broadcasted_iota(jnp.int32, (win_keys, cols), 0)
    s_w = jnp.where(krow <= newest, _dot(kw_ref[0, pl.ds(w0, win_keys), :], q_t), NEG_INF)
    oldest = lax.broadcasted_iota(jnp.int32, (Q_BLOCK, cols), 0) > newest - WINDOW
    s_w = jnp.concatenate([jnp.where(oldest, s_w[:Q_BLOCK], NEG_INF), s_w[Q_BLOCK:]], axis=0)
    p_w = jnp.exp2(s_w - jnp.max(s_w, axis=0, keepdims=True))
    vw_t = jnp.concatenate([vw_ref[0, w0 // Q_BLOCK + w] for w in range(n_win)], axis=1)
    o_w = _dot(vw_t, p_w.astype(BF16)) * (1.0 / jnp.sum(p_w, axis=0, keepdims=True))

    def sel_scores(j, s_ref, diagonal=False):
        k0 = pl.multiple_of(j * tk, tk)
        e0 = pl.multiple_of(k0 % SEL_BIAS_KEYS, tk)
        kaug = jnp.concatenate([ks_ref[0, pl.ds(k0, tk), :], e_ref[pl.ds(e0, tk), :]], axis=1)
        s = _dot(kaug, qaug_ref[k0 // SEL_BIAS_KEYS])
        if diagonal:
            lk = lax.broadcasted_iota(jnp.int32, (tk, cols), 0)
            lq = lax.broadcasted_iota(jnp.int32, (tk, cols), 1) & (Q_BLOCK - 1)
            s = jnp.where(lk - lq <= t0 - k0, s, NEG_INF)
        s_ref[...] = s
        return jnp.max(s, axis=0, keepdims=True)

    def sel_update(s_ref, s_max, j, carry):
        m, l, acc = carry
        m_new = jnp.maximum(m, s_max)
        alpha = jnp.exp2(m - m_new)
        p = jnp.exp2(s_ref[...] - m_new)
        l = alpha * l + jnp.sum(p, axis=0, keepdims=True)
        acc = alpha * acc + _dot(vs_ref[0, j], p.astype(BF16))
        return m_new, l, acc

    n_full = t0 // tk
    last_full = jnp.maximum(n_full - 1, 0)
    max_d = sel_scores(n_full, sd_ref, diagonal=True)
    max_a = sel_scores(0, sa_ref)

    def sel_pair(j, carry):
        max_a, state = carry
        max_b = sel_scores(j + 1, sb_ref)
        state = sel_update(sa_ref, max_a, j, state)
        max_a = sel_scores(jnp.minimum(j + 2, last_full), sa_ref)
        return max_a, sel_update(sb_ref, max_b, j + 1, state)

    unroll = 4
    carry = lax.fori_loop(0, n_full // unroll,
                          lambda jj, c: sel_pair(unroll * jj + 2, sel_pair(unroll * jj, c)), (max_a, init))

    def finish(rest, carry):
        max_cur, state = carry
        first = n_full - rest
        bufs = (sa_ref, sb_ref)
        for i in range(rest):
            if i + 1 < rest:
                max_next = sel_scores(first + i + 1, bufs[(i + 1) % 2])
            state = sel_update(bufs[i % 2], max_cur, first + i, state)
            if i + 1 < rest:
                max_cur = max_next
        _, l_s, acc_s = sel_update(sd_ref, max_d, n_full, state)
        o_s = acc_s * (1.0 / l_s)
        g_t = gate_ref[...].T
        for n in range(NSA_HPG):
            c = slice(n * Q_BLOCK, (n + 1) * Q_BLOCK)
            o = (g_t[n:n + 1] * o_c[:, c] + g_t[NSA_HPG + n:NSA_HPG + n + 1] * o_s[:, c]
                 + g_t[2 * NSA_HPG + n:2 * NSA_HPG + n + 1] * o_w[:, c])
            o_ref[:, n * HEAD_DIM:(n + 1) * HEAD_DIM] = o.T.astype(BF16)
        return jnp.int32(0)

    lax.switch(n_full % unroll, [functools.partial(finish, rest) for rest in range(unroll)], carry)


def _nsa_attention(queries, rotated, vs_t, vw_t, cmp, gates, onehot, batch, seq, *, tk):
    nq = seq // Q_BLOCK
    ns = seq // SEL_BLOCK
    ncp = seq // CMP_STRIDE
    halves = -(-ns // SEL_BIAS_BLOCKS)
    m = batch * seq
    once = pl.Buffered(1)

    def slab(first):
        return pl.BlockSpec((1, seq, HEAD_DIM), lambda b, g, qi: (first + g, b, 0), pipeline_mode=once)

    def slab_t(tile):
        return pl.BlockSpec((1, seq // tile, HEAD_DIM, tile), lambda b, g, qi: (g, b, 0, 0), pipeline_mode=once)

    def cmp_spec(which):
        return pl.BlockSpec((1, 1, 1, ncp // CMP_CHUNK, CMP_CHUNK, HEAD_DIM),
                            lambda b, g, qi: (which, b, g, 0, 0, 0))

    return pl.pallas_call(
        functools.partial(_attn_body, ns=ns, tk=tk),
        grid=(batch, NSA_KV_GROUPS, nq),
        in_specs=[
            pl.BlockSpec((NSA_HPG, 1, HEAD_DIM, Q_BLOCK), lambda b, g, qi: (g, b * nq + qi, 0, 0)),
            slab(ROT_KS), slab_t(tk), slab(ROT_KW), slab_t(Q_BLOCK),
            cmp_spec(0), cmp_spec(1),
            pl.BlockSpec((Q_BLOCK, LANES), lambda b, g, qi: (b * nq + qi, g)),
            pl.BlockSpec(onehot.shape, lambda b, g, qi: (0, 0), pipeline_mode=once),
        ],
        out_specs=pl.BlockSpec((Q_BLOCK, NSA_HPG * HEAD_DIM), lambda b, g, qi: (b * nq + qi, g)),
        out_shape=jax.ShapeDtypeStruct((m, NSA_HEADS * HEAD_DIM), BF16),
        scratch_shapes=[
            pltpu.VMEM((halves, 2 * HEAD_DIM, NSA_HPG * Q_BLOCK), BF16),
            pltpu.VMEM((ncp, Q_BLOCK), F32),
            pltpu.VMEM((tk, NSA_HPG * Q_BLOCK), F32),
            pltpu.VMEM((tk, NSA_HPG * Q_BLOCK), F32),
            pltpu.VMEM((tk, NSA_HPG * Q_BLOCK), F32),
        ],
        compiler_params=_params(("parallel", "parallel", "arbitrary")),
        name="nsa_attention",
    )(queries, rotated, vs_t, rotated, vw_t, cmp, cmp, gates, onehot)


def _sel_onehot(seq):
    keys = np.arange(min(seq, SEL_BIAS_KEYS))
    onehot = (keys[:, None] // SEL_BLOCK == np.arange(SEL_BIAS_BLOCKS)[None, :]).astype(np.float32)
    return jnp.asarray(onehot, BF16)


def _gate_weight(w_gl):
    d = w_gl.shape[0]
    w = w_gl.reshape(d, 3, NSA_KV_GROUPS, NSA_HPG).transpose(0, 2, 1, 3).reshape(d, NSA_KV_GROUPS, 3 * NSA_HPG)
    w = jnp.pad(w, ((0, 0), (0, 0), (0, LANES - 3 * NSA_HPG)))
    return w.reshape(d, NSA_KV_GROUPS * LANES)


def _nsa_layer(x, h, w_in, kc_pe, kc_w1, kc_w2, vc_pe, vc_w1, vc_w2, w_out, batch, seq, *, tk=1024):
    assert SEL_BLOCK == 4 * CMP_STRIDE and CMP_LEN == 2 * CMP_STRIDE and seq % max(tk, CMP_STRIDE * CMP_CHUNK) == 0
    n_main = NSA_HEADS * HEAD_DIM + 6 * NSA_KV_GROUPS * HEAD_DIM
    queries, rotated, kr, (vs_t, vw_t) = _nsa_projections(h, w_in, _rope_tables(jnp.arange(seq)), seq, tm=512, sel_tile=tk)
    gates = _mm_sigmoid(h, _gate_weight(w_in[:, n_main:]).astype(BF16), tm=1024)
    rows = seq // CMP_STRIDE
    cmp_tables = _rope_tables(jnp.arange(rows) * CMP_STRIDE + CMP_LEN - 1)
    half = CMP_LEN // 2
    pe = jnp.stack([kc_pe, vc_pe]).reshape(2, 2, 1, half * HEAD_DIM)
    w1 = jnp.stack([kc_w1, vc_w1]).astype(BF16)
    w1 = w1.reshape(2, 2, half * HEAD_DIM, w1.shape[-1])
    w2 = jnp.stack([kc_w2, vc_w2]).astype(BF16)
    cmp = _compress(kr, pe, w1, w2, cmp_tables, batch, seq)
    att = _nsa_attention(queries, rotated, vs_t, vw_t, cmp, gates, _sel_onehot(seq), batch, seq, tk=tk)
    return _mm_residual(att, w_out.astype(BF16), x, tm=512, tn=w_out.shape[1])


def kernel(x, p, norm_mix, norm_ffn, norm_ple, ffn_up, ffn_down, ple_proj, ple_gate, gm_in, gm_ln_g, gm_ln_b, gm_ws, gm_bs, gm_out, nsa_in, nsa_kc_pe, nsa_kc_w1, nsa_kc_w2, nsa_vc_pe, nsa_vc_w1, nsa_vc_w2, nsa_out, final_norm):
    batch, seq, d = x.shape
    m = batch * seq
    depth = p.shape[0]
    xf = x.reshape(m, d)
    row = lambda v: v.reshape(1, -1)
    p_rows = p.reshape(depth, m, -1)
    ffn_up_bf, ffn_down_bf = ffn_up.astype(BF16), ffn_down.astype(BF16)
    ple_gate_bf, ple_proj_bf = ple_gate.astype(BF16), ple_proj.astype(BF16)
    for i in range(depth):
        j = i // 2
        if i % 2 == 0:
            z = _norm_mm(xf, row(norm_mix[i]), gm_in[j].astype(BF16), act="gelu", out_dtype=BF16,
                         tm=512, tn=gm_in.shape[-1])
            xf = _gmlp_gate_out(z, xf, row(gm_ln_g[j]), row(gm_ln_b[j]), gm_ws[j], gm_bs[j].T,
                                gm_out[j].astype(BF16), tm=512)
        else:
            xf = _nsa_layer(xf, h_mix, nsa_in[j], nsa_kc_pe[j], nsa_kc_w1[j], nsa_kc_w2[j],
                            nsa_vc_pe[j], nsa_vc_w1[j], nsa_vc_w2[j], nsa_out[j], batch, seq)
        xf = _ffn(xf, row(norm_ffn[i]), ffn_up_bf, ffn_down_bf, i, tm=1024, tf=512)
        if i == depth - 1:
            post, post_g = "final", final_norm
        elif (i + 1) % 2 == 1:
            post, post_g = "next", norm_mix[i + 1]
        else:
            post, post_g = "none", final_norm
        out = _ple(xf, p_rows, row(norm_ple[i]), ple_gate_bf, ple_proj_bf, row(post_g), i, post=post, tm=512)
        xf, h_mix = out if post == "next" else (out, None)
    return xf.reshape(batch, seq, d)
```

```python
import functools

import numpy as np
import jax
import jax.numpy as jnp
from jax import lax
from jax.experimental import pallas as pl
from jax.experimental.pallas import tpu as pltpu

F32 = jnp.float32
BF16 = jnp.bfloat16

EPS = 1e-6
HEAD_DIM = 128
NSA_HEADS = 16
NSA_KV_GROUPS = 4
NSA_HPG = NSA_HEADS // NSA_KV_GROUPS
ROT_DIM = HEAD_DIM // 4
ROPE_THETA = 500000.0
CMP_LEN = 32
CMP_STRIDE = 16
SEL_BLOCK = 64
SEL_TOPK = 16
WINDOW = 512
Q_BLOCK = 128
GM_CHUNK = 128
NEG_INF = -1e30
LOG2_E = 1.4426950408889634

LANES = 128
SEL_BIAS_BLOCKS = LANES
SEL_BIAS_KEYS = SEL_BIAS_BLOCKS * SEL_BLOCK
VMEM_LIMIT = 56 * 1024 * 1024


def _params(sem):
    return pltpu.CompilerParams(dimension_semantics=sem, vmem_limit_bytes=VMEM_LIMIT)


def _rmsnorm(x, g):
    return x * lax.rsqrt(jnp.mean(x * x, axis=-1, keepdims=True) + EPS) * g


def _dot(a, b):
    return jnp.dot(a, b, preferred_element_type=F32)


def _rope(x, c, sa, sb):
    return x * c + pltpu.roll(x, LANES - ROT_DIM // 2, 1) * sa + pltpu.roll(x, ROT_DIM // 2, 1) * sb


def _rope_tables(pos):
    half = ROT_DIM // 2
    inv = jnp.power(jnp.float32(ROPE_THETA), -jnp.arange(half, dtype=F32) * 2.0 / ROT_DIM)
    ang = pos.astype(F32)[:, None] * inv[None, :]
    cos, sin = jnp.cos(ang), jnp.sin(ang)
    n = pos.shape[0]
    rest = HEAD_DIM - ROT_DIM
    c = jnp.concatenate([cos, cos, jnp.ones((n, rest), F32)], axis=1)
    sa = jnp.concatenate([-sin, jnp.zeros((n, half + rest), F32)], axis=1)
    sb = jnp.concatenate([jnp.zeros((n, half), F32), sin, jnp.zeros((n, rest), F32)], axis=1)
    return c, sa, sb


def _norm_gelu_mm_body(x_ref, g_ref, w_ref, o_ref):
    h = _rmsnorm(x_ref[...], g_ref[...]).astype(BF16)
    o_ref[...] = jax.nn.gelu(_dot(h, w_ref[...])).astype(o_ref.dtype)


def _norm_gelu_mm(x, g, w, *, tm):
    m, d = x.shape
    n = w.shape[1]
    return pl.pallas_call(
        _norm_gelu_mm_body,
        grid=(m // tm,),
        in_specs=[
            pl.BlockSpec((tm, d), lambda i: (i, 0)),
            pl.BlockSpec((1, d), lambda i: (0, 0)),
            pl.BlockSpec((d, n), lambda i: (0, 0)),
        ],
        out_specs=pl.BlockSpec((tm, n), lambda i: (i, 0)),
        out_shape=jax.ShapeDtypeStruct((m, n), BF16),
        compiler_params=_params(("parallel",)),
        name="norm_gelu_mm",
    )(x, g, w)


def _mm_sigmoid_body(h_ref, w_ref, o_ref):
    o_ref[...] = jax.nn.sigmoid(_dot(h_ref[...], w_ref[...]))


def _mm_sigmoid(h, w, *, tm):
    m, d = h.shape
    n = w.shape[1]
    return pl.pallas_call(
        _mm_sigmoid_body,
        grid=(m // tm,),
        in_specs=[pl.BlockSpec((tm, d), lambda i: (i, 0)), pl.BlockSpec((d, n), lambda i: (0, 0))],
        out_specs=pl.BlockSpec((tm, n), lambda i: (i, 0)),
        out_shape=jax.ShapeDtypeStruct((m, n), F32),
        compiler_params=_params(("parallel",)),
        name="mm_sigmoid",
    )(h, w)


def _gmlp_body(z_ref, x_ref, lg_ref, lb_ref, ws_ref, bs_ref, wo_ref, o_ref, y_ref, *, tm, width):
    groups = ws_ref.shape[0]
    gd = width // groups
    u = z_ref[:, :width]
    v = z_ref[:, width:].astype(F32)
    mu = jnp.mean(v, axis=-1, keepdims=True)
    var = jnp.mean(jnp.square(v - mu), axis=-1, keepdims=True)
    vn = ((v - mu) * lax.rsqrt(var + EPS) * lg_ref[...] + lb_ref[...]).astype(BF16)
    r = lax.broadcasted_iota(jnp.int32, (GM_CHUNK, GM_CHUNK), 0)
    c = lax.broadcasted_iota(jnp.int32, (GM_CHUNK, GM_CHUNK), 1)
    causal = c <= r
    for g in range(groups):
        wg = jnp.where(causal, ws_ref[g], 0.0).astype(BF16)
        bg = bs_ref[:, g:g + 1]
        cols = slice(g * gd, (g + 1) * gd)
        chunks = [slice(ch * GM_CHUNK, (ch + 1) * GM_CHUNK) for ch in range(tm // GM_CHUNK)]
        sv = _dot(wg, jnp.concatenate([vn[rows, cols] for rows in chunks], axis=1))
        for ch, rows in enumerate(chunks):
            y_ref[rows, cols] = (u[rows, cols].astype(F32) * (sv[:, ch * gd:(ch + 1) * gd] + bg)).astype(BF16)
    o_ref[...] = x_ref[...] + _dot(y_ref[...], wo_ref[...])


def _gmlp_gate_out(z, x, ln_g, ln_b, ws, bs, w_out, *, tm):
    m, d = x.shape
    width = z.shape[1] // 2
    groups = ws.shape[0]
    return pl.pallas_call(
        functools.partial(_gmlp_body, tm=tm, width=width),
        grid=(m // tm,),
        in_specs=[
            pl.BlockSpec((tm, 2 * width), lambda i: (i, 0)),
            pl.BlockSpec((tm, d), lambda i: (i, 0)),
            pl.BlockSpec((1, width), lambda i: (0, 0)),
            pl.BlockSpec((1, width), lambda i: (0, 0)),
            pl.BlockSpec((groups, GM_CHUNK, GM_CHUNK), lambda i: (0, 0, 0)),
            pl.BlockSpec((GM_CHUNK, groups), lambda i: (0, 0)),
            pl.BlockSpec((width, d), lambda i: (0, 0)),
        ],
        out_specs=pl.BlockSpec((tm, d), lambda i: (i, 0)),
        out_shape=jax.ShapeDtypeStruct((m, d), F32),
        scratch_shapes=[pltpu.VMEM((tm, width), BF16)],
        compiler_params=_params(("parallel",)),
        name="gmlp_gate_out",
    )(z, x, ln_g, ln_b, ws, bs, w_out)


def _ffn_body(x_ref, g_ref, wu_ref, wd_ref, o_ref, h_ref):
    f = pl.program_id(1)

    @pl.when(f == 0)
    def _():
        x = x_ref[...]
        h_ref[...] = _rmsnorm(x, g_ref[...]).astype(BF16)
        o_ref[...] = x

    a = jnp.square(jnp.maximum(_dot(h_ref[...], wu_ref[0]), 0.0)).astype(BF16)
    o_ref[...] += _dot(a, wd_ref[0])


def _ffn(x, g, w_up, w_down, layer, *, tm, tf):
    m, d = x.shape
    ff = w_up.shape[2]
    return pl.pallas_call(
        _ffn_body,
        grid=(m // tm, ff // tf),
        in_specs=[
            pl.BlockSpec((tm, d), lambda i, f: (i, 0)),
            pl.BlockSpec((1, d), lambda i, f: (0, 0)),
            pl.BlockSpec((1, d, tf), lambda i, f: (layer, 0, f)),
            pl.BlockSpec((1, tf, d), lambda i, f: (layer, f, 0)),
        ],
        out_specs=pl.BlockSpec((tm, d), lambda i, f: (i, 0)),
        out_shape=jax.ShapeDtypeStruct((m, d), F32),
        scratch_shapes=[pltpu.VMEM((tm, d), BF16)],
        compiler_params=_params(("parallel", "arbitrary")),
        name="ffn",
    )(x, g, w_up, w_down)


def _ple_body(x_ref, p_ref, g_ref, wg_ref, wp_ref, pg_ref, *o_refs, post):
    x = x_ref[...]
    h = _rmsnorm(x, g_ref[...]).astype(BF16)
    gate = jax.nn.sigmoid(_dot(h, wg_ref[0]))
    y = x + gate * _dot(p_ref[0].astype(BF16), wp_ref[0])
    if post == "final":
        o_refs[0][...] = _rmsnorm(y, pg_ref[...])
    else:
        o_refs[0][...] = y
    if post == "next":
        o_refs[1][...] = _rmsnorm(y, pg_ref[...]).astype(BF16)


def _ple(x, p, g, w_gate, w_proj, post_g, layer, *, post, tm):
    m, d = x.shape
    pd = p.shape[2]
    row_spec = pl.BlockSpec((tm, d), lambda i: (i, 0))
    out_specs, out_shape = row_spec, jax.ShapeDtypeStruct((m, d), F32)
    if post == "next":
        out_specs, out_shape = [row_spec, row_spec], [out_shape, jax.ShapeDtypeStruct((m, d), BF16)]
    return pl.pallas_call(
        functools.partial(_ple_body, post=post),
        grid=(m // tm,),
        in_specs=[
            row_spec,
            pl.BlockSpec((1, tm, pd), lambda i: (layer, i, 0)),
            pl.BlockSpec((1, d), lambda i: (0, 0)),
            pl.BlockSpec((1, d, d), lambda i: (layer, 0, 0)),
            pl.BlockSpec((1, pd, d), lambda i: (layer, 0, 0)),
            pl.BlockSpec((1, d), lambda i: (0, 0)),
        ],
        out_specs=out_specs,
        out_shape=out_shape,
        compiler_params=_params(("parallel",)),
        name="ple_" + post,
    )(x, p, g, w_gate, w_proj, post_g)


def _mm_res_body(a_ref, w_ref, x_ref, o_ref):
    o_ref[...] = x_ref[...] + _dot(a_ref[...], w_ref[...])


def _mm_residual(a, w, x, *, tm):
    m, k = a.shape
    n = w.shape[1]
    return pl.pallas_call(
        _mm_res_body,
        grid=(m // tm,),
        in_specs=[
            pl.BlockSpec((tm, k), lambda i: (i, 0)),
            pl.BlockSpec((k, n), lambda i: (0, 0)),
            pl.BlockSpec((tm, n), lambda i: (i, 0)),
        ],
        out_specs=pl.BlockSpec((tm, n), lambda i: (i, 0)),
        out_shape=jax.ShapeDtypeStruct((m, n), F32),
        compiler_params=_params(("parallel",)),
        name="mm_residual",
    )(a, w, x)


ROT_KS, ROT_KW = 0, NSA_KV_GROUPS


def _proj_rotated_body(h_ref, w_ref, c_ref, sa_ref, sb_ref, o_ref):
    acc = _dot(h_ref[...], w_ref[...])
    c, sa, sb = c_ref[...], sa_ref[...], sb_ref[...]
    for hh in range(o_ref.shape[0]):
        o_ref[hh] = _rope(acc[:, hh * HEAD_DIM:(hh + 1) * HEAD_DIM], c, sa, sb).astype(BF16)


def _proj_query_body(h_ref, w_ref, c_ref, sa_ref, sb_ref, o_ref):
    acc = _dot(h_ref[...], w_ref[...])
    c, sa, sb = c_ref[...], sa_ref[...], sb_ref[...]
    tile = o_ref.shape[-1]
    for hh in range(o_ref.shape[0]):
        seg = _rope(acc[:, hh * HEAD_DIM:(hh + 1) * HEAD_DIM], c, sa, sb) * (HEAD_DIM ** -0.5 * LOG2_E)
        for t in range(acc.shape[0] // tile):
            o_ref[hh, t] = seg[t * tile:(t + 1) * tile].T.astype(BF16)


def _proj_grouped_body(h_ref, w_ref, o_ref, rows_ref):
    acc = _dot(h_ref[...], w_ref[...])
    groups = acc.shape[0] // CMP_STRIDE
    for hh in range(o_ref.shape[0]):
        rows_ref[hh] = acc[:, hh * HEAD_DIM:(hh + 1) * HEAD_DIM]
        o_ref[hh] = jnp.concatenate([rows_ref[hh, pl.ds(l, groups, stride=CMP_STRIDE), :]
                                     for l in range(CMP_STRIDE)], axis=1).astype(BF16)


def _proj_flipped_body(h_ref, w_ref, o_ref):
    acc = _dot(h_ref[...], w_ref[...])
    tile = o_ref.shape[-1]
    for hh in range(o_ref.shape[0]):
        for t in range(acc.shape[0] // tile):
            o_ref[hh, t] = acc[t * tile:(t + 1) * tile, hh * HEAD_DIM:(hh + 1) * HEAD_DIM].T.astype(BF16)


def _proj(body, h, w, extra, extra_specs, out_block, out_dims, scratch, name, *, tm):
    m, d = h.shape
    n = w.shape[1]
    heads = n // HEAD_DIM
    return pl.pallas_call(
        body,
        grid=(m // tm,),
        in_specs=[pl.BlockSpec((tm, d), lambda i: (i, 0)), pl.BlockSpec((d, n), lambda i: (0, 0))] + extra_specs,
        out_specs=pl.BlockSpec((heads,) + out_block, lambda i: (0, i) + (0,) * (len(out_block) - 1)),
        out_shape=jax.ShapeDtypeStruct((heads,) + out_dims, BF16),
        scratch_shapes=scratch,
        compiler_params=_params(("parallel",)),
        name=name,
    )(h, w, *extra)


def _nsa_projections(h, w_in, tables, seq, *, tm, sel_tile):
    m = h.shape[0]
    hd = HEAD_DIM
    q_end = NSA_HEADS * hd
    kv = NSA_KV_GROUPS * hd
    kc, vc, ks, vs, kw, vw = (slice(q_end + i * kv, q_end + (i + 1) * kv) for i in range(6))
    w_in = w_in.astype(BF16)
    columns = lambda cols: jnp.concatenate([w_in[:, c] for c in cols], axis=1)
    tpb = seq // tm
    tab_spec = pl.BlockSpec((tm, hd), lambda i: (i % tpb, 0))
    queries = _proj(_proj_query_body, h, columns([slice(0, q_end)]), tables, [tab_spec] * 3,
                    (tm // Q_BLOCK, hd, Q_BLOCK), (m // Q_BLOCK, hd, Q_BLOCK), [], "nsa_proj_query", tm=tm)
    rotated = _proj(_proj_rotated_body, h, columns([ks, kw]), tables, [tab_spec] * 3,
                    (tm, hd), (m, hd), [], "nsa_proj_rotated", tm=tm)
    grouped = _proj(_proj_grouped_body, h, columns([kc, vc]), (), [],
                    (tm // CMP_STRIDE, CMP_STRIDE * hd), (m // CMP_STRIDE, CMP_STRIDE * hd),
                    [pltpu.VMEM((2 * NSA_KV_GROUPS, tm, hd), F32)], "nsa_proj_grouped", tm=tm)
    flipped = [_proj(_proj_flipped_body, h, columns([cols]), (), [], (tile_rows // tile, hd, tile),
                     (m // tile, hd, tile), [], "nsa_proj_flipped", tm=tile_rows)
               for cols, tile, tile_rows in ((vs, sel_tile, max(tm, sel_tile)), (vw, Q_BLOCK, tm))]
    return queries, rotated, grouped, flipped


CMP_CHUNK = LANES


def _compress_body(kr_ref, pe_ref, w1_ref, w2_ref, c_ref, sa_ref, sb_ref, o_ref):
    kr = kr_ref[0].astype(F32)
    rows = kr.shape[0]
    a = _dot((kr + pe_ref[0, 0]).astype(BF16), w1_ref[0, 0])
    b = _dot((kr + pe_ref[0, 1]).astype(BF16), w1_ref[0, 1])
    hid = jax.nn.gelu(a + pltpu.roll(b, rows - 1, 0))
    out = _dot(hid.astype(BF16), w2_ref[0])
    is_k = pl.program_id(0) == 0

    @pl.when(is_k)
    def _():
        roped = _rope(out, c_ref[...], sa_ref[...], sb_ref[...]).astype(BF16)
        for j in range(rows // CMP_CHUNK):
            o_ref[0, 0, 0, j] = roped[j * CMP_CHUNK:(j + 1) * CMP_CHUNK]

    @pl.when(jnp.logical_not(is_k))
    def _():
        for j in range(rows // CMP_CHUNK):
            o_ref[0, 0, 0, j] = out[j * CMP_CHUNK:(j + 1) * CMP_CHUNK].T.astype(BF16)


def _compress(kr, pe, w1, w2, tables, batch, seq):
    rows = seq // CMP_STRIDE
    kdim = CMP_STRIDE * HEAD_DIM
    hid = w1.shape[-1]
    tab_spec = pl.BlockSpec((rows, HEAD_DIM), lambda w, b, g: (0, 0))
    chunks = rows // CMP_CHUNK
    return pl.pallas_call(
        _compress_body,
        grid=(2, batch, NSA_KV_GROUPS),
        in_specs=[
            pl.BlockSpec((1, rows, kdim), lambda w, b, g: (NSA_KV_GROUPS * w + g, b, 0)),
            pl.BlockSpec((1, 2, 1, kdim), lambda w, b, g: (w, 0, 0, 0)),
            pl.BlockSpec((1, 2, kdim, hid), lambda w, b, g: (w, 0, 0, 0)),
            pl.BlockSpec((1, hid, HEAD_DIM), lambda w, b, g: (w, 0, 0)),
            tab_spec, tab_spec, tab_spec,
        ],
        out_specs=pl.BlockSpec((1, 1, 1, chunks, CMP_CHUNK, HEAD_DIM), lambda w, b, g: (w, b, g, 0, 0, 0)),
        out_shape=jax.ShapeDtypeStruct((2, batch, NSA_KV_GROUPS, chunks, CMP_CHUNK, HEAD_DIM), BF16),
        compiler_params=_params(("parallel", "parallel", "parallel")),
        name="nsa_compress",
    )(kr, pe, w1, w2, *tables)


def _attn_body(q_ref, ks_ref, vs_ref, kw_ref, vw_ref, kc_ref, vc_ref, gate_ref, e_ref,
               o_ref, qaug_ref, ps_ref, sa_ref, sb_ref, sd_ref, *, ns, tk):
    qi = pl.program_id(2)
    t0 = qi * Q_BLOCK
    cols = NSA_HPG * Q_BLOCK
    halves = qaug_ref.shape[0]
    for n in range(NSA_HPG):
        for h in range(halves):
            qaug_ref[h, 0:HEAD_DIM, n * Q_BLOCK:(n + 1) * Q_BLOCK] = q_ref[n, 0]
    q_t = qaug_ref[0, 0:HEAD_DIM, :]
    tq = t0 + (lax.broadcasted_iota(jnp.int32, (1, cols), 1) & (Q_BLOCK - 1))

    ncp = ps_ref.shape[0]
    step = min(2 * CMP_CHUNK, ncp)

    def compressed(rows):
        chunks = range(rows // CMP_CHUNK)
        s = _dot(jnp.concatenate([kc_ref[0, 0, 0, i] for i in chunks], axis=0), q_t)
        cend = lax.broadcasted_iota(jnp.int32, (rows, cols), 0) * CMP_STRIDE + (CMP_LEN - 1)
        s = jnp.where(cend <= tq, s, NEG_INF)
        p = jnp.exp2(s - jnp.max(s, axis=0, keepdims=True))
        acc = _dot(jnp.concatenate([vc_ref[0, 0, 0, i] for i in chunks], axis=1), p.astype(BF16))
        inv = jnp.where(tq >= CMP_LEN - 1, 1.0 / jnp.sum(p, axis=0, keepdims=True), 0.0)
        p = p * inv
        ps = p[:, 0:Q_BLOCK]
        for n in range(1, NSA_HPG):
            ps = ps + p[:, n * Q_BLOCK:(n + 1) * Q_BLOCK]
        ps_ref[0:rows, :] = ps
        if rows < ncp:
            ps_ref[rows:ncp, :] = jnp.zeros((ncp - rows, Q_BLOCK), F32)
        return (acc * inv,) + fast_rounds(rows // ratio)

    ratio = SEL_BLOCK // CMP_STRIDE
    nsp = halves * SEL_BIAS_BLOCKS
    n_forced = 3
    rounds = min(SEL_TOPK, ns) - n_forced

    def block_masks(nb):
        blk = lax.broadcasted_iota(jnp.int32, (nb, Q_BLOCK), 0)
        cur = (t0 + lax.broadcasted_iota(jnp.int32, (nb, Q_BLOCK), 1)) // SEL_BLOCK
        return blk, blk <= cur, (blk == 0) | (blk == cur) | (blk == cur - 1)

    def free_scores(nb):
        r = [ps_ref[pl.ds(j, nb, stride=ratio), :] for j in range(ratio)]
        blk, valid, forced = block_masks(nb)
        prev = jnp.where(blk == 0, 0.0, pltpu.roll(r[ratio - 1], 1, 0))
        free = valid & jnp.logical_not(forced)
        return blk, free, jnp.where(free, 2.0 * (r[0] + r[1] + r[2]) + r[3] + prev, NEG_INF)

    def all_blocks(taken):
        nb = taken.shape[0]
        return taken if nb == nsp else jnp.concatenate([taken, jnp.zeros((nsp - nb, Q_BLOCK), F32)], axis=0)

    def fast_rounds(nb):
        _, free, score = free_scores(nb)
        for _ in range(rounds):
            score = jnp.where(score == jnp.max(score, axis=0, keepdims=True), -jnp.inf, score)
        removed = score == -jnp.inf
        n_removed = jnp.sum(jnp.where(removed & free, 1.0, 0.0), axis=0, keepdims=True)
        cur_row = (t0 + lax.broadcasted_iota(jnp.int32, (1, Q_BLOCK), 1)) // SEL_BLOCK
        n_free = jnp.clip(cur_row + 1 - n_forced, 0, rounds).astype(F32)
        return all_blocks(jnp.where(removed, 1.0, 0.0)), jnp.sum(jnp.where(n_removed == n_free, 0.0, 1.0))

    def tie_breaking_rounds():
        blk, _, score = free_scores(ns)
        blk_f, taken = blk.astype(F32), jnp.zeros((ns, Q_BLOCK), F32)
        for _ in range(rounds):
            top = jnp.max(score, axis=0, keepdims=True)
            first = jnp.min(jnp.where(score == top, blk_f, float(ns)), axis=0, keepdims=True)
            hit = blk_f == first
            taken = jnp.where(hit, 1.0, taken)
            score = jnp.where(hit, -jnp.inf, score)
        return all_blocks(taken)

    variants = [functools.partial(compressed, rows) for rows in range(step, ncp + 1, step)]
    reach = jnp.minimum((t0 + Q_BLOCK - CMP_LEN) // (CMP_STRIDE * step), len(variants) - 1)
    o_c, taken, n_tied = lax.switch(reach, variants) if len(variants) > 1 else variants[0]()
    taken = lax.cond(n_tied == 0.0, lambda: taken, tie_breaking_rounds)
    _, valid, forced = block_masks(nsp)
    bias = jnp.where((forced | (taken > 0.0)) & valid, 0.0, NEG_INF).astype(BF16)

    for h in range(halves):
        bh = bias[h * SEL_BIAS_BLOCKS:(h + 1) * SEL_BIAS_BLOCKS]
        for n in range(NSA_HPG):
            qaug_ref[h, HEAD_DIM:HEAD_DIM + SEL_BIAS_BLOCKS, n * Q_BLOCK:(n + 1) * Q_BLOCK] = bh

    init = (jnp.full((1, cols), NEG_INF, F32), jnp.zeros((1, cols), F32),
            jnp.zeros((HEAD_DIM, cols), F32))

    n_win = WINDOW // Q_BLOCK + 1
    win_keys = n_win * Q_BLOCK
    w0 = pl.multiple_of(jnp.maximum(t0 - WINDOW, 0), Q_BLOCK)
    newest = tq - w0
    krow = lax.broadcasted_iota(jnp.int32, (win_keys, cols), 0)
    s_w = jnp.where(krow <= newest, _dot(kw_ref[0, pl.ds(w0, win_keys), :], q_t), NEG_INF)
    oldest = lax.broadcasted_iota(jnp.int32, (Q_BLOCK, cols), 0) > newest - WINDOW
    s_w = jnp.concatenate([jnp.where(oldest, s_w[:Q_BLOCK], NEG_INF), s_w[Q_BLOCK:]], axis=0)
    p_w = jnp.exp2(s_w - jnp.max(s_w, axis=0, keepdims=True))
    vw_t = jnp.concatenate([vw_ref[0, w0 // Q_BLOCK + w] for w in range(n_win)], axis=1)
    o_w = _dot(vw_t, p_w.astype(BF16)) * (1.0 / jnp.sum(p_w, axis=0, keepdims=True))

    def sel_scores(j, s_ref, diagonal=False):
        k0 = pl.multiple_of(j * tk, tk)
        e0 = pl.multiple_of(k0 % SEL_BIAS_KEYS, tk)
        kaug = jnp.concatenate([ks_ref[0, pl.ds(k0, tk), :], e_ref[pl.ds(e0, tk), :]], axis=1)
        s = _dot(kaug, qaug_ref[k0 // SEL_BIAS_KEYS])
        if diagonal:
            lk = lax.broadcasted_iota(jnp.int32, (tk, cols), 0)
            lq = lax.broadcasted_iota(jnp.int32, (tk, cols), 1) & (Q_BLOCK - 1)
            s = jnp.where(lk - lq <= t0 - k0, s, NEG_INF)
        s_ref[...] = s
        return jnp.max(s, axis=0, keepdims=True)

    def sel_update(s_ref, s_max, j, carry):
        m, l, acc = carry
        m_new = jnp.maximum(m, s_max)
        alpha = jnp.exp2(m - m_new)
        p = jnp.exp2(s_ref[...] - m_new)
        l = alpha * l + jnp.sum(p, axis=0, keepdims=True)
        acc = alpha * acc + _dot(vs_ref[0, j], p.astype(BF16))
        return m_new, l, acc

    n_full = t0 // tk
    last_full = jnp.maximum(n_full - 1, 0)
    max_d = sel_scores(n_full, sd_ref, diagonal=True)
    max_a = sel_scores(0, sa_ref)

    def sel_pair(j, carry):
        max_a, state = carry
        max_b = sel_scores(j + 1, sb_ref)
        state = sel_update(sa_ref, max_a, j, state)
        max_a = sel_scores(jnp.minimum(j + 2, last_full), sa_ref)
        return max_a, sel_update(sb_ref, max_b, j + 1, state)

    unroll = 4
    carry = lax.fori_loop(0, n_full // unroll,
                          lambda jj, c: sel_pair(unroll * jj + 2, sel_pair(unroll * jj, c)), (max_a, init))

    def finish(rest, carry):
        max_cur, state = carry
        first = n_full - rest
        bufs = (sa_ref, sb_ref)
        for i in range(rest):
            if i + 1 < rest:
                max_next = sel_scores(first + i + 1, bufs[(i + 1) % 2])
            state = sel_update(bufs[i % 2], max_cur, first + i, state)
            if i + 1 < rest:
                max_cur = max_next
        _, l_s, acc_s = sel_update(sd_ref, max_d, n_full, state)
        o_s = acc_s * (1.0 / l_s)
        g_t = gate_ref[...].T
        for n in range(NSA_HPG):
            c = slice(n * Q_BLOCK, (n + 1) * Q_BLOCK)
            o = (g_t[n:n + 1] * o_c[:, c] + g_t[NSA_HPG + n:NSA_HPG + n + 1] * o_s[:, c]
                 + g_t[2 * NSA_HPG + n:2 * NSA_HPG + n + 1] * o_w[:, c])
            o_ref[:, n * HEAD_DIM:(n + 1) * HEAD_DIM] = o.T.astype(BF16)
        return jnp.int32(0)

    lax.switch(n_full % unroll, [functools.partial(finish, rest) for rest in range(unroll)], carry)


def _nsa_attention(queries, rotated, vs_t, vw_t, cmp, gates, onehot, batch, seq, *, tk):
    nq = seq // Q_BLOCK
    ns = seq // SEL_BLOCK
    ncp = seq // CMP_STRIDE
    halves = -(-ns // SEL_BIAS_BLOCKS)
    m = batch * seq
    once = pl.Buffered(1)

    def slab(first):
        return pl.BlockSpec((1, seq, HEAD_DIM), lambda b, g, qi: (first + g, b, 0), pipeline_mode=once)

    def slab_t(tile):
        return pl.BlockSpec((1, seq // tile, HEAD_DIM, tile), lambda b, g, qi: (g, b, 0, 0), pipeline_mode=once)

    def cmp_spec(which):
        return pl.BlockSpec((1, 1, 1, ncp // CMP_CHUNK, CMP_CHUNK, HEAD_DIM),
                            lambda b, g, qi: (which, b, g, 0, 0, 0))

    return pl.pallas_call(
        functools.partial(_attn_body, ns=ns, tk=tk),
        grid=(batch, NSA_KV_GROUPS, nq),
        in_specs=[
            pl.BlockSpec((NSA_HPG, 1, HEAD_DIM, Q_BLOCK), lambda b, g, qi: (g, b * nq + qi, 0, 0)),
            slab(ROT_KS), slab_t(tk), slab(ROT_KW), slab_t(Q_BLOCK),
            cmp_spec(0), cmp_spec(1),
            pl.BlockSpec((Q_BLOCK, LANES), lambda b, g, qi: (b * nq + qi, g)),
            pl.BlockSpec(onehot.shape, lambda b, g, qi: (0, 0), pipeline_mode=once),
        ],
        out_specs=pl.BlockSpec((Q_BLOCK, NSA_HPG * HEAD_DIM), lambda b, g, qi: (b * nq + qi, g)),
        out_shape=jax.ShapeDtypeStruct((m, NSA_HEADS * HEAD_DIM), BF16),
        scratch_shapes=[
            pltpu.VMEM((halves, 2 * HEAD_DIM, NSA_HPG * Q_BLOCK), BF16),
            pltpu.VMEM((ncp, Q_BLOCK), F32),
            pltpu.VMEM((tk, NSA_HPG * Q_BLOCK), F32),
            pltpu.VMEM((tk, NSA_HPG * Q_BLOCK), F32),
            pltpu.VMEM((tk, NSA_HPG * Q_BLOCK), F32),
        ],
        compiler_params=_params(("parallel", "parallel", "arbitrary")),
        name="nsa_attention",
    )(queries, rotated, vs_t, rotated, vw_t, cmp, cmp, gates, onehot)


def _sel_onehot(seq):
    keys = np.arange(min(seq, SEL_BIAS_KEYS))
    onehot = (keys[:, None] // SEL_BLOCK == np.arange(SEL_BIAS_BLOCKS)[None, :]).astype(np.float32)
    return jnp.asarray(onehot, BF16)


def _gate_weight(w_gl):
    d = w_gl.shape[0]
    w = w_gl.reshape(d, 3, NSA_KV_GROUPS, NSA_HPG).transpose(0, 2, 1, 3).reshape(d, NSA_KV_GROUPS, 3 * NSA_HPG)
    w = jnp.pad(w, ((0, 0), (0, 0), (0, LANES - 3 * NSA_HPG)))
    return w.reshape(d, NSA_KV_GROUPS * LANES)


def _nsa_layer(x, h, w_in, kc_pe, kc_w1, kc_w2, vc_pe, vc_w1, vc_w2, w_out, batch, seq, *, tk=1024):
    assert SEL_BLOCK == 4 * CMP_STRIDE and CMP_LEN == 2 * CMP_STRIDE and seq % max(tk, CMP_STRIDE * CMP_CHUNK) == 0
    n_main = NSA_HEADS * HEAD_DIM + 6 * NSA_KV_GROUPS * HEAD_DIM
    queries, rotated, kr, (vs_t, vw_t) = _nsa_projections(h, w_in, _rope_tables(jnp.arange(seq)), seq, tm=512, sel_tile=tk)
    gates = _mm_sigmoid(h, _gate_weight(w_in[:, n_main:]).astype(BF16), tm=1024)
    rows = seq // CMP_STRIDE
    cmp_tables = _rope_tables(jnp.arange(rows) * CMP_STRIDE + CMP_LEN - 1)
    half = CMP_LEN // 2
    pe = jnp.stack([kc_pe, vc_pe]).reshape(2, 2, 1, half * HEAD_DIM)
    w1 = jnp.stack([kc_w1, vc_w1]).astype(BF16)
    w1 = w1.reshape(2, 2, half * HEAD_DIM, w1.shape[-1])
    w2 = jnp.stack([kc_w2, vc_w2]).astype(BF16)
    cmp = _compress(kr, pe, w1, w2, cmp_tables, batch, seq)
    att = _nsa_attention(queries, rotated, vs_t, vw_t, cmp, gates, _sel_onehot(seq), batch, seq, tk=tk)
    return _mm_residual(att, w_out.astype(BF16), x, tm=512)


def kernel(x, p, norm_mix, norm_ffn, norm_ple, ffn_up, ffn_down, ple_proj, ple_gate, gm_in, gm_ln_g, gm_ln_b, gm_ws, gm_bs, gm_out, nsa_in, nsa_kc_pe, nsa_kc_w1, nsa_kc_w2, nsa_vc_pe, nsa_vc_w1, nsa_vc_w2, nsa_out, final_norm):
    batch, seq, d = x.shape
    m = batch * seq
    depth = p.shape[0]
    xf = x.reshape(m, d)
    row = lambda v: v.reshape(1, -1)
    p_rows = p.reshape(depth, m, -1)
    ffn_up_bf, ffn_down_bf = ffn_up.astype(BF16), ffn_down.astype(BF16)
    ple_gate_bf, ple_proj_bf = ple_gate.astype(BF16), ple_proj.astype(BF16)
    for i in range(depth):
        j = i // 2
        if i % 2 == 0:
            z = _norm_gelu_mm(xf, row(norm_mix[i]), gm_in[j].astype(BF16), tm=512)
            xf = _gmlp_gate_out(z, xf, row(gm_ln_g[j]), row(gm_ln_b[j]), gm_ws[j], gm_bs[j].T,
                                gm_out[j].astype(BF16), tm=512)
        else:
            xf = _nsa_layer(xf, h_mix, nsa_in[j], nsa_kc_pe[j], nsa_kc_w1[j], nsa_kc_w2[j],
                            nsa_vc_pe[j], nsa_vc_w1[j], nsa_vc_w2[j], nsa_out[j], batch, seq)
        xf = _ffn(xf, row(norm_ffn[i]), ffn_up_bf, ffn_down_bf, i, tm=1024, tf=512)
        if i == depth - 1:
            post, post_g = "final", final_norm
        elif (i + 1) % 2 == 1:
            post, post_g = "next", norm_mix[i + 1]
        else:
            post, post_g = "none", final_norm
        out = _ple(xf, p_rows, row(norm_ple[i]), ple_gate_bf, ple_proj_bf, row(post_g), i, post=post, tm=512)
        xf, h_mix = out if post == "next" else (out, None)
    return xf.reshape(batch, seq, d)
```

```python
import functools

import numpy as np
import jax
import jax.numpy as jnp
from jax import lax
from jax.experimental import pallas as pl
from jax.experimental.pallas import tpu as pltpu

F32 = jnp.float32
BF16 = jnp.bfloat16

EPS = 1e-6
HEAD_DIM = 128
NSA_HEADS = 16
NSA_KV_GROUPS = 4
NSA_HPG = NSA_HEADS // NSA_KV_GROUPS
ROT_DIM = HEAD_DIM // 4
ROPE_THETA = 500000.0
CMP_LEN = 32
CMP_STRIDE = 16
SEL_BLOCK = 64
SEL_TOPK = 16
WINDOW = 512
Q_BLOCK = 128
GM_CHUNK = 128
NEG_INF = -1e30
LOG2_E = 1.4426950408889634

LANES = 128
SEL_BIAS_BLOCKS = LANES
SEL_BIAS_KEYS = SEL_BIAS_BLOCKS * SEL_BLOCK
VMEM_LIMIT = 56 * 1024 * 1024


def _params(sem):
    return pltpu.CompilerParams(dimension_semantics=sem, vmem_limit_bytes=VMEM_LIMIT)


def _rmsnorm(x, g):
    return x * lax.rsqrt(jnp.mean(x * x, axis=-1, keepdims=True) + EPS) * g


def _dot(a, b):
    return jnp.dot(a, b, preferred_element_type=F32)


def _rope(x, c, sa, sb):
    return x * c + pltpu.roll(x, LANES - ROT_DIM // 2, 1) * sa + pltpu.roll(x, ROT_DIM // 2, 1) * sb


def _rope_tables(pos):
    half = ROT_DIM // 2
    inv = jnp.power(jnp.float32(ROPE_THETA), -jnp.arange(half, dtype=F32) * 2.0 / ROT_DIM)
    ang = pos.astype(F32)[:, None] * inv[None, :]
    cos, sin = jnp.cos(ang), jnp.sin(ang)
    n = pos.shape[0]
    rest = HEAD_DIM - ROT_DIM
    c = jnp.concatenate([cos, cos, jnp.ones((n, rest), F32)], axis=1)
    sa = jnp.concatenate([-sin, jnp.zeros((n, half + rest), F32)], axis=1)
    sb = jnp.concatenate([jnp.zeros((n, half), F32), sin, jnp.zeros((n, rest), F32)], axis=1)
    return c, sa, sb


def _norm_gelu_mm_body(x_ref, g_ref, w_ref, o_ref):
    h = _rmsnorm(x_ref[...], g_ref[...]).astype(BF16)
    o_ref[...] = jax.nn.gelu(_dot(h, w_ref[...])).astype(o_ref.dtype)


def _norm_gelu_mm(x, g, w, *, tm):
    m, d = x.shape
    n = w.shape[1]
    return pl.pallas_call(
        _norm_gelu_mm_body,
        grid=(m // tm,),
        in_specs=[
            pl.BlockSpec((tm, d), lambda i: (i, 0)),
            pl.BlockSpec((1, d), lambda i: (0, 0)),
            pl.BlockSpec((d, n), lambda i: (0, 0)),
        ],
        out_specs=pl.BlockSpec((tm, n), lambda i: (i, 0)),
        out_shape=jax.ShapeDtypeStruct((m, n), BF16),
        compiler_params=_params(("parallel",)),
        name="norm_gelu_mm",
    )(x, g, w)


def _gmlp_body(z_ref, x_ref, lg_ref, lb_ref, ws_ref, bs_ref, wo_ref, o_ref, y_ref, *, tm, width):
    groups = ws_ref.shape[0]
    gd = width // groups
    u = z_ref[:, :width]
    v = z_ref[:, width:].astype(F32)
    mu = jnp.mean(v, axis=-1, keepdims=True)
    var = jnp.mean(jnp.square(v - mu), axis=-1, keepdims=True)
    vn = ((v - mu) * lax.rsqrt(var + EPS) * lg_ref[...] + lb_ref[...]).astype(BF16)
    r = lax.broadcasted_iota(jnp.int32, (GM_CHUNK, GM_CHUNK), 0)
    c = lax.broadcasted_iota(jnp.int32, (GM_CHUNK, GM_CHUNK), 1)
    causal = c <= r
    for g in range(groups):
        wg = jnp.where(causal, ws_ref[g], 0.0).astype(BF16)
        bg = bs_ref[:, g:g + 1]
        cols = slice(g * gd, (g + 1) * gd)
        chunks = [slice(ch * GM_CHUNK, (ch + 1) * GM_CHUNK) for ch in range(tm // GM_CHUNK)]
        sv = _dot(wg, jnp.concatenate([vn[rows, cols] for rows in chunks], axis=1))
        for ch, rows in enumerate(chunks):
            y_ref[rows, cols] = (u[rows, cols].astype(F32) * (sv[:, ch * gd:(ch + 1) * gd] + bg)).astype(BF16)
    o_ref[...] = x_ref[...] + _dot(y_ref[...], wo_ref[...])


def _gmlp_gate_out(z, x, ln_g, ln_b, ws, bs, w_out, *, tm):
    m, d = x.shape
    width = z.shape[1] // 2
    groups = ws.shape[0]
    return pl.pallas_call(
        functools.partial(_gmlp_body, tm=tm, width=width),
        grid=(m // tm,),
        in_specs=[
            pl.BlockSpec((tm, 2 * width), lambda i: (i, 0)),
            pl.BlockSpec((tm, d), lambda i: (i, 0)),
            pl.BlockSpec((1, width), lambda i: (0, 0)),
            pl.BlockSpec((1, width), lambda i: (0, 0)),
            pl.BlockSpec((groups, GM_CHUNK, GM_CHUNK), lambda i: (0, 0, 0)),
            pl.BlockSpec((GM_CHUNK, groups), lambda i: (0, 0)),
            pl.BlockSpec((width, d), lambda i: (0, 0)),
        ],
        out_specs=pl.BlockSpec((tm, d), lambda i: (i, 0)),
        out_shape=jax.ShapeDtypeStruct((m, d), F32),
        scratch_shapes=[pltpu.VMEM((tm, width), BF16)],
        compiler_params=_params(("parallel",)),
        name="gmlp_gate_out",
    )(z, x, ln_g, ln_b, ws, bs, w_out)


def _ffn_body(x_ref, g_ref, wu_ref, wd_ref, o_ref, h_ref):
    f = pl.program_id(1)

    @pl.when(f == 0)
    def _():
        x = x_ref[...]
        h_ref[...] = _rmsnorm(x, g_ref[...]).astype(BF16)
        o_ref[...] = x

    a = jnp.square(jnp.maximum(_dot(h_ref[...], wu_ref[0]), 0.0)).astype(BF16)
    o_ref[...] += _dot(a, wd_ref[0])


def _ffn(x, g, w_up, w_down, layer, *, tm, tf):
    m, d = x.shape
    ff = w_up.shape[2]
    return pl.pallas_call(
        _ffn_body,
        grid=(m // tm, ff // tf),
        in_specs=[
            pl.BlockSpec((tm, d), lambda i, f: (i, 0)),
            pl.BlockSpec((1, d), lambda i, f: (0, 0)),
            pl.BlockSpec((1, d, tf), lambda i, f: (layer, 0, f)),
            pl.BlockSpec((1, tf, d), lambda i, f: (layer, f, 0)),
        ],
        out_specs=pl.BlockSpec((tm, d), lambda i, f: (i, 0)),
        out_shape=jax.ShapeDtypeStruct((m, d), F32),
        scratch_shapes=[pltpu.VMEM((tm, d), BF16)],
        compiler_params=_params(("parallel", "arbitrary")),
        name="ffn",
    )(x, g, w_up, w_down)


def _ple_body(x_ref, p_ref, g_ref, wg_ref, wp_ref, pg_ref, *o_refs, post):
    x = x_ref[...]
    h = _rmsnorm(x, g_ref[...]).astype(BF16)
    gate = jax.nn.sigmoid(_dot(h, wg_ref[0]))
    y = x + gate * _dot(p_ref[0].astype(BF16), wp_ref[0])
    if post == "final":
        o_refs[0][...] = _rmsnorm(y, pg_ref[...])
    else:
        o_refs[0][...] = y
    if post == "next":
        o_refs[1][...] = _rmsnorm(y, pg_ref[...]).astype(BF16)


def _ple(x, p, g, w_gate, w_proj, post_g, layer, *, post, tm):
    m, d = x.shape
    pd = p.shape[2]
    row_spec = pl.BlockSpec((tm, d), lambda i: (i, 0))
    out_specs, out_shape = row_spec, jax.ShapeDtypeStruct((m, d), F32)
    if post == "next":
        out_specs, out_shape = [row_spec, row_spec], [out_shape, jax.ShapeDtypeStruct((m, d), BF16)]
    return pl.pallas_call(
        functools.partial(_ple_body, post=post),
        grid=(m // tm,),
        in_specs=[
            row_spec,
            pl.BlockSpec((1, tm, pd), lambda i: (layer, i, 0)),
            pl.BlockSpec((1, d), lambda i: (0, 0)),
            pl.BlockSpec((1, d, d), lambda i: (layer, 0, 0)),
            pl.BlockSpec((1, pd, d), lambda i: (layer, 0, 0)),
            pl.BlockSpec((1, d), lambda i: (0, 0)),
        ],
        out_specs=out_specs,
        out_shape=out_shape,
        compiler_params=_params(("parallel",)),
        name="ple_" + post,
    )(x, p, g, w_gate, w_proj, post_g)


def _mm_res_body(a_ref, w_ref, x_ref, o_ref):
    o_ref[...] = x_ref[...] + _dot(a_ref[...], w_ref[...])


def _mm_residual(a, w, x, *, tm):
    m, k = a.shape
    n = w.shape[1]
    return pl.pallas_call(
        _mm_res_body,
        grid=(m // tm,),
        in_specs=[
            pl.BlockSpec((tm, k), lambda i: (i, 0)),
            pl.BlockSpec((k, n), lambda i: (0, 0)),
            pl.BlockSpec((tm, n), lambda i: (i, 0)),
        ],
        out_specs=pl.BlockSpec((tm, n), lambda i: (i, 0)),
        out_shape=jax.ShapeDtypeStruct((m, n), F32),
        compiler_params=_params(("parallel",)),
        name="mm_residual",
    )(a, w, x)


ROT_KS, ROT_KW = 0, NSA_KV_GROUPS


def _proj_query_body(h_ref, w_ref, c_ref, sa_ref, sb_ref, o_ref):
    acc = _dot(h_ref[...], w_ref[...])
    c, sa, sb = c_ref[...], sa_ref[...], sb_ref[...]
    tile = o_ref.shape[-1]
    for hh in range(o_ref.shape[0]):
        seg = _rope(acc[:, hh * HEAD_DIM:(hh + 1) * HEAD_DIM], c, sa, sb) * (HEAD_DIM ** -0.5 * LOG2_E)
        for t in range(acc.shape[0] // tile):
            o_ref[hh, t] = seg[t * tile:(t + 1) * tile].T.astype(BF16)


def _proj_flipped_body(h_ref, w_ref, o_ref):
    acc = _dot(h_ref[...], w_ref[...])
    tile = o_ref.shape[-1]
    for hh in range(o_ref.shape[0]):
        for t in range(acc.shape[0] // tile):
            o_ref[hh, t] = acc[t * tile:(t + 1) * tile, hh * HEAD_DIM:(hh + 1) * HEAD_DIM].T.astype(BF16)


def _proj_keys_body(h_ref, w_ref, c_ref, sa_ref, sb_ref, rot_ref, grp_ref, gate_ref, rows_ref):
    acc = _dot(h_ref[...], w_ref[...])
    c, sa, sb = c_ref[...], sa_ref[...], sb_ref[...]
    n_rot, n_grp = rot_ref.shape[0], grp_ref.shape[0]
    groups = acc.shape[0] // CMP_STRIDE
    for hh in range(n_rot):
        rot_ref[hh] = _rope(acc[:, hh * HEAD_DIM:(hh + 1) * HEAD_DIM], c, sa, sb).astype(BF16)
    for hh in range(n_grp):
        col = (n_rot + hh) * HEAD_DIM
        rows_ref[hh] = acc[:, col:col + HEAD_DIM]
        grp_ref[hh] = jnp.concatenate([rows_ref[hh, pl.ds(l, groups, stride=CMP_STRIDE), :]
                                       for l in range(CMP_STRIDE)], axis=1).astype(BF16)
    gate_ref[...] = jax.nn.sigmoid(acc[:, (n_rot + n_grp) * HEAD_DIM:])


def _proj_keys(h, w, tables, seq, n_rot, n_grp, *, tm):
    m, d = h.shape
    n = w.shape[1]
    hd = HEAD_DIM
    n_gate = n - (n_rot + n_grp) * hd
    tpb = seq // tm
    tab_spec = pl.BlockSpec((tm, hd), lambda i: (i % tpb, 0))
    return pl.pallas_call(
        _proj_keys_body,
        grid=(m // tm,),
        in_specs=[pl.BlockSpec((tm, d), lambda i: (i, 0)), pl.BlockSpec((d, n), lambda i: (0, 0))] + [tab_spec] * 3,
        out_specs=[
            pl.BlockSpec((n_rot, tm, hd), lambda i: (0, i, 0)),
            pl.BlockSpec((n_grp, tm // CMP_STRIDE, CMP_STRIDE * hd), lambda i: (0, i, 0)),
            pl.BlockSpec((tm, n_gate), lambda i: (i, 0)),
        ],
        out_shape=[
            jax.ShapeDtypeStruct((n_rot, m, hd), BF16),
            jax.ShapeDtypeStruct((n_grp, m // CMP_STRIDE, CMP_STRIDE * hd), BF16),
            jax.ShapeDtypeStruct((m, n_gate), F32),
        ],
        scratch_shapes=[pltpu.VMEM((n_grp, tm, hd), F32)],
        compiler_params=_params(("parallel",)),
        name="nsa_proj_keys",
    )(h, w, *tables)


def _proj(body, h, w, extra, extra_specs, out_block, out_dims, scratch, name, *, tm):
    m, d = h.shape
    n = w.shape[1]
    heads = n // HEAD_DIM
    return pl.pallas_call(
        body,
        grid=(m // tm,),
        in_specs=[pl.BlockSpec((tm, d), lambda i: (i, 0)), pl.BlockSpec((d, n), lambda i: (0, 0))] + extra_specs,
        out_specs=pl.BlockSpec((heads,) + out_block, lambda i: (0, i) + (0,) * (len(out_block) - 1)),
        out_shape=jax.ShapeDtypeStruct((heads,) + out_dims, BF16),
        scratch_shapes=scratch,
        compiler_params=_params(("parallel",)),
        name=name,
    )(h, w, *extra)


def _nsa_projections(h, w_in, tables, seq, *, tm, sel_tile):
    m = h.shape[0]
    hd = HEAD_DIM
    q_end = NSA_HEADS * hd
    kv = NSA_KV_GROUPS * hd
    kc, vc, ks, vs, kw, vw = (slice(q_end + i * kv, q_end + (i + 1) * kv) for i in range(6))
    w_in = w_in.astype(BF16)
    columns = lambda cols: jnp.concatenate([w_in[:, c] for c in cols], axis=1)
    tpb = seq // tm
    tab_spec = pl.BlockSpec((tm, hd), lambda i: (i % tpb, 0))
    queries = _proj(_proj_query_body, h, columns([slice(0, q_end)]), tables, [tab_spec] * 3,
                    (tm // Q_BLOCK, hd, Q_BLOCK), (m // Q_BLOCK, hd, Q_BLOCK), [], "nsa_proj_query", tm=tm)
    w_keys = jnp.concatenate([columns([ks, kw, kc, vc]), _gate_weight(w_in[:, vw.stop:])], axis=1)
    rotated, grouped, gates = _proj_keys(h, w_keys, tables, seq, 2 * NSA_KV_GROUPS, 2 * NSA_KV_GROUPS, tm=tm)
    flipped = [_proj(_proj_flipped_body, h, columns([cols]), (), [], (tile_rows // tile, hd, tile),
                     (m // tile, hd, tile), [], "nsa_proj_flipped", tm=tile_rows)
               for cols, tile, tile_rows in ((vs, sel_tile, max(tm, sel_tile)), (vw, Q_BLOCK, tm))]
    return queries, rotated, grouped, gates, flipped


CMP_CHUNK = LANES


def _compress_body(kr_ref, pe_ref, w1_ref, w2_ref, c_ref, sa_ref, sb_ref, o_ref):
    kr = kr_ref[0].astype(F32)
    rows = kr.shape[0]
    a = _dot((kr + pe_ref[0, 0]).astype(BF16), w1_ref[0, 0])
    b = _dot((kr + pe_ref[0, 1]).astype(BF16), w1_ref[0, 1])
    hid = jax.nn.gelu(a + pltpu.roll(b, rows - 1, 0))
    out = _dot(hid.astype(BF16), w2_ref[0])
    is_k = pl.program_id(0) == 0

    @pl.when(is_k)
    def _():
        roped = _rope(out, c_ref[...], sa_ref[...], sb_ref[...]).astype(BF16)
        for j in range(rows // CMP_CHUNK):
            o_ref[0, 0, 0, j] = roped[j * CMP_CHUNK:(j + 1) * CMP_CHUNK]

    @pl.when(jnp.logical_not(is_k))
    def _():
        for j in range(rows // CMP_CHUNK):
            o_ref[0, 0, 0, j] = out[j * CMP_CHUNK:(j + 1) * CMP_CHUNK].T.astype(BF16)


def _compress(kr, pe, w1, w2, tables, batch, seq):
    rows = seq // CMP_STRIDE
    kdim = CMP_STRIDE * HEAD_DIM
    hid = w1.shape[-1]
    tab_spec = pl.BlockSpec((rows, HEAD_DIM), lambda w, b, g: (0, 0))
    chunks = rows // CMP_CHUNK
    return pl.pallas_call(
        _compress_body,
        grid=(2, batch, NSA_KV_GROUPS),
        in_specs=[
            pl.BlockSpec((1, rows, kdim), lambda w, b, g: (NSA_KV_GROUPS * w + g, b, 0)),
            pl.BlockSpec((1, 2, 1, kdim), lambda w, b, g: (w, 0, 0, 0)),
            pl.BlockSpec((1, 2, kdim, hid), lambda w, b, g: (w, 0, 0, 0)),
            pl.BlockSpec((1, hid, HEAD_DIM), lambda w, b, g: (w, 0, 0)),
            tab_spec, tab_spec, tab_spec,
        ],
        out_specs=pl.BlockSpec((1, 1, 1, chunks, CMP_CHUNK, HEAD_DIM), lambda w, b, g: (w, b, g, 0, 0, 0)),
        out_shape=jax.ShapeDtypeStruct((2, batch, NSA_KV_GROUPS, chunks, CMP_CHUNK, HEAD_DIM), BF16),
        compiler_params=_params(("parallel", "parallel", "parallel")),
        name="nsa_compress",
    )(kr, pe, w1, w2, *tables)


def _attn_body(q_ref, ks_ref, vs_ref, kw_ref, vw_ref, kc_ref, vc_ref, gate_ref, e_ref,
               o_ref, qaug_ref, ps_ref, sa_ref, sb_ref, sd_ref, *, ns, tk):
    qi = pl.program_id(2)
    t0 = qi * Q_BLOCK
    cols = NSA_HPG * Q_BLOCK
    halves = qaug_ref.shape[0]
    for n in range(NSA_HPG):
        for h in range(halves):
            qaug_ref[h, 0:HEAD_DIM, n * Q_BLOCK:(n + 1) * Q_BLOCK] = q_ref[n, 0]
    q_t = qaug_ref[0, 0:HEAD_DIM, :]
    tq = t0 + (lax.broadcasted_iota(jnp.int32, (1, cols), 1) & (Q_BLOCK - 1))

    ncp = ps_ref.shape[0]
    step = min(2 * CMP_CHUNK, ncp)

    def compressed(rows):
        chunks = range(rows // CMP_CHUNK)
        s = _dot(jnp.concatenate([kc_ref[0, 0, 0, i] for i in chunks], axis=0), q_t)
        cend = lax.broadcasted_iota(jnp.int32, (rows, cols), 0) * CMP_STRIDE + (CMP_LEN - 1)
        s = jnp.where(cend <= tq, s, NEG_INF)
        p = jnp.exp2(s - jnp.max(s, axis=0, keepdims=True))
        acc = _dot(jnp.concatenate([vc_ref[0, 0, 0, i] for i in chunks], axis=1), p.astype(BF16))
        inv = jnp.where(tq >= CMP_LEN - 1, 1.0 / jnp.sum(p, axis=0, keepdims=True), 0.0)
        p = p * inv
        ps = p[:, 0:Q_BLOCK]
        for n in range(1, NSA_HPG):
            ps = ps + p[:, n * Q_BLOCK:(n + 1) * Q_BLOCK]
        ps_ref[0:rows, :] = ps
        if rows < ncp:
            ps_ref[rows:ncp, :] = jnp.zeros((ncp - rows, Q_BLOCK), F32)
        return (acc * inv,) + fast_rounds(rows // ratio)

    ratio = SEL_BLOCK // CMP_STRIDE
    nsp = halves * SEL_BIAS_BLOCKS
    n_forced = 3
    rounds = min(SEL_TOPK, ns) - n_forced

    def block_masks(nb):
        blk = lax.broadcasted_iota(jnp.int32, (nb, Q_BLOCK), 0)
        cur = (t0 + lax.broadcasted_iota(jnp.int32, (nb, Q_BLOCK), 1)) // SEL_BLOCK
        return blk, blk <= cur, (blk == 0) | (blk == cur) | (blk == cur - 1)

    def free_scores(nb):
        r = [ps_ref[pl.ds(j, nb, stride=ratio), :] for j in range(ratio)]
        blk, valid, forced = block_masks(nb)
        prev = jnp.where(blk == 0, 0.0, pltpu.roll(r[ratio - 1], 1, 0))
        free = valid & jnp.logical_not(forced)
        return blk, free, jnp.where(free, 2.0 * (r[0] + r[1] + r[2]) + r[3] + prev, NEG_INF)

    def all_blocks(taken):
        nb = taken.shape[0]
        return taken if nb == nsp else jnp.concatenate([taken, jnp.zeros((nsp - nb, Q_BLOCK), F32)], axis=0)

    def fast_rounds(nb):
        _, free, score = free_scores(nb)
        for _ in range(rounds):
            score = jnp.where(score == jnp.max(score, axis=0, keepdims=True), -jnp.inf, score)
        removed = score == -jnp.inf
        n_removed = jnp.sum(jnp.where(removed & free, 1.0, 0.0), axis=0, keepdims=True)
        cur_row = (t0 + lax.broadcasted_iota(jnp.int32, (1, Q_BLOCK), 1)) // SEL_BLOCK
        n_free = jnp.clip(cur_row + 1 - n_forced, 0, rounds).astype(F32)
        return all_blocks(jnp.where(removed, 1.0, 0.0)), jnp.sum(jnp.where(n_removed == n_free, 0.0, 1.0))

    def tie_breaking_rounds():
        blk, _, score = free_scores(ns)
        blk_f, taken = blk.astype(F32), jnp.zeros((ns, Q_BLOCK), F32)
        for _ in range(rounds):
            top = jnp.max(score, axis=0, keepdims=True)
            first = jnp.min(jnp.where(score == top, blk_f, float(ns)), axis=0, keepdims=True)
            hit = blk_f == first
            taken = jnp.where(hit, 1.0, taken)
            score = jnp.where(hit, -jnp.inf, score)
        return all_blocks(taken)

    variants = [functools.partial(compressed, rows) for rows in range(step, ncp + 1, step)]
    reach = jnp.minimum((t0 + Q_BLOCK - CMP_LEN) // (CMP_STRIDE * step), len(variants) - 1)
    o_c, taken, n_tied = lax.switch(reach, variants) if len(variants) > 1 else variants[0]()
    taken = lax.cond(n_tied == 0.0, lambda: taken, tie_breaking_rounds)
    _, valid, forced = block_masks(nsp)
    bias = jnp.where((forced | (taken > 0.0)) & valid, 0.0, NEG_INF).astype(BF16)

    for h in range(halves):
        bh = bias[h * SEL_BIAS_BLOCKS:(h + 1) * SEL_BIAS_BLOCKS]
        for n in range(NSA_HPG):
            qaug_ref[h, HEAD_DIM:HEAD_DIM + SEL_BIAS_BLOCKS, n * Q_BLOCK:(n + 1) * Q_BLOCK] = bh

    init = (jnp.full((1, cols), NEG_INF, F32), jnp.zeros((1, cols), F32),
            jnp.zeros((HEAD_DIM, cols), F32))

    n_win = WINDOW // Q_BLOCK + 1
    win_keys = n_win * Q_BLOCK
    w0 = pl.multiple_of(jnp.maximum(t0 - WINDOW, 0), Q_BLOCK)
    newest = tq - w0
    krow = lax.broadcasted_iota(jnp.int32, (win_keys, cols), 0)
    s_w = jnp.where(krow <= newest, _dot(kw_ref[0, pl.ds(w0, win_keys), :], q_t), NEG_INF)
    oldest = lax.broadcasted_iota(jnp.int32, (Q_BLOCK, cols), 0) > newest - WINDOW
    s_w = jnp.concatenate([jnp.where(oldest, s_w[:Q_BLOCK], NEG_INF), s_w[Q_BLOCK:]], axis=0)
    p_w = jnp.exp2(s_w - jnp.max(s_w, axis=0, keepdims=True))
    vw_t = jnp.concatenate([vw_ref[0, w0 // Q_BLOCK + w] for w in range(n_win)], axis=1)
    o_w = _dot(vw_t, p_w.astype(BF16)) * (1.0 / jnp.sum(p_w, axis=0, keepdims=True))

    def sel_scores(j, s_ref, diagonal=False):
        k0 = pl.multiple_of(j * tk, tk)
        e0 = pl.multiple_of(k0 % SEL_BIAS_KEYS, tk)
        kaug = jnp.concatenate([ks_ref[0, pl.ds(k0, tk), :], e_ref[pl.ds(e0, tk), :]], axis=1)
        s = _dot(kaug, qaug_ref[k0 // SEL_BIAS_KEYS])
        if diagonal:
            lk = lax.broadcasted_iota(jnp.int32, (tk, cols), 0)
            lq = lax.broadcasted_iota(jnp.int32, (tk, cols), 1) & (Q_BLOCK - 1)
            s = jnp.where(lk - lq <= t0 - k0, s, NEG_INF)
        s_ref[...] = s
        return jnp.max(s, axis=0, keepdims=True)

    def sel_update(s_ref, s_max, j, carry):
        m, l, acc = carry
        m_new = jnp.maximum(m, s_max)
        alpha = jnp.exp2(m - m_new)
        p = jnp.exp2(s_ref[...] - m_new)
        l = alpha * l + jnp.sum(p, axis=0, keepdims=True)
        acc = alpha * acc + _dot(vs_ref[0, j], p.astype(BF16))
        return m_new, l, acc

    n_full = t0 // tk
    last_full = jnp.maximum(n_full - 1, 0)
    max_d = sel_scores(n_full, sd_ref, diagonal=True)
    max_a = sel_scores(0, sa_ref)

    def sel_pair(j, carry):
        max_a, state = carry
        max_b = sel_scores(j + 1, sb_ref)
        state = sel_update(sa_ref, max_a, j, state)
        max_a = sel_scores(jnp.minimum(j + 2, last_full), sa_ref)
        return max_a, sel_update(sb_ref, max_b, j + 1, state)

    unroll = 4
    carry = lax.fori_loop(0, n_full // unroll,
                          lambda jj, c: sel_pair(unroll * jj + 2, sel_pair(unroll * jj, c)), (max_a, init))

    def finish(rest, carry):
        max_cur, state = carry
        first = n_full - rest
        bufs = (sa_ref, sb_ref)
        for i in range(rest):
            if i + 1 < rest:
                max_next = sel_scores(first + i + 1, bufs[(i + 1) % 2])
            state = sel_update(bufs[i % 2], max_cur, first + i, state)
            if i + 1 < rest:
                max_cur = max_next
        _, l_s, acc_s = sel_update(sd_ref, max_d, n_full, state)
        o_s = acc_s * (1.0 / l_s)
        g_t = gate_ref[...].T
        for n in range(NSA_HPG):
            c = slice(n * Q_BLOCK, (n + 1) * Q_BLOCK)
            o = (g_t[n:n + 1] * o_c[:, c] + g_t[NSA_HPG + n:NSA_HPG + n + 1] * o_s[:, c]
                 + g_t[2 * NSA_HPG + n:2 * NSA_HPG + n + 1] * o_w[:, c])
            o_ref[:, n * HEAD_DIM:(n + 1) * HEAD_DIM] = o.T.astype(BF16)
        return jnp.int32(0)

    lax.switch(n_full % unroll, [functools.partial(finish, rest) for rest in range(unroll)], carry)


def _nsa_attention(queries, rotated, vs_t, vw_t, cmp, gates, onehot, batch, seq, *, tk):
    nq = seq // Q_BLOCK
    ns = seq // SEL_BLOCK
    ncp = seq // CMP_STRIDE
    halves = -(-ns // SEL_BIAS_BLOCKS)
    m = batch * seq
    once = pl.Buffered(1)

    def slab(first):
        return pl.BlockSpec((1, seq, HEAD_DIM), lambda b, g, qi: (first + g, b, 0), pipeline_mode=once)

    def slab_t(tile):
        return pl.BlockSpec((1, seq // tile, HEAD_DIM, tile), lambda b, g, qi: (g, b, 0, 0), pipeline_mode=once)

    def cmp_spec(which):
        return pl.BlockSpec((1, 1, 1, ncp // CMP_CHUNK, CMP_CHUNK, HEAD_DIM),
                            lambda b, g, qi: (which, b, g, 0, 0, 0))

    return pl.pallas_call(
        functools.partial(_attn_body, ns=ns, tk=tk),
        grid=(batch, NSA_KV_GROUPS, nq),
        in_specs=[
            pl.BlockSpec((NSA_HPG, 1, HEAD_DIM, Q_BLOCK), lambda b, g, qi: (g, b * nq + qi, 0, 0)),
            slab(ROT_KS), slab_t(tk), slab(ROT_KW), slab_t(Q_BLOCK),
            cmp_spec(0), cmp_spec(1),
            pl.BlockSpec((Q_BLOCK, LANES), lambda b, g, qi: (b * nq + qi, g)),
            pl.BlockSpec(onehot.shape, lambda b, g, qi: (0, 0), pipeline_mode=once),
        ],
        out_specs=pl.BlockSpec((Q_BLOCK, NSA_HPG * HEAD_DIM), lambda b, g, qi: (b * nq + qi, g)),
        out_shape=jax.ShapeDtypeStruct((m, NSA_HEADS * HEAD_DIM), BF16),
        scratch_shapes=[
            pltpu.VMEM((halves, 2 * HEAD_DIM, NSA_HPG * Q_BLOCK), BF16),
            pltpu.VMEM((ncp, Q_BLOCK), F32),
            pltpu.VMEM((tk, NSA_HPG * Q_BLOCK), F32),
            pltpu.VMEM((tk, NSA_HPG * Q_BLOCK), F32),
            pltpu.VMEM((tk, NSA_HPG * Q_BLOCK), F32),
        ],
        compiler_params=_params(("parallel", "parallel", "arbitrary")),
        name="nsa_attention",
    )(queries, rotated, vs_t, rotated, vw_t, cmp, cmp, gates, onehot)


def _sel_onehot(seq):
    keys = np.arange(min(seq, SEL_BIAS_KEYS))
    onehot = (keys[:, None] // SEL_BLOCK == np.arange(SEL_BIAS_BLOCKS)[None, :]).astype(np.float32)
    return jnp.asarray(onehot, BF16)


def _gate_weight(w_gl):
    d = w_gl.shape[0]
    w = w_gl.reshape(d, 3, NSA_KV_GROUPS, NSA_HPG).transpose(0, 2, 1, 3).reshape(d, NSA_KV_GROUPS, 3 * NSA_HPG)
    w = jnp.pad(w, ((0, 0), (0, 0), (0, LANES - 3 * NSA_HPG)))
    return w.reshape(d, NSA_KV_GROUPS * LANES)


def _nsa_layer(x, h, w_in, kc_pe, kc_w1, kc_w2, vc_pe, vc_w1, vc_w2, w_out, batch, seq, *, tk=1024):
    assert SEL_BLOCK == 4 * CMP_STRIDE and CMP_LEN == 2 * CMP_STRIDE and seq % max(tk, CMP_STRIDE * CMP_CHUNK) == 0
    queries, rotated, kr, gates, (vs_t, vw_t) = _nsa_projections(
        h, w_in, _rope_tables(jnp.arange(seq)), seq, tm=512, sel_tile=tk)
    rows = seq // CMP_STRIDE
    cmp_tables = _rope_tables(jnp.arange(rows) * CMP_STRIDE + CMP_LEN - 1)
    half = CMP_LEN // 2
    pe = jnp.stack([kc_pe, vc_pe]).reshape(2, 2, 1, half * HEAD_DIM)
    w1 = jnp.stack([kc_w1, vc_w1]).astype(BF16)
    w1 = w1.reshape(2, 2, half * HEAD_DIM, w1.shape[-1])
    w2 = jnp.stack([kc_w2, vc_w2]).astype(BF16)
    cmp = _compress(kr, pe, w1, w2, cmp_tables, batch, seq)
    att = _nsa_attention(queries, rotated, vs_t, vw_t, cmp, gates, _sel_onehot(seq), batch, seq, tk=tk)
    return _mm_residual(att, w_out.astype(BF16), x, tm=512)


def kernel(x, p, norm_mix, norm_ffn, norm_ple, ffn_up, ffn_down, ple_proj, ple_gate, gm_in, gm_ln_g, gm_ln_b, gm_ws, gm_bs, gm_out, nsa_in, nsa_kc_pe, nsa_kc_w1, nsa_kc_w2, nsa_vc_pe, nsa_vc_w1, nsa_vc_w2, nsa_out, final_norm):
    batch, seq, d = x.shape
    m = batch * seq
    depth = p.shape[0]
    xf = x.reshape(m, d)
    row = lambda v: v.reshape(1, -1)
    p_rows = p.reshape(depth, m, -1)
    ffn_up_bf, ffn_down_bf = ffn_up.astype(BF16), ffn_down.astype(BF16)
    ple_gate_bf, ple_proj_bf = ple_gate.astype(BF16), ple_proj.astype(BF16)
    for i in range(depth):
        j = i // 2
        if i % 2 == 0:
            z = _norm_gelu_mm(xf, row(norm_mix[i]), gm_in[j].astype(BF16), tm=512)
            xf = _gmlp_gate_out(z, xf, row(gm_ln_g[j]), row(gm_ln_b[j]), gm_ws[j], gm_bs[j].T,
                                gm_out[j].astype(BF16), tm=512)
        else:
            xf = _nsa_layer(xf, h_mix, nsa_in[j], nsa_kc_pe[j], nsa_kc_w1[j], nsa_kc_w2[j],
                            nsa_vc_pe[j], nsa_vc_w1[j], nsa_vc_w2[j], nsa_out[j], batch, seq)
        xf = _ffn(xf, row(norm_ffn[i]), ffn_up_bf, ffn_down_bf, i, tm=1024, tf=512)
        if i == depth - 1:
            post, post_g = "final", final_norm
        elif (i + 1) % 2 == 1:
            post, post_g = "next", norm_mix[i + 1]
        else:
            post, post_g = "none", final_norm
        out = _ple(xf, p_rows, row(norm_ple[i]), ple_gate_bf, ple_proj_bf, row(post_g), i, post=post, tm=512)
        xf, h_mix = out if post == "next" else (out, None)
    return xf.reshape(batch, seq, d)
```

```python
import functools

import numpy as np
import jax
import jax.numpy as jnp
from jax import lax
from jax.experimental import pallas as pl
from jax.experimental.pallas import tpu as pltpu

F32 = jnp.float32
BF16 = jnp.bfloat16

EPS = 1e-6
HEAD_DIM = 128
NSA_HEADS = 16
NSA_KV_GROUPS = 4
NSA_HPG = NSA_HEADS // NSA_KV_GROUPS
ROT_DIM = HEAD_DIM // 4
ROPE_THETA = 500000.0
CMP_LEN = 32
CMP_STRIDE = 16
SEL_BLOCK = 64
SEL_TOPK = 16
WINDOW = 512
Q_BLOCK = 128
GM_CHUNK = 128
NEG_INF = -1e30
LOG2_E = 1.4426950408889634

LANES = 128
SEL_BIAS_BLOCKS = LANES
SEL_BIAS_KEYS = SEL_BIAS_BLOCKS * SEL_BLOCK
VMEM_LIMIT = 56 * 1024 * 1024


def _params(sem):
    return pltpu.CompilerParams(dimension_semantics=sem, vmem_limit_bytes=VMEM_LIMIT)


def _rmsnorm(x, g):
    return x * lax.rsqrt(jnp.mean(x * x, axis=-1, keepdims=True) + EPS) * g


def _dot(a, b):
    return jnp.dot(a, b, preferred_element_type=F32)


def _rope(x, c, sa, sb):
    return x * c + pltpu.roll(x, LANES - ROT_DIM // 2, 1) * sa + pltpu.roll(x, ROT_DIM // 2, 1) * sb


def _rope_tables(pos):
    half = ROT_DIM // 2
    inv = jnp.power(jnp.float32(ROPE_THETA), -jnp.arange(half, dtype=F32) * 2.0 / ROT_DIM)
    ang = pos.astype(F32)[:, None] * inv[None, :]
    cos, sin = jnp.cos(ang), jnp.sin(ang)
    n = pos.shape[0]
    rest = HEAD_DIM - ROT_DIM
    c = jnp.concatenate([cos, cos, jnp.ones((n, rest), F32)], axis=1)
    sa = jnp.concatenate([-sin, jnp.zeros((n, half + rest), F32)], axis=1)
    sb = jnp.concatenate([jnp.zeros((n, half), F32), sin, jnp.zeros((n, rest), F32)], axis=1)
    return c, sa, sb


def _norm_gelu_mm_body(x_ref, g_ref, w_ref, o_ref):
    h = _rmsnorm(x_ref[...], g_ref[...]).astype(BF16)
    o_ref[...] = jax.nn.gelu(_dot(h, w_ref[...])).astype(o_ref.dtype)


def _norm_gelu_mm(x, g, w, *, tm):
    m, d = x.shape
    n = w.shape[1]
    return pl.pallas_call(
        _norm_gelu_mm_body,
        grid=(m // tm,),
        in_specs=[
            pl.BlockSpec((tm, d), lambda i: (i, 0)),
            pl.BlockSpec((1, d), lambda i: (0, 0)),
            pl.BlockSpec((d, n), lambda i: (0, 0)),
        ],
        out_specs=pl.BlockSpec((tm, n), lambda i: (i, 0)),
        out_shape=jax.ShapeDtypeStruct((m, n), BF16),
        compiler_params=_params(("parallel",)),
        name="norm_gelu_mm",
    )(x, g, w)


def _gmlp_body(z_ref, x_ref, lg_ref, lb_ref, ws_ref, bs_ref, wo_ref, o_ref, y_ref, *, tm, width):
    groups = ws_ref.shape[0]
    gd = width // groups
    u = z_ref[:, :width]
    v = z_ref[:, width:].astype(F32)
    mu = jnp.mean(v, axis=-1, keepdims=True)
    var = jnp.mean(jnp.square(v - mu), axis=-1, keepdims=True)
    vn = ((v - mu) * lax.rsqrt(var + EPS) * lg_ref[...] + lb_ref[...]).astype(BF16)
    r = lax.broadcasted_iota(jnp.int32, (GM_CHUNK, GM_CHUNK), 0)
    c = lax.broadcasted_iota(jnp.int32, (GM_CHUNK, GM_CHUNK), 1)
    causal = c <= r
    for g in range(groups):
        wg = jnp.where(causal, ws_ref[g], 0.0).astype(BF16)
        bg = bs_ref[:, g:g + 1]
        cols = slice(g * gd, (g + 1) * gd)
        chunks = [slice(ch * GM_CHUNK, (ch + 1) * GM_CHUNK) for ch in range(tm // GM_CHUNK)]
        sv = _dot(wg, jnp.concatenate([vn[rows, cols] for rows in chunks], axis=1))
        for ch, rows in enumerate(chunks):
            y_ref[rows, cols] = (u[rows, cols].astype(F32) * (sv[:, ch * gd:(ch + 1) * gd] + bg)).astype(BF16)
    o_ref[...] = x_ref[...] + _dot(y_ref[...], wo_ref[...])


def _gmlp_gate_out(z, x, ln_g, ln_b, ws, bs, w_out, *, tm):
    m, d = x.shape
    width = z.shape[1] // 2
    groups = ws.shape[0]
    return pl.pallas_call(
        functools.partial(_gmlp_body, tm=tm, width=width),
        grid=(m // tm,),
        in_specs=[
            pl.BlockSpec((tm, 2 * width), lambda i: (i, 0)),
            pl.BlockSpec((tm, d), lambda i: (i, 0)),
            pl.BlockSpec((1, width), lambda i: (0, 0)),
            pl.BlockSpec((1, width), lambda i: (0, 0)),
            pl.BlockSpec((groups, GM_CHUNK, GM_CHUNK), lambda i: (0, 0, 0)),
            pl.BlockSpec((GM_CHUNK, groups), lambda i: (0, 0)),
            pl.BlockSpec((width, d), lambda i: (0, 0)),
        ],
        out_specs=pl.BlockSpec((tm, d), lambda i: (i, 0)),
        out_shape=jax.ShapeDtypeStruct((m, d), F32),
        scratch_shapes=[pltpu.VMEM((tm, width), BF16)],
        compiler_params=_params(("parallel",)),
        name="gmlp_gate_out",
    )(z, x, ln_g, ln_b, ws, bs, w_out)


def _ffn_body(x_ref, g_ref, wu_ref, wd_ref, o_ref, h_ref):
    f = pl.program_id(1)

    @pl.when(f == 0)
    def _():
        x = x_ref[...]
        h_ref[...] = _rmsnorm(x, g_ref[...]).astype(BF16)
        o_ref[...] = x

    a = jnp.square(jnp.maximum(_dot(h_ref[...], wu_ref[0]), 0.0)).astype(BF16)
    o_ref[...] += _dot(a, wd_ref[0])


def _ffn(x, g, w_up, w_down, layer, *, tm, tf):
    m, d = x.shape
    ff = w_up.shape[2]
    return pl.pallas_call(
        _ffn_body,
        grid=(m // tm, ff // tf),
        in_specs=[
            pl.BlockSpec((tm, d), lambda i, f: (i, 0)),
            pl.BlockSpec((1, d), lambda i, f: (0, 0)),
            pl.BlockSpec((1, d, tf), lambda i, f: (layer, 0, f)),
            pl.BlockSpec((1, tf, d), lambda i, f: (layer, f, 0)),
        ],
        out_specs=pl.BlockSpec((tm, d), lambda i, f: (i, 0)),
        out_shape=jax.ShapeDtypeStruct((m, d), F32),
        scratch_shapes=[pltpu.VMEM((tm, d), BF16)],
        compiler_params=_params(("parallel", "arbitrary")),
        name="ffn",
    )(x, g, w_up, w_down)


def _ple_body(x_ref, p_ref, g_ref, wg_ref, wp_ref, pg_ref, *o_refs, post):
    x = x_ref[...]
    h = _rmsnorm(x, g_ref[...]).astype(BF16)
    gate = jax.nn.sigmoid(_dot(h, wg_ref[0]))
    y = x + gate * _dot(p_ref[0].astype(BF16), wp_ref[0])
    if post == "final":
        o_refs[0][...] = _rmsnorm(y, pg_ref[...])
    else:
        o_refs[0][...] = y
    if post == "next":
        o_refs[1][...] = _rmsnorm(y, pg_ref[...]).astype(BF16)


def _ple(x, p, g, w_gate, w_proj, post_g, layer, *, post, tm):
    m, d = x.shape
    pd = p.shape[2]
    row_spec = pl.BlockSpec((tm, d), lambda i: (i, 0))
    out_specs, out_shape = row_spec, jax.ShapeDtypeStruct((m, d), F32)
    if post == "next":
        out_specs, out_shape = [row_spec, row_spec], [out_shape, jax.ShapeDtypeStruct((m, d), BF16)]
    return pl.pallas_call(
        functools.partial(_ple_body, post=post),
        grid=(m // tm,),
        in_specs=[
            row_spec,
            pl.BlockSpec((1, tm, pd), lambda i: (layer, i, 0)),
            pl.BlockSpec((1, d), lambda i: (0, 0)),
            pl.BlockSpec((1, d, d), lambda i: (layer, 0, 0)),
            pl.BlockSpec((1, pd, d), lambda i: (layer, 0, 0)),
            pl.BlockSpec((1, d), lambda i: (0, 0)),
        ],
        out_specs=out_specs,
        out_shape=out_shape,
        compiler_params=_params(("parallel",)),
        name="ple_" + post,
    )(x, p, g, w_gate, w_proj, post_g)


def _mm_res_body(a_ref, w_ref, x_ref, o_ref):
    o_ref[...] = x_ref[...] + _dot(a_ref[...], w_ref[...])


def _mm_residual(a, w, x, *, tm):
    m, k = a.shape
    n = w.shape[1]
    return pl.pallas_call(
        _mm_res_body,
        grid=(m // tm,),
        in_specs=[
            pl.BlockSpec((tm, k), lambda i: (i, 0)),
            pl.BlockSpec((k, n), lambda i: (0, 0)),
            pl.BlockSpec((tm, n), lambda i: (i, 0)),
        ],
        out_specs=pl.BlockSpec((tm, n), lambda i: (i, 0)),
        out_shape=jax.ShapeDtypeStruct((m, n), F32),
        compiler_params=_params(("parallel",)),
        name="mm_residual",
    )(a, w, x)


ROT_KS, ROT_KW = 0, NSA_KV_GROUPS


def _proj_query_body(h_ref, w_ref, c_ref, sa_ref, sb_ref, o_ref):
    acc = _dot(h_ref[...], w_ref[...])
    c, sa, sb = c_ref[...], sa_ref[...], sb_ref[...]
    tile = o_ref.shape[-1]
    for hh in range(o_ref.shape[0]):
        seg = _rope(acc[:, hh * HEAD_DIM:(hh + 1) * HEAD_DIM], c, sa, sb) * (HEAD_DIM ** -0.5 * LOG2_E)
        for t in range(acc.shape[0] // tile):
            o_ref[hh, t] = seg[t * tile:(t + 1) * tile].T.astype(BF16)


def _proj_flipped_body(h_ref, w_ref, o_ref):
    acc = _dot(h_ref[...], w_ref[...])
    tile = o_ref.shape[-1]
    for hh in range(o_ref.shape[0]):
        for t in range(acc.shape[0] // tile):
            o_ref[hh, t] = acc[t * tile:(t + 1) * tile, hh * HEAD_DIM:(hh + 1) * HEAD_DIM].T.astype(BF16)


def _proj_keys_body(h_ref, w_ref, c_ref, sa_ref, sb_ref, rot_ref, grp_ref, gate_ref, rows_ref):
    acc = _dot(h_ref[...], w_ref[...])
    c, sa, sb = c_ref[...], sa_ref[...], sb_ref[...]
    n_rot, n_grp = rot_ref.shape[0], grp_ref.shape[0]
    groups = acc.shape[0] // CMP_STRIDE
    for hh in range(n_rot):
        rot_ref[hh] = _rope(acc[:, hh * HEAD_DIM:(hh + 1) * HEAD_DIM], c, sa, sb).astype(BF16)
    for hh in range(n_grp):
        col = (n_rot + hh) * HEAD_DIM
        rows_ref[hh] = acc[:, col:col + HEAD_DIM]
        grp_ref[hh] = jnp.concatenate([rows_ref[hh, pl.ds(l, groups, stride=CMP_STRIDE), :]
                                       for l in range(CMP_STRIDE)], axis=1).astype(BF16)
    gate_ref[...] = jax.nn.sigmoid(acc[:, (n_rot + n_grp) * HEAD_DIM:])


def _proj_keys(h, w, tables, seq, n_rot, n_grp, *, tm):
    m, d = h.shape
    n = w.shape[1]
    hd = HEAD_DIM
    n_gate = n - (n_rot + n_grp) * hd
    tpb = seq // tm
    tab_spec = pl.BlockSpec((tm, hd), lambda i: (i % tpb, 0))
    return pl.pallas_call(
        _proj_keys_body,
        grid=(m // tm,),
        in_specs=[pl.BlockSpec((tm, d), lambda i: (i, 0)), pl.BlockSpec((d, n), lambda i: (0, 0))] + [tab_spec] * 3,
        out_specs=[
            pl.BlockSpec((n_rot, tm, hd), lambda i: (0, i, 0)),
            pl.BlockSpec((n_grp, tm // CMP_STRIDE, CMP_STRIDE * hd), lambda i: (0, i, 0)),
            pl.BlockSpec((tm, n_gate), lambda i: (i, 0)),
        ],
        out_shape=[
            jax.ShapeDtypeStruct((n_rot, m, hd), BF16),
            jax.ShapeDtypeStruct((n_grp, m // CMP_STRIDE, CMP_STRIDE * hd), BF16),
            jax.ShapeDtypeStruct((m, n_gate), F32),
        ],
        scratch_shapes=[pltpu.VMEM((n_grp, tm, hd), F32)],
        compiler_params=_params(("parallel",)),
        name="nsa_proj_keys",
    )(h, w, *tables)


def _proj(body, h, w, extra, extra_specs, out_block, out_dims, scratch, name, *, tm):
    m, d = h.shape
    n = w.shape[1]
    heads = n // HEAD_DIM
    return pl.pallas_call(
        body,
        grid=(m // tm,),
        in_specs=[pl.BlockSpec((tm, d), lambda i: (i, 0)), pl.BlockSpec((d, n), lambda i: (0, 0))] + extra_specs,
        out_specs=pl.BlockSpec((heads,) + out_block, lambda i: (0, i) + (0,) * (len(out_block) - 1)),
        out_shape=jax.ShapeDtypeStruct((heads,) + out_dims, BF16),
        scratch_shapes=scratch,
        compiler_params=_params(("parallel",)),
        name=name,
    )(h, w, *extra)


def _nsa_projections(h, w_in, tables, seq, *, tm, sel_tile):
    m = h.shape[0]
    hd = HEAD_DIM
    q_end = NSA_HEADS * hd
    kv = NSA_KV_GROUPS * hd
    kc, vc, ks, vs, kw, vw = (slice(q_end + i * kv, q_end + (i + 1) * kv) for i in range(6))
    w_in = w_in.astype(BF16)
    columns = lambda cols: jnp.concatenate([w_in[:, c] for c in cols], axis=1)
    tpb = seq // tm
    tab_spec = pl.BlockSpec((tm, hd), lambda i: (i % tpb, 0))
    queries = _proj(_proj_query_body, h, columns([slice(0, q_end)]), tables, [tab_spec] * 3,
                    (tm // Q_BLOCK, hd, Q_BLOCK), (m // Q_BLOCK, hd, Q_BLOCK), [], "nsa_proj_query", tm=tm)
    w_keys = jnp.concatenate([columns([ks, kw, kc, vc]), _gate_weight(w_in[:, vw.stop:])], axis=1)
    rotated, grouped, gates = _proj_keys(h, w_keys, tables, seq, 2 * NSA_KV_GROUPS, 2 * NSA_KV_GROUPS, tm=tm)
    rows_v = max(tm, sel_tile)

    def values_body(h_ref, w_ref, vs_ref, vw_ref):
        acc = _dot(h_ref[...], w_ref[...])
        for first, o_ref in ((0, vs_ref), (NSA_KV_GROUPS, vw_ref)):
            tile = o_ref.shape[-1]
            for hh in range(o_ref.shape[0]):
                col = (first + hh) * hd
                for t in range(rows_v // tile):
                    o_ref[hh, t] = acc[t * tile:(t + 1) * tile, col:col + hd].T.astype(BF16)

    w_values = columns([vs, vw])
    flipped = pl.pallas_call(
        values_body,
        grid=(m // rows_v,),
        in_specs=[pl.BlockSpec((rows_v, h.shape[1]), lambda i: (i, 0)),
                  pl.BlockSpec(w_values.shape, lambda i: (0, 0))],
        out_specs=[pl.BlockSpec((NSA_KV_GROUPS, rows_v // tile, hd, tile), lambda i: (0, i, 0, 0))
                   for tile in (sel_tile, Q_BLOCK)],
        out_shape=[jax.ShapeDtypeStruct((NSA_KV_GROUPS, m // tile, hd, tile), BF16) for tile in (sel_tile, Q_BLOCK)],
        compiler_params=_params(("parallel",)),
        name="nsa_proj_values",
    )(h, w_values)
    return queries, rotated, grouped, gates, flipped


CMP_CHUNK = LANES


def _compress_body(kr_ref, pe_ref, w1_ref, w2_ref, c_ref, sa_ref, sb_ref, o_ref):
    kr = kr_ref[0].astype(F32)
    rows = kr.shape[0]
    a = _dot((kr + pe_ref[0, 0]).astype(BF16), w1_ref[0, 0])
    b = _dot((kr + pe_ref[0, 1]).astype(BF16), w1_ref[0, 1])
    hid = jax.nn.gelu(a + pltpu.roll(b, rows - 1, 0))
    out = _dot(hid.astype(BF16), w2_ref[0])
    is_k = pl.program_id(0) == 0

    @pl.when(is_k)
    def _():
        roped = _rope(out, c_ref[...], sa_ref[...], sb_ref[...]).astype(BF16)
        for j in range(rows // CMP_CHUNK):
            o_ref[0, 0, 0, j] = roped[j * CMP_CHUNK:(j + 1) * CMP_CHUNK]

    @pl.when(jnp.logical_not(is_k))
    def _():
        for j in range(rows // CMP_CHUNK):
            o_ref[0, 0, 0, j] = out[j * CMP_CHUNK:(j + 1) * CMP_CHUNK].T.astype(BF16)


def _compress(kr, pe, w1, w2, tables, batch, seq):
    rows = seq // CMP_STRIDE
    kdim = CMP_STRIDE * HEAD_DIM
    hid = w1.shape[-1]
    tab_spec = pl.BlockSpec((rows, HEAD_DIM), lambda w, b, g: (0, 0))
    chunks = rows // CMP_CHUNK
    return pl.pallas_call(
        _compress_body,
        grid=(2, batch, NSA_KV_GROUPS),
        in_specs=[
            pl.BlockSpec((1, rows, kdim), lambda w, b, g: (NSA_KV_GROUPS * w + g, b, 0)),
            pl.BlockSpec((1, 2, 1, kdim), lambda w, b, g: (w, 0, 0, 0)),
            pl.BlockSpec((1, 2, kdim, hid), lambda w, b, g: (w, 0, 0, 0)),
            pl.BlockSpec((1, hid, HEAD_DIM), lambda w, b, g: (w, 0, 0)),
            tab_spec, tab_spec, tab_spec,
        ],
        out_specs=pl.BlockSpec((1, 1, 1, chunks, CMP_CHUNK, HEAD_DIM), lambda w, b, g: (w, b, g, 0, 0, 0)),
        out_shape=jax.ShapeDtypeStruct((2, batch, NSA_KV_GROUPS, chunks, CMP_CHUNK, HEAD_DIM), BF16),
        compiler_params=_params(("parallel", "parallel", "parallel")),
        name="nsa_compress",
    )(kr, pe, w1, w2, *tables)


def _attn_body(q_ref, ks_ref, vs_ref, kw_ref, vw_ref, kc_ref, vc_ref, gate_ref, e_ref,
               o_ref, qaug_ref, ps_ref, sa_ref, sb_ref, sd_ref, *, ns, tk):
    qi = pl.program_id(2)
    t0 = qi * Q_BLOCK
    cols = NSA_HPG * Q_BLOCK
    halves = qaug_ref.shape[0]
    for n in range(NSA_HPG):
        for h in range(halves):
            qaug_ref[h, 0:HEAD_DIM, n * Q_BLOCK:(n + 1) * Q_BLOCK] = q_ref[n, 0]
    q_t = qaug_ref[0, 0:HEAD_DIM, :]
    tq = t0 + (lax.broadcasted_iota(jnp.int32, (1, cols), 1) & (Q_BLOCK - 1))

    ncp = ps_ref.shape[0]
    step = min(2 * CMP_CHUNK, ncp)

    def compressed(rows):
        chunks = range(rows // CMP_CHUNK)
        s = _dot(jnp.concatenate([kc_ref[0, 0, 0, i] for i in chunks], axis=0), q_t)
        cend = lax.broadcasted_iota(jnp.int32, (rows, cols), 0) * CMP_STRIDE + (CMP_LEN - 1)
        s = jnp.where(cend <= tq, s, NEG_INF)
        p = jnp.exp2(s - jnp.max(s, axis=0, keepdims=True))
        acc = _dot(jnp.concatenate([vc_ref[0, 0, 0, i] for i in chunks], axis=1), p.astype(BF16))
        inv = jnp.where(tq >= CMP_LEN - 1, 1.0 / jnp.sum(p, axis=0, keepdims=True), 0.0)
        p = p * inv
        ps = p[:, 0:Q_BLOCK]
        for n in range(1, NSA_HPG):
            ps = ps + p[:, n * Q_BLOCK:(n + 1) * Q_BLOCK]
        ps_ref[0:rows, :] = ps
        if rows < ncp:
            ps_ref[rows:ncp, :] = jnp.zeros((ncp - rows, Q_BLOCK), F32)
        return (acc * inv,) + fast_rounds(rows // ratio)

    ratio = SEL_BLOCK // CMP_STRIDE
    nsp = halves * SEL_BIAS_BLOCKS
    n_forced = 3
    rounds = min(SEL_TOPK, ns) - n_forced

    def block_masks(nb):
        blk = lax.broadcasted_iota(jnp.int32, (nb, Q_BLOCK), 0)
        cur = (t0 + lax.broadcasted_iota(jnp.int32, (nb, Q_BLOCK), 1)) // SEL_BLOCK
        return blk, blk <= cur, (blk == 0) | (blk == cur) | (blk == cur - 1)

    def free_scores(nb):
        r = [ps_ref[pl.ds(j, nb, stride=ratio), :] for j in range(ratio)]
        blk, valid, forced = block_masks(nb)
        prev = jnp.where(blk == 0, 0.0, pltpu.roll(r[ratio - 1], 1, 0))
        free = valid & jnp.logical_not(forced)
        return blk, free, jnp.where(free, 2.0 * (r[0] + r[1] + r[2]) + r[3] + prev, NEG_INF)

    def all_blocks(taken):
        nb = taken.shape[0]
        return taken if nb == nsp else jnp.concatenate([taken, jnp.zeros((nsp - nb, Q_BLOCK), F32)], axis=0)

    def fast_rounds(nb):
        _, free, score = free_scores(nb)
        for _ in range(rounds):
            score = jnp.where(score == jnp.max(score, axis=0, keepdims=True), -jnp.inf, score)
        removed = score == -jnp.inf
        n_removed = jnp.sum(jnp.where(removed & free, 1.0, 0.0), axis=0, keepdims=True)
        cur_row = (t0 + lax.broadcasted_iota(jnp.int32, (1, Q_BLOCK), 1)) // SEL_BLOCK
        n_free = jnp.clip(cur_row + 1 - n_forced, 0, rounds).astype(F32)
        return all_blocks(jnp.where(removed, 1.0, 0.0)), jnp.sum(jnp.where(n_removed == n_free, 0.0, 1.0))

    def tie_breaking_rounds():
        blk, _, score = free_scores(ns)
        blk_f, taken = blk.astype(F32), jnp.zeros((ns, Q_BLOCK), F32)
        for _ in range(rounds):
            top = jnp.max(score, axis=0, keepdims=True)
            first = jnp.min(jnp.where(score == top, blk_f, float(ns)), axis=0, keepdims=True)
            hit = blk_f == first
            taken = jnp.where(hit, 1.0, taken)
            score = jnp.where(hit, -jnp.inf, score)
        return all_blocks(taken)

    variants = [functools.partial(compressed, rows) for rows in range(step, ncp + 1, step)]
    reach = jnp.minimum((t0 + Q_BLOCK - CMP_LEN) // (CMP_STRIDE * step), len(variants) - 1)
    o_c, taken, n_tied = lax.switch(reach, variants) if len(variants) > 1 else variants[0]()
    taken = lax.cond(n_tied == 0.0, lambda: taken, tie_breaking_rounds)
    _, valid, forced = block_masks(nsp)
    bias = jnp.where((forced | (taken > 0.0)) & valid, 0.0, NEG_INF).astype(BF16)

    for h in range(halves):
        bh = bias[h * SEL_BIAS_BLOCKS:(h + 1) * SEL_BIAS_BLOCKS]
        for n in range(NSA_HPG):
            qaug_ref[h, HEAD_DIM:HEAD_DIM + SEL_BIAS_BLOCKS, n * Q_BLOCK:(n + 1) * Q_BLOCK] = bh

    init = (jnp.full((1, cols), NEG_INF, F32), jnp.zeros((1, cols), F32),
            jnp.zeros((HEAD_DIM, cols), F32))

    n_win = WINDOW // Q_BLOCK + 1
    win_keys = n_win * Q_BLOCK
    w0 = pl.multiple_of(jnp.maximum(t0 - WINDOW, 0), Q_BLOCK)
    newest = tq - w0
    krow = lax.broadcasted_iota(jnp.int32, (win_keys, cols), 0)
    s_w = jnp.where(krow <= newest, _dot(kw_ref[0, pl.ds(w0, win_keys), :], q_t), NEG_INF)
    oldest = lax.broadcasted_iota(jnp.int32, (Q_BLOCK, cols), 0) > newest - WINDOW
    s_w = jnp.concatenate([jnp.where(oldest, s_w[:Q_BLOCK], NEG_INF), s_w[Q_BLOCK:]], axis=0)
    p_w = jnp.exp2(s_w - jnp.max(s_w, axis=0, keepdims=True))
    vw_t = jnp.concatenate([vw_ref[0, w0 // Q_BLOCK + w] for w in range(n_win)], axis=1)
    o_w = _dot(vw_t, p_w.astype(BF16)) * (1.0 / jnp.sum(p_w, axis=0, keepdims=True))

    def sel_scores(j, s_ref, diagonal=False):
        k0 = pl.multiple_of(j * tk, tk)
        e0 = pl.multiple_of(k0 % SEL_BIAS_KEYS, tk)
        kaug = jnp.concatenate([ks_ref[0, pl.ds(k0, tk), :], e_ref[pl.ds(e0, tk), :]], axis=1)
        s = _dot(kaug, qaug_ref[k0 // SEL_BIAS_KEYS])
        if diagonal:
            lk = lax.broadcasted_iota(jnp.int32, (tk, cols), 0)
            lq = lax.broadcasted_iota(jnp.int32, (tk, cols), 1) & (Q_BLOCK - 1)
            s = jnp.where(lk - lq <= t0 - k0, s, NEG_INF)
        s_ref[...] = s
        return jnp.max(s, axis=0, keepdims=True)

    def sel_update(s_ref, s_max, j, carry):
        m, l, acc = carry
        m_new = jnp.maximum(m, s_max)
        alpha = jnp.exp2(m - m_new)
        p = jnp.exp2(s_ref[...] - m_new)
        l = alpha * l + jnp.sum(p, axis=0, keepdims=True)
        acc = alpha * acc + _dot(vs_ref[0, j], p.astype(BF16))
        return m_new, l, acc

    n_full = t0 // tk
    last_full = jnp.maximum(n_full - 1, 0)
    max_d = sel_scores(n_full, sd_ref, diagonal=True)
    max_a = sel_scores(0, sa_ref)

    def sel_pair(j, carry):
        max_a, state = carry
        max_b = sel_scores(j + 1, sb_ref)
        state = sel_update(sa_ref, max_a, j, state)
        max_a = sel_scores(jnp.minimum(j + 2, last_full), sa_ref)
        return max_a, sel_update(sb_ref, max_b, j + 1, state)

    unroll = 4
    carry = lax.fori_loop(0, n_full // unroll,
                          lambda jj, c: sel_pair(unroll * jj + 2, sel_pair(unroll * jj, c)), (max_a, init))

    def finish(rest, carry):
        max_cur, state = carry
        first = n_full - rest
        bufs = (sa_ref, sb_ref)
        for i in range(rest):
            if i + 1 < rest:
                max_next = sel_scores(first + i + 1, bufs[(i + 1) % 2])
            state = sel_update(bufs[i % 2], max_cur, first + i, state)
            if i + 1 < rest:
                max_cur = max_next
        _, l_s, acc_s = sel_update(sd_ref, max_d, n_full, state)
        o_s = acc_s * (1.0 / l_s)
        g_t = gate_ref[...].T
        for n in range(NSA_HPG):
            c = slice(n * Q_BLOCK, (n + 1) * Q_BLOCK)
            o = (g_t[n:n + 1] * o_c[:, c] + g_t[NSA_HPG + n:NSA_HPG + n + 1] * o_s[:, c]
                 + g_t[2 * NSA_HPG + n:2 * NSA_HPG + n + 1] * o_w[:, c])
            o_ref[:, n * HEAD_DIM:(n + 1) * HEAD_DIM] = o.T.astype(BF16)
        return jnp.int32(0)

    lax.switch(n_full % unroll, [functools.partial(finish, rest) for rest in range(unroll)], carry)


def _nsa_attention(queries, rotated, vs_t, vw_t, cmp, gates, onehot, batch, seq, *, tk):
    nq = seq // Q_BLOCK
    ns = seq // SEL_BLOCK
    ncp = seq // CMP_STRIDE
    halves = -(-ns // SEL_BIAS_BLOCKS)
    m = batch * seq
    once = pl.Buffered(1)

    def slab(first):
        return pl.BlockSpec((1, seq, HEAD_DIM), lambda b, g, qi: (first + g, b, 0), pipeline_mode=once)

    def slab_t(tile):
        return pl.BlockSpec((1, seq // tile, HEAD_DIM, tile), lambda b, g, qi: (g, b, 0, 0), pipeline_mode=once)

    def cmp_spec(which):
        return pl.BlockSpec((1, 1, 1, ncp // CMP_CHUNK, CMP_CHUNK, HEAD_DIM),
                            lambda b, g, qi: (which, b, g, 0, 0, 0))

    return pl.pallas_call(
        functools.partial(_attn_body, ns=ns, tk=tk),
        grid=(batch, NSA_KV_GROUPS, nq),
        in_specs=[
            pl.BlockSpec((NSA_HPG, 1, HEAD_DIM, Q_BLOCK), lambda b, g, qi: (g, b * nq + qi, 0, 0)),
            slab(ROT_KS), slab_t(tk), slab(ROT_KW), slab_t(Q_BLOCK),
            cmp_spec(0), cmp_spec(1),
            pl.BlockSpec((Q_BLOCK, LANES), lambda b, g, qi: (b * nq + qi, g)),
            pl.BlockSpec(onehot.shape, lambda b, g, qi: (0, 0), pipeline_mode=once),
        ],
        out_specs=pl.BlockSpec((Q_BLOCK, NSA_HPG * HEAD_DIM), lambda b, g, qi: (b * nq + qi, g)),
        out_shape=jax.ShapeDtypeStruct((m, NSA_HEADS * HEAD_DIM), BF16),
        scratch_shapes=[
            pltpu.VMEM((halves, 2 * HEAD_DIM, NSA_HPG * Q_BLOCK), BF16),
            pltpu.VMEM((ncp, Q_BLOCK), F32),
            pltpu.VMEM((tk, NSA_HPG * Q_BLOCK), F32),
            pltpu.VMEM((tk, NSA_HPG * Q_BLOCK), F32),
            pltpu.VMEM((tk, NSA_HPG * Q_BLOCK), F32),
        ],
        compiler_params=_params(("parallel", "parallel", "arbitrary")),
        name="nsa_attention",
    )(queries, rotated, vs_t, rotated, vw_t, cmp, cmp, gates, onehot)


def _sel_onehot(seq):
    keys = np.arange(min(seq, SEL_BIAS_KEYS))
    onehot = (keys[:, None] // SEL_BLOCK == np.arange(SEL_BIAS_BLOCKS)[None, :]).astype(np.float32)
    return jnp.asarray(onehot, BF16)


def _gate_weight(w_gl):
    d = w_gl.shape[0]
    w = w_gl.reshape(d, 3, NSA_KV_GROUPS, NSA_HPG).transpose(0, 2, 1, 3).reshape(d, NSA_KV_GROUPS, 3 * NSA_HPG)
    w = jnp.pad(w, ((0, 0), (0, 0), (0, LANES - 3 * NSA_HPG)))
    return w.reshape(d, NSA_KV_GROUPS * LANES)


def _nsa_layer(x, h, w_in, kc_pe, kc_w1, kc_w2, vc_pe, vc_w1, vc_w2, w_out, batch, seq, *, tk=1024):
    assert SEL_BLOCK == 4 * CMP_STRIDE and CMP_LEN == 2 * CMP_STRIDE and seq % max(tk, CMP_STRIDE * CMP_CHUNK) == 0
    queries, rotated, kr, gates, (vs_t, vw_t) = _nsa_projections(
        h, w_in, _rope_tables(jnp.arange(seq)), seq, tm=512, sel_tile=tk)
    rows = seq // CMP_STRIDE
    cmp_tables = _rope_tables(jnp.arange(rows) * CMP_STRIDE + CMP_LEN - 1)
    half = CMP_LEN // 2
    pe = jnp.stack([kc_pe, vc_pe]).reshape(2, 2, 1, half * HEAD_DIM)
    w1 = jnp.stack([kc_w1, vc_w1]).astype(BF16)
    w1 = w1.reshape(2, 2, half * HEAD_DIM, w1.shape[-1])
    w2 = jnp.stack([kc_w2, vc_w2]).astype(BF16)
    cmp = _compress(kr, pe, w1, w2, cmp_tables, batch, seq)
    att = _nsa_attention(queries, rotated, vs_t, vw_t, cmp, gates, _sel_onehot(seq), batch, seq, tk=tk)
    return _mm_residual(att, w_out.astype(BF16), x, tm=512)


def kernel(x, p, norm_mix, norm_ffn, norm_ple, ffn_up, ffn_down, ple_proj, ple_gate, gm_in, gm_ln_g, gm_ln_b, gm_ws, gm_bs, gm_out, nsa_in, nsa_kc_pe, nsa_kc_w1, nsa_kc_w2, nsa_vc_pe, nsa_vc_w1, nsa_vc_w2, nsa_out, final_norm):
    batch, seq, d = x.shape
    m = batch * seq
    depth = p.shape[0]
    xf = x.reshape(m, d)
    row = lambda v: v.reshape(1, -1)
    p_rows = p.reshape(depth, m, -1)
    ffn_up_bf, ffn_down_bf = ffn_up.astype(BF16), ffn_down.astype(BF16)
    ple_gate_bf, ple_proj_bf = ple_gate.astype(BF16), ple_proj.astype(BF16)
    for i in range(depth):
        j = i // 2
        if i % 2 == 0:
            z = _norm_gelu_mm(xf, row(norm_mix[i]), gm_in[j].astype(BF16), tm=512)
            xf = _gmlp_gate_out(z, xf, row(gm_ln_g[j]), row(gm_ln_b[j]), gm_ws[j], gm_bs[j].T,
                                gm_out[j].astype(BF16), tm=512)
        else:
            xf = _nsa_layer(xf, h_mix, nsa_in[j], nsa_kc_pe[j], nsa_kc_w1[j], nsa_kc_w2[j],
                            nsa_vc_pe[j], nsa_vc_w1[j], nsa_vc_w2[j], nsa_out[j], batch, seq)
        xf = _ffn(xf, row(norm_ffn[i]), ffn_up_bf, ffn_down_bf, i, tm=1024, tf=512)
        if i == depth - 1:
            post, post_g = "final", final_norm
        elif (i + 1) % 2 == 1:
            post, post_g = "next", norm_mix[i + 1]
        else:
            post, post_g = "none", final_norm
        out = _ple(xf, p_rows, row(norm_ple[i]), ple_gate_bf, ple_proj_bf, row(post_g), i, post=post, tm=512)
        xf, h_mix = out if post == "next" else (out, None)
    return xf.reshape(batch, seq, d)
```
